```python
import jax, jax.numpy as jnp
from jax import lax
import numpy as np

D_MODEL = 4096
BATCH = 4
SEQ = 2048
DEPTH = 1
DEC_BATCH = 128
DEC_SEQ = 4
PAST_LEN = 16384
PAGE_SIZE = 128

MIX_WIDTH = D_MODEL
RWKV_WIDTH = MIX_WIDTH // 2
HEAD_SIZE = 64
RWKV_HEADS = RWKV_WIDTH // HEAD_SIZE
DECAY_LORA = max(32, int(round(1.8 * RWKV_WIDTH ** 0.5 / 32)) * 32)
AAA_LORA = max(32, int(round(1.8 * RWKV_WIDTH ** 0.5 / 32)) * 32)
GATE_LORA = max(32, int(round(0.6 * RWKV_WIDTH ** 0.8 / 32)) * 32)
RWKV_PROJ = 3 * RWKV_WIDTH + DECAY_LORA + AAA_LORA + GATE_LORA
POOL_WIDTH = MIX_WIDTH - RWKV_WIDTH
POOL_WINDOWS = (2, 4, 8, 16)
POOL_GROUP = POOL_WIDTH // len(POOL_WINDOWS)
POOL_BUF = max(POOL_WINDOWS) - 1
IN_PROJ = RWKV_PROJ + POOL_WIDTH
N_MEM = 256
MEM_HEADS = 4
MEM_HEAD_DIM = D_MODEL // MEM_HEADS
D_FF = ((8 * D_MODEL // 3 + 255) // 256) * 256
CONV_W = 3
NORM_EPS = 1e-6
GN_EPS = 64e-5

kernel_name = "hybrid_rwkv7_pool_memxattn_convffn_step"


def _rmsnorm(x, g):
    xf = x.astype(jnp.float32)
    y = xf * lax.rsqrt(jnp.mean(xf * xf, axis=-1, keepdims=True) + NORM_EPS)
    return (y * g.astype(jnp.float32)).astype(x.dtype)


def _rwkv7_mix(p, prev_row, s0, mu_shift, w0, b_w, a0, b_a, b_g, k_k, k_a, r_k, gn_w, gn_b):
    B, T, _ = p.shape
    f32 = jnp.float32
    C = RWKV_WIDTH
    pf = p.astype(f32)
    p_prev = jnp.concatenate([prev_row[:, None, :].astype(f32), pf[:, :-1]], axis=1)
    xs = pf + (p_prev - pf) * mu_shift.astype(f32)
    r = xs[..., :C]
    k = xs[..., C:2 * C]
    v = xs[..., 2 * C:3 * C]
    o = 3 * C
    wd = xs[..., o:o + DECAY_LORA]
    o += DECAY_LORA
    ad = xs[..., o:o + AAA_LORA]
    o += AAA_LORA
    gd = xs[..., o:]
    w = -jax.nn.softplus(-(w0.astype(f32) + jnp.tanh(wd) @ b_w.astype(f32))) - 0.5
    decay = jnp.exp(-jnp.exp(w))
    a = jax.nn.sigmoid(a0.astype(f32) + ad @ b_a.astype(f32))
    g = jax.nn.sigmoid(gd) @ b_g.astype(f32)
    hs = (B, T, RWKV_HEADS, HEAD_SIZE)
    kk = (k * k_k.astype(f32)).reshape(hs)
    kk = kk / jnp.maximum(jnp.sqrt(jnp.sum(kk * kk, axis=-1, keepdims=True)), 1e-12)
    k = k * (1.0 + (a - 1.0) * k_a.astype(f32))
    r, k, v, a, decay = [t.reshape(hs) for t in (r, k, v, a, decay)]
    aa = -kk
    bb = kk * a

    def step(S, inp):
        r_t, w_t, k_t, v_t, a_t, b_t = inp
        sa = jnp.einsum('bhvk,bhk->bhv', S, a_t)
        S = S * w_t[:, :, None, :] + sa[..., None] * b_t[:, :, None, :] + v_t[..., None] * k_t[:, :, None, :]
        return S, jnp.einsum('bhvk,bhk->bhv', S, r_t)

    seq_first = lambda t: jnp.swapaxes(t, 0, 1)
    s_fin, y = lax.scan(step, s0.astype(f32), tuple(seq_first(t) for t in (r, decay, k, v, aa, bb)))
    y = seq_first(y)
    mean = jnp.mean(y, axis=-1, keepdims=True)
    var = jnp.mean(jnp.square(y - mean), axis=-1, keepdims=True)
    y = (y - mean) * lax.rsqrt(var + GN_EPS) * gn_w.astype(f32).reshape(RWKV_HEADS, HEAD_SIZE) \
        + gn_b.astype(f32).reshape(RWKV_HEADS, HEAD_SIZE)
    y = y + jnp.sum(r * k * r_k.astype(f32), axis=-1, keepdims=True) * v
    out = (y.reshape(B, T, C) * g).astype(p.dtype)
    return out, p[:, -1], s_fin


def _pool_mix(u, buf, pos0, w_pool, pool_scale):
    T = u.shape[1]
    ext = jnp.concatenate([buf.astype(u.dtype), u], axis=1)
    extf = ext.astype(jnp.float32)
    cs = jnp.cumsum(extf, axis=1)
    cs = jnp.concatenate([jnp.zeros_like(cs[:, :1]), cs], axis=1)
    end = cs[:, POOL_BUF + 1:]
    pos = pos0 + jnp.arange(T)
    outs = []
    for gi, win in enumerate(POOL_WINDOWS):
        sl = slice(gi * POOL_GROUP, (gi + 1) * POOL_GROUP)
        start = cs[:, POOL_BUF + 1 - win:POOL_BUF + 1 - win + T, sl]
        cnt = jnp.minimum(pos + 1, win).astype(jnp.float32)[None, :, None]
        d = (end[..., sl] - start) / cnt - extf[:, POOL_BUF:, sl]
        outs.append(jnp.einsum('btc,cd->btd', d.astype(u.dtype), w_pool[gi]))
    out = jnp.concatenate(outs, axis=-1) * pool_scale
    return out, ext[:, -POOL_BUF:]


def _mem_kv(mem, g_mem, w_mk, w_mv):
    B = mem.shape[0]
    m = _rmsnorm(mem, g_mem)
    k = (m @ w_mk).reshape(B, N_MEM, MEM_HEADS, MEM_HEAD_DIM)
    v = (m @ w_mv).reshape(B, N_MEM, MEM_HEADS, MEM_HEAD_DIM)
    return k, v


def _mem_attend(h, mem_k, mem_v, w_mq, w_mo):
    B, T, _ = h.shape
    q = (h @ w_mq).reshape(B, T, MEM_HEADS, MEM_HEAD_DIM)
    s = jnp.einsum('bthd,bmhd->bhtm', q, mem_k.astype(q.dtype)).astype(jnp.float32) * (MEM_HEAD_DIM ** -0.5)
    pr = jax.nn.softmax(s, axis=-1).astype(h.dtype)
    o = jnp.einsum('bhtm,bmhd->bthd', pr, mem_v.astype(h.dtype)).reshape(B, T, D_MODEL)
    return o @ w_mo


def _conv_ffn(h, buf, w_gate, w_val, conv_w, conv_b, w_down):
    T = h.shape[1]
    z = h @ w_gate
    ext = jnp.concatenate([buf.astype(z.dtype), z], axis=1)
    zc = conv_b
    for j in range(CONV_W):
        zc = zc + conv_w[j] * ext[:, j:j + T]
    act = jax.nn.silu(zc) * (h @ w_val)
    return act @ w_down, ext[:, -(CONV_W - 1):]


def _layer(x, pos0, shift_prev, wkv0, pool_buf, conv_buf, mem_k, mem_v, lw):
    (g_mix, w_in, mu_shift, w0, b_w, a0, b_a, b_g, k_k, k_a, r_k, gn_w, gn_b, w_pool, pool_scale, w_out,
     g_attn, w_mq, w_mo, g_ffn, w_gate, w_val, conv_w, conv_b, w_down) = lw
    proj = _rmsnorm(x, g_mix) @ w_in
    r_out, new_shift, new_wkv = _rwkv7_mix(proj[..., :RWKV_PROJ], shift_prev, wkv0, mu_shift, w0, b_w,
                                           a0, b_a, b_g, k_k, k_a, r_k, gn_w, gn_b)
    p_out, new_pool = _pool_mix(proj[..., RWKV_PROJ:], pool_buf, pos0, w_pool, pool_scale)
    x = x + jnp.concatenate([r_out, p_out], axis=-1) @ w_out
    x = x + _mem_attend(_rmsnorm(x, g_attn), mem_k, mem_v, w_mq, w_mo)
    f, new_conv = _conv_ffn(_rmsnorm(x, g_ffn), conv_buf, w_gate, w_val, conv_w, conv_b, w_down)
    return x + f, new_shift, new_wkv, new_pool, new_conv


def setup_inputs(seed: int = 0) -> dict:
    key = jax.random.key(seed)
    ks = iter(jax.random.split(key, 64))
    f32 = jnp.float32

    def nrm(shape, scale):
        return jax.random.normal(next(ks), shape, f32) * scale

    def uni(shape, lo, hi):
        return jax.random.uniform(next(ks), shape, f32, lo, hi)

    L, C, F = DEPTH, RWKV_WIDTH, D_FF
    return {
        'x_prompt': nrm((BATCH, SEQ, D_MODEL), 1.0),
        'x_sample': nrm((DEC_BATCH, DEC_SEQ, D_MODEL), 1.0),
        'mem_prompt': nrm((BATCH, N_MEM, D_MODEL), 1.0),
        'cache_mem_k': nrm((L, DEC_BATCH, N_MEM, MEM_HEADS, MEM_HEAD_DIM), 1.0),
        'cache_mem_v': nrm((L, DEC_BATCH, N_MEM, MEM_HEADS, MEM_HEAD_DIM), 1.0),
        'state_shift': nrm((L, DEC_BATCH, RWKV_PROJ), 1.0),
        'state_wkv': nrm((L, DEC_BATCH, RWKV_HEADS, HEAD_SIZE, HEAD_SIZE), 0.3),
        'state_pool': nrm((L, DEC_BATCH, POOL_BUF, POOL_WIDTH), 1.0),
        'state_conv': nrm((L, DEC_BATCH, CONV_W - 1, F), 1.0),
        'g_mix': 1.0 + nrm((L, D_MODEL), 0.05),
        'w_in': nrm((L, D_MODEL, IN_PROJ), D_MODEL ** -0.5),
        'mu_shift': uni((L, RWKV_PROJ), 0.0, 1.0),
        'w0': uni((L, C), -6.0, -1.0),
        'b_w': nrm((L, DECAY_LORA, C), DECAY_LORA ** -0.5),
        'a0': nrm((L, C), 0.1),
        'b_a': nrm((L, AAA_LORA, C), AAA_LORA ** -0.5),
        'b_g': nrm((L, GATE_LORA, C), GATE_LORA ** -0.5),
        'k_k': 0.85 + nrm((L, C), 0.05),
        'k_a': 1.0 + nrm((L, C), 0.05),
        'r_k': nrm((L, RWKV_HEADS, HEAD_SIZE), 0.1),
        'gn_w': 1.0 + nrm((L, C), 0.05),
        'gn_b': nrm((L, C), 0.01),
        'w_pool': nrm((L, len(POOL_WINDOWS), POOL_GROUP, POOL_GROUP), POOL_GROUP ** -0.5),
        'pool_scale': 1.0 + nrm((L, POOL_WIDTH), 0.05),
        'w_out': nrm((L, MIX_WIDTH, D_MODEL), MIX_WIDTH ** -0.5),
        'g_mem': 1.0 + nrm((L, D_MODEL), 0.05),
        'w_mk': nrm((L, D_MODEL, D_MODEL), D_MODEL ** -0.5),
        'w_mv': nrm((L, D_MODEL, D_MODEL), D_MODEL ** -0.5),
        'g_attn': 1.0 + nrm((L, D_MODEL), 0.05),
        'w_mq': nrm((L, D_MODEL, D_MODEL), D_MODEL ** -0.5),
        'w_mo': nrm((L, D_MODEL, D_MODEL), D_MODEL ** -0.5),
        'g_ffn': 1.0 + nrm((L, D_MODEL), 0.05),
        'w_gate': nrm((L, D_MODEL, F), D_MODEL ** -0.5),
        'w_val': nrm((L, D_MODEL, F), D_MODEL ** -0.5),
        'conv_w': nrm((L, CONV_W, F), CONV_W ** -0.5),
        'conv_b': nrm((L, F), 0.01),
        'w_down': nrm((L, F, D_MODEL), F ** -0.5),
        'g_final': 1.0 + nrm((D_MODEL,), 0.05),
    }


def reference(x_prompt, x_sample, mem_prompt, cache_mem_k, cache_mem_v, state_shift, state_wkv, state_pool,
              state_conv, g_mix, w_in, mu_shift, w0, b_w, a0, b_a, b_g, k_k, k_a, r_k, gn_w, gn_b, w_pool,
              pool_scale, w_out, g_mem, w_mk, w_mv, g_attn, w_mq, w_mo, g_ffn, w_gate, w_val, conv_w, conv_b,
              w_down, g_final):
    xp, xs = x_prompt, x_sample
    B = xp.shape[0]
    pm_k, pm_v, p_sh, p_wkv, p_pool, p_conv = [], [], [], [], [], []
    s_sh, s_wkv, s_pool, s_conv = [], [], [], []
    for l in range(DEPTH):
        lw = (g_mix[l], w_in[l], mu_shift[l], w0[l], b_w[l], a0[l], b_a[l], b_g[l], k_k[l], k_a[l], r_k[l],
              gn_w[l], gn_b[l], w_pool[l], pool_scale[l], w_out[l], g_attn[l], w_mq[l], w_mo[l], g_ffn[l],
              w_gate[l], w_val[l], conv_w[l], conv_b[l], w_down[l])
        mk, mv = _mem_kv(mem_prompt, g_mem[l], w_mk[l], w_mv[l])
        xp, sh, wk, po, co = _layer(
            xp, 0,
            jnp.zeros((B, RWKV_PROJ), xp.dtype),
            jnp.zeros((B, RWKV_HEADS, HEAD_SIZE, HEAD_SIZE), jnp.float32),
            jnp.zeros((B, POOL_BUF, POOL_WIDTH), xp.dtype),
            jnp.zeros((B, CONV_W - 1, D_FF), xp.dtype),
            mk, mv, lw)
        pm_k.append(mk.astype(cache_mem_k.dtype))
        pm_v.append(mv.astype(cache_mem_v.dtype))
        p_sh.append(sh.astype(state_shift.dtype))
        p_wkv.append(wk.astype(state_wkv.dtype))
        p_pool.append(po.astype(state_pool.dtype))
        p_conv.append(co.astype(state_conv.dtype))
        xs, sh, wk, po, co = _layer(
            xs, PAST_LEN, state_shift[l], state_wkv[l], state_pool[l], state_conv[l],
            cache_mem_k[l], cache_mem_v[l], lw)
        s_sh.append(sh.astype(state_shift.dtype))
        s_wkv.append(wk.astype(state_wkv.dtype))
        s_pool.append(po.astype(state_pool.dtype))
        s_conv.append(co.astype(state_conv.dtype))
    y_prompt = _rmsnorm(xp, g_final)
    y_sample = _rmsnorm(xs, g_final)
    return (y_prompt, y_sample,
            jnp.stack(pm_k), jnp.stack(pm_v), jnp.stack(p_sh), jnp.stack(p_wkv), jnp.stack(p_pool),
            jnp.stack(p_conv),
            jnp.stack(s_sh), jnp.stack(s_wkv), jnp.stack(s_pool), jnp.stack(s_conv))
```

```python
import functools

import jax
import jax.numpy as jnp
from jax import lax
from jax.experimental import pallas as pl
from jax.experimental.pallas import tpu as pltpu

F32 = jnp.float32
BF16 = jnp.bfloat16

HEAD_SIZE = 64
POOL_WINDOWS = (2, 4, 8, 16)
POOL_HIST = 16
MEM_HEADS = 4
CONV_W = 3
PAST_LEN = 16384
NORM_EPS = 1e-6
GN_EPS = 64e-5
LANE = 128
SUBLANE = 8
VMEM_LIMIT = 56 * 1024 * 1024


def _pick(dim, pref, mult=SUBLANE):
    if dim <= pref:
        return dim
    t = (pref // mult) * mult
    while t >= mult:
        if dim % t == 0:
            return t
        t -= mult
    return dim


def _round_up(x, m):
    return (x + m - 1) // m * m


def _params(*sem):
    return pltpu.CompilerParams(dimension_semantics=sem, vmem_limit_bytes=VMEM_LIMIT)


def _sigmoid(x):
    return 1.0 / (1.0 + jnp.exp(-x))


def _rmsnorm_kernel(x_ref, g_ref, o_ref):
    x = x_ref[...].astype(F32)
    ms = jnp.mean(x * x, axis=-1, keepdims=True)
    o_ref[...] = (x * lax.rsqrt(ms + NORM_EPS) * g_ref[...]).astype(o_ref.dtype)


def _rmsnorm(x, g, out_dtype):
    m, d = x.shape
    tr = _pick(m, 256)
    return pl.pallas_call(
        _rmsnorm_kernel,
        grid=(m // tr,),
        in_specs=[pl.BlockSpec((tr, d), lambda i: (i, 0)),
                  pl.BlockSpec((1, d), lambda i: (0, 0))],
        out_specs=pl.BlockSpec((tr, d), lambda i: (i, 0)),
        out_shape=jax.ShapeDtypeStruct((m, d), out_dtype),
        compiler_params=_params("arbitrary"),
        name="rmsnorm",
    )(x, g.reshape(1, d).astype(F32))


def _mm_kernel(*refs, nk, has_res):
    if has_res:
        x_ref, w_ref, res_ref, o_ref = refs[:4]
        scratch = refs[4:]
    else:
        x_ref, w_ref, o_ref = refs[:3]
        res_ref = None
        scratch = refs[3:]
    part = jnp.dot(x_ref[...].astype(BF16), w_ref[...], preferred_element_type=F32)
    if nk == 1:
        if has_res:
            part = part + res_ref[...]
        o_ref[...] = part.astype(o_ref.dtype)
        return
    acc_ref = scratch[0]
    k = pl.program_id(2)

    @pl.when(k == 0)
    def _():
        acc_ref[...] = part

    @pl.when(k > 0)
    def _():
        acc_ref[...] += part

    @pl.when(k == nk - 1)
    def _():
        r = acc_ref[...]
        if has_res:
            r = r + res_ref[...]
        o_ref[...] = r.astype(o_ref.dtype)


def _matmul(x, w, *, out_dtype, res=None, tm=1024, tn=512, tk=None, name="matmul"):
    m, kd = x.shape
    _, n = w.shape
    tm = _pick(m, tm)
    tn = _pick(n, tn, LANE)
    tk = kd if tk is None else _pick(kd, tk, LANE)
    nk = kd // tk
    in_specs = [pl.BlockSpec((tm, tk), lambda i, j, k: (i, k)),
                pl.BlockSpec((tk, tn), lambda i, j, k: (k, j))]
    args = [x, w]
    if res is not None:
        in_specs.append(pl.BlockSpec((tm, tn), lambda i, j, k: (i, j)))
        args.append(res)
    scratch = [pltpu.VMEM((tm, tn), F32)] if nk > 1 else []
    return pl.pallas_call(
        functools.partial(_mm_kernel, nk=nk, has_res=res is not None),
        grid=(m // tm, n // tn, nk),
        in_specs=in_specs,
        out_specs=pl.BlockSpec((tm, tn), lambda i, j, k: (i, j)),
        out_shape=jax.ShapeDtypeStruct((m, n), out_dtype),
        scratch_shapes=scratch,
        compiler_params=_params("arbitrary", "arbitrary", "arbitrary"),
        name=name,
    )(*args)


def _mm2_kernel(x1_ref, x2_ref, w1_ref, w2_ref, res_ref, o_ref):
    acc = jnp.dot(x1_ref[...], w1_ref[...], preferred_element_type=F32)
    acc = acc + jnp.dot(x2_ref[...], w2_ref[...], preferred_element_type=F32)
    o_ref[...] = (acc + res_ref[...]).astype(o_ref.dtype)


def _matmul2_res(x1, x2, w1, w2, res, *, tm=1024, tn=512):
    m, k1 = x1.shape
    _, k2 = x2.shape
    _, n = w1.shape
    tm = _pick(m, tm)
    tn = _pick(n, tn, LANE)
    return pl.pallas_call(
        _mm2_kernel,
        grid=(m // tm, n // tn),
        in_specs=[pl.BlockSpec((tm, k1), lambda i, j: (i, 0)),
                  pl.BlockSpec((tm, k2), lambda i, j: (i, 0)),
                  pl.BlockSpec((k1, tn), lambda i, j: (0, j)),
                  pl.BlockSpec((k2, tn), lambda i, j: (0, j)),
                  pl.BlockSpec((tm, tn), lambda i, j: (i, j))],
        out_specs=pl.BlockSpec((tm, tn), lambda i, j: (i, j)),
        out_shape=jax.ShapeDtypeStruct((m, n), F32),
        compiler_params=_params("arbitrary", "arbitrary"),
        name="out_proj",
    )(x1, x2, w1, w2, res)


def _prep_kernel(p_ref, st_ref, mu_ref, w0_ref, a0_ref, bw_ref, ba_ref, bg_ref,
                 r_ref, k_ref, v_ref, d_ref, a_ref, g_ref, ext_ref, *, halo, tt, shift, c):
    @pl.when(pl.program_id(1) == 0)
    def _():
        ext_ref[0:halo, :] = st_ref[...]

    p = p_ref[...]
    ext_ref[halo:halo + tt, :] = p
    prev = ext_ref[halo - shift:halo - shift + tt, :]
    xs = p + (prev - p) * mu_ref[...]
    ext_ref[0:halo, :] = ext_ref[tt:tt + halo, :]

    r_ref[...] = xs[:, 0:c]
    k_ref[...] = xs[:, c:2 * c]
    v_ref[...] = xs[:, 2 * c:3 * c]
    lo = xs[:, 3 * c:]
    wpre = jnp.dot(jnp.tanh(lo).astype(BF16), bw_ref[...], preferred_element_type=F32)
    apre = jnp.dot(lo.astype(BF16), ba_ref[...], preferred_element_type=F32)
    g_ref[...] = jnp.dot(_sigmoid(lo).astype(BF16), bg_ref[...], preferred_element_type=F32)
    z = -(w0_ref[...] + wpre)
    softplus = jnp.maximum(z, 0.0) + jnp.log(1.0 + jnp.exp(-jnp.abs(z)))
    d_ref[...] = jnp.exp(-jnp.exp(-softplus - 0.5))
    a_ref[...] = _sigmoid(a0_ref[...] + apre)


def _rwkv_prep(proj, state, mu, w0, a0, bw, ba, bg, *, nseq, shift, halo, tt, c):
    m, ca = proj.shape
    rows = m // nseq
    tt = _pick(rows, tt)
    nt = rows // tt
    row_spec = lambda width: pl.BlockSpec((tt, width), lambda s, j: (s * nt + j, 0))
    full = lambda arr: pl.BlockSpec(arr.shape, lambda s, j: (0,) * arr.ndim)
    out = jax.ShapeDtypeStruct((m, c), F32)
    return pl.pallas_call(
        functools.partial(_prep_kernel, halo=halo, tt=tt, shift=shift, c=c),
        grid=(nseq, nt),
        in_specs=[row_spec(ca),
                  pl.BlockSpec((None, halo, ca), lambda s, j: (s, 0, 0)),
                  full(mu), full(w0), full(a0), full(bw), full(ba), full(bg)],
        out_specs=[row_spec(c)] * 6,
        out_shape=[out] * 6,
        scratch_shapes=[pltpu.VMEM((halo + tt, ca), F32)],
        compiler_params=_params("arbitrary", "arbitrary"),
        name="rwkv_prep",
    )(proj, state, mu, w0, a0, bw, ba, bg)


def _wkv_kernel(r_ref, k_ref, v_ref, d_ref, a_ref, s0_ref, kk_ref, ka_ref, rk_ref, gw_ref, gb_ref,
                o_ref, sf_ref, s_ref, y_ref, *, tt, n):
    @pl.when(pl.program_id(1) == 0)
    def _():
        s_ref[...] = s0_ref[...]

    k_k = kk_ref[...]
    k_a = ka_ref[...]
    r_k = rk_ref[...]
    gn_w = gw_ref[...]
    gn_b = gb_ref[...]
    sub = lax.broadcasted_iota(jnp.int32, (SUBLANE, 1), 0)
    inv_n = 1.0 / n

    def token(t, carry):
        r = r_ref[t]
        k = k_ref[t]
        v = v_ref[t]
        w = d_ref[t]
        a = a_ref[t]
        kk = k * k_k
        norm = jnp.sqrt(jnp.sum(kk * kk, axis=0, keepdims=True))
        kk = kk / jnp.maximum(norm, 1e-12)
        k2 = k * (1.0 + (a - 1.0) * k_a)
        av = -kk
        bv = kk * a
        wr = w * r
        b_r = jnp.sum(bv * r, axis=0, keepdims=True)
        k_r = jnp.sum(k2 * r, axis=0, keepdims=True)
        bonus = jnp.sum(r * k2 * r_k, axis=0, keepdims=True)

        def vblock(g8, c2):
            base = pl.multiple_of(g8 * SUBLANE, SUBLANE)
            v8 = v_ref[t, pl.ds(base, SUBLANE), :]
            y8 = jnp.zeros_like(v8)
            for jj in range(SUBLANE):
                s_v = s_ref[base + jj]
                sa = jnp.sum(s_v * av, axis=0, keepdims=True)
                q = jnp.sum(s_v * wr, axis=0, keepdims=True)
                vv = v8[jj:jj + 1, :]
                s_ref[base + jj] = s_v * w + sa * bv + vv * k2
                y8 = jnp.where(sub == jj, q + sa * b_r + vv * k_r, y8)
            y_ref[pl.ds(base, SUBLANE), :] = y8
            return c2

        lax.fori_loop(0, n // SUBLANE, vblock, 0)
        y = y_ref[...]
        mean = jnp.sum(y, axis=0, keepdims=True) * inv_n
        yc = y - mean
        var = jnp.sum(yc * yc, axis=0, keepdims=True) * inv_n
        o_ref[t] = yc * lax.rsqrt(var + GN_EPS) * gn_w + gn_b + bonus * v
        return carry

    lax.fori_loop(0, tt, token, 0)

    @pl.when(pl.program_id(1) == pl.num_programs(1) - 1)
    def _():
        sf_ref[...] = s_ref[...]


def _wkv(r, k, v, d, a, s0, k_k, k_a, r_k, gn_w, gn_b, *, tt):
    g, t, n, lanes = r.shape
    tt = _pick(t, tt, 1)
    tok = pl.BlockSpec((None, tt, n, lanes), lambda i, j: (i, j, 0, 0))
    st = pl.BlockSpec((None, n, n, lanes), lambda i, j: (i, 0, 0, 0))
    gp = k_k.shape[0]
    par = pl.BlockSpec((None, n, lanes), (lambda i, j: (i, 0, 0)) if gp > 1 else (lambda i, j: (0, 0, 0)))
    return pl.pallas_call(
        functools.partial(_wkv_kernel, tt=tt, n=n),
        grid=(g, t // tt),
        in_specs=[tok] * 5 + [st] + [par] * 5,
        out_specs=[tok, st],
        out_shape=[jax.ShapeDtypeStruct((g, t, n, lanes), F32),
                   jax.ShapeDtypeStruct((g, n, n, lanes), F32)],
        scratch_shapes=[pltpu.VMEM((n, n, lanes), F32), pltpu.VMEM((n, lanes), F32)],
        compiler_params=_params("arbitrary", "arbitrary"),
        name="wkv_recurrence",
    )(r, k, v, d, a, s0, k_k, k_a, r_k, gn_w, gn_b)


def _gate_kernel(y_ref, g_ref, o_ref):
    o_ref[...] = (y_ref[...] * g_ref[...]).astype(o_ref.dtype)


def _gate(y, g):
    m, c = y.shape
    tr = _pick(m, 512)
    spec = pl.BlockSpec((tr, c), lambda i: (i, 0))
    return pl.pallas_call(
        _gate_kernel, grid=(m // tr,), in_specs=[spec, spec], out_specs=spec,
        out_shape=jax.ShapeDtypeStruct((m, c), BF16),
        compiler_params=_params("arbitrary"), name="wkv_gate",
    )(y, g)


def _pool_kernel(u_ref, h_ref, w_ref, sc_ref, o_ref, ext_ref, *, halo, tt, shift, pos0):
    gi = pl.program_id(1)
    j = pl.program_id(2)

    @pl.when(j == 0)
    def _():
        ext_ref[0:halo, :] = h_ref[...]

    x = u_ref[...]
    ext_ref[halo:halo + tt, :] = x
    row = lax.broadcasted_iota(jnp.int32, (tt, 1), 0) + j * tt
    time = row if shift == 1 else row // shift

    for g, win in enumerate(POOL_WINDOWS):
        @pl.when(gi == g)
        def _(win=win):
            acc = x
            for back in range(1, win):
                lo = halo - back * shift
                acc = acc + ext_ref[lo:lo + tt, :]
            cnt = jnp.minimum(pos0 + time + 1, win).astype(F32)
            d = acc / cnt - x
            y = jnp.dot(d.astype(BF16), w_ref[...], preferred_element_type=F32)
            o_ref[...] = (y * sc_ref[...]).astype(o_ref.dtype)

    if tt >= halo:
        ext_ref[0:halo, :] = ext_ref[tt:tt + halo, :]


def _pool(u, hist, w_pool, scale, *, nseq, shift, tt, pos0):
    m, cp = u.shape
    ng = len(POOL_WINDOWS)
    gw = cp // ng
    rows = m // nseq
    tt = _pick(rows, tt)
    nt = rows // tt
    halo = POOL_HIST * shift
    assert nt == 1 or tt >= halo
    return pl.pallas_call(
        functools.partial(_pool_kernel, halo=halo, tt=tt, shift=shift, pos0=pos0),
        grid=(nseq, ng, nt),
        in_specs=[pl.BlockSpec((tt, gw), lambda s, g, j: (s * nt + j, g)),
                  pl.BlockSpec((None, halo, gw), lambda s, g, j: (s, 0, g)),
                  pl.BlockSpec((None, gw, gw), lambda s, g, j: (g, 0, 0)),
                  pl.BlockSpec((1, gw), lambda s, g, j: (0, g))],
        out_specs=pl.BlockSpec((tt, gw), lambda s, g, j: (s * nt + j, g)),
        out_shape=jax.ShapeDtypeStruct((m, cp), BF16),
        scratch_shapes=[pltpu.VMEM((halo + tt, gw), F32)],
        compiler_params=_params("arbitrary", "arbitrary", "arbitrary"),
        name="pool_mix",
    )(u, hist, w_pool, scale)


def _attend(q, k, v, scale):
    s = lax.dot_general(q, k, (((1,), (1,)), ((), ())), preferred_element_type=F32) * scale
    s = s - jnp.max(s, axis=-1, keepdims=True)
    e = jnp.exp(s)
    pr = e / jnp.sum(e, axis=-1, keepdims=True)
    return jnp.dot(pr.astype(BF16), v, preferred_element_type=F32)


def _attn_prompt_kernel(q_ref, k_ref, v_ref, o_ref, *, scale):
    o = _attend(q_ref[...], k_ref[...].astype(BF16), v_ref[...].astype(BF16), scale)
    o_ref[...] = o.astype(o_ref.dtype)


def _attn_prompt(q, mk, mv, *, batch, heads):
    m, d = q.shape
    t = m // batch
    nm = mk.shape[1]
    hd = d // heads
    tq = _pick(t, 512)
    nq = t // tq
    return pl.pallas_call(
        functools.partial(_attn_prompt_kernel, scale=hd ** -0.5),
        grid=(batch, nq, heads),
        in_specs=[pl.BlockSpec((tq, hd), lambda b, i, h: (b * nq + i, h)),
                  pl.BlockSpec((None, nm, hd), lambda b, i, h: (b, 0, h)),
                  pl.BlockSpec((None, nm, hd), lambda b, i, h: (b, 0, h))],
        out_specs=pl.BlockSpec((tq, hd), lambda b, i, h: (b * nq + i, h)),
        out_shape=jax.ShapeDtypeStruct((m, d), BF16),
        compiler_params=_params("arbitrary", "arbitrary", "arbitrary"),
        name="attn_prompt",
    )(q, mk, mv)


def _attn_sample_kernel(q_ref, k_ref, v_ref, o_ref, *, scale, heads, hd):
    for h in range(heads):
        cols = slice(h * hd, (h + 1) * hd)
        o = _attend(q_ref[:, cols].astype(BF16), k_ref[:, cols].astype(BF16),
                    v_ref[:, cols].astype(BF16), scale)
        o_ref[:, cols] = o.astype(o_ref.dtype)


def _attn_sample(q, mk, mv, *, heads):
    b, t, d = q.shape
    nm = mk.shape[1]
    hd = d // heads
    return pl.pallas_call(
        functools.partial(_attn_sample_kernel, scale=hd ** -0.5, heads=heads, hd=hd),
        grid=(b,),
        in_specs=[pl.BlockSpec((None, t, d), lambda i: (i, 0, 0)),
                  pl.BlockSpec((None, nm, d), lambda i: (i, 0, 0)),
                  pl.BlockSpec((None, nm, d), lambda i: (i, 0, 0))],
        out_specs=pl.BlockSpec((None, t, d), lambda i: (i, 0, 0)),
        out_shape=jax.ShapeDtypeStruct((b, t, d), F32),
        compiler_params=_params("arbitrary"),
        name="attn_sample",
    )(q, mk, mv)


def _ffn_up_kernel(h_ref, hp_ref, wg_ref, wv_ref, cw_ref, cb_ref, st_ref, act_ref, zt_ref, zext_ref,
                   *, halo, tm, shift, recompute_halo):
    h = h_ref[...]
    z = jnp.dot(h, wg_ref[...], preferred_element_type=F32)
    val = jnp.dot(h, wv_ref[...], preferred_element_type=F32)
    zprev = st_ref[...]
    if recompute_halo:
        zhalo = jnp.dot(hp_ref[...], wg_ref[...], preferred_element_type=F32)
        zprev = jnp.where(pl.program_id(1) == 0, zprev, zhalo)
    zext_ref[0:halo, :] = zprev
    zext_ref[halo:halo + tm, :] = z
    zm1 = zext_ref[halo - shift:halo - shift + tm, :]
    zm2 = zext_ref[halo - 2 * shift:halo - 2 * shift + tm, :]
    zc = cb_ref[...] + cw_ref[0:1, :] * zm2 + cw_ref[1:2, :] * zm1 + cw_ref[2:3, :] * z
    act_ref[...] = (zc * _sigmoid(zc) * val).astype(act_ref.dtype)
    zt_ref[...] = zext_ref[tm:tm + halo, :]


def _ffn_up(h, wg, wv, cw, cb, state, *, nseq, shift, halo, tm, tn):
    m, d = h.shape
    _, fp = wg.shape
    rows = m // nseq
    tm = _pick(rows, tm)
    nt = rows // tm
    tn = _pick(fp, tn, LANE)
    assert tm >= halo and (nt == 1 or halo == SUBLANE)
    hb = tm // SUBLANE
    return pl.pallas_call(
        functools.partial(_ffn_up_kernel, halo=halo, tm=tm, shift=shift, recompute_halo=nt > 1),
        grid=(nseq, nt, fp // tn),
        in_specs=[pl.BlockSpec((tm, d), lambda s, i, j: (s * nt + i, 0)),
                  pl.BlockSpec((SUBLANE, d), lambda s, i, j: (jnp.maximum((s * nt + i) * hb - 1, 0), 0)),
                  pl.BlockSpec((d, tn), lambda s, i, j: (0, j)),
                  pl.BlockSpec((d, tn), lambda s, i, j: (0, j)),
                  pl.BlockSpec((CONV_W, tn), lambda s, i, j: (0, j)),
                  pl.BlockSpec((1, tn), lambda s, i, j: (0, j)),
                  pl.BlockSpec((None, halo, tn), lambda s, i, j: (s, 0, j))],
        out_specs=[pl.BlockSpec((tm, tn), lambda s, i, j: (s * nt + i, j)),
                   pl.BlockSpec((halo, tn), lambda s, i, j: (s * nt + i, j))],
        out_shape=[jax.ShapeDtypeStruct((m, fp), BF16),
                   jax.ShapeDtypeStruct((nseq * nt * halo, fp), F32)],
        scratch_shapes=[pltpu.VMEM((halo + tm, tn), F32)],
        compiler_params=_params("arbitrary", "arbitrary", "arbitrary"),
        name="ffn_up",
    )(h, h, wg, wv, cw, cb, state)


def _to_chains_prompt(x, b, t, h):
    return x.reshape(b, t, h, HEAD_SIZE).transpose(1, 3, 0, 2).reshape(1, t, HEAD_SIZE, b * h)


def _from_chains_prompt(x, b, t, h):
    return x.reshape(t, HEAD_SIZE, b, h).transpose(2, 0, 3, 1).reshape(b * t, h * HEAD_SIZE)


def _to_chains_sample(x, b, t, h):
    return x.reshape(t, b, h, HEAD_SIZE).transpose(2, 0, 3, 1)


def _from_chains_sample(x, b, t, h):
    return x.transpose(1, 3, 0, 2).reshape(t * b, h * HEAD_SIZE)


def kernel(x_prompt, x_sample, mem_prompt, cache_mem_k, cache_mem_v, state_shift, state_wkv, state_pool,
           state_conv, g_mix, w_in, mu_shift, w0, b_w, a0, b_a, b_g, k_k, k_a, r_k, gn_w, gn_b, w_pool,
           pool_scale, w_out, g_mem, w_mk, w_mv, g_attn, w_mq, w_mo, g_ffn, w_gate, w_val, conv_w, conv_b,
           w_down, g_final):
    bp, tp, d = x_prompt.shape
    bs, ts, _ = x_sample.shape
    depth = w_in.shape[0]
    c = w0.shape[1]
    nh = c // HEAD_SIZE
    cp = pool_scale.shape[1]
    pw = mu_shift.shape[1]
    lora = pw - 3 * c
    lp = _round_up(lora, LANE)
    ca = 3 * c + lp
    dl, al, gl = b_w.shape[1], b_a.shape[1], b_g.shape[1]
    f = w_gate.shape[2]
    fp = _round_up(f, 512)
    nmem = mem_prompt.shape[1]
    nbuf = state_pool.shape[2]
    mp, ms = bp * tp, bs * ts

    xp = x_prompt.reshape(mp, d)
    xs = x_sample.transpose(1, 0, 2).reshape(ms, d)

    outs = {n: [] for n in ("pm_k", "pm_v", "p_sh", "p_wkv", "p_pool", "p_conv",
                            "s_sh", "s_wkv", "s_pool", "s_conv")}

    for l in range(depth):
        wa = jnp.pad(w_in[l][:, :pw], ((0, 0), (0, ca - pw))).astype(BF16)
        wu = w_in[l][:, pw:].astype(BF16)
        mu = jnp.pad(mu_shift[l], (0, ca - pw)).reshape(1, ca)
        bw = jnp.zeros((lp, c), F32).at[0:dl].set(b_w[l]).astype(BF16)
        ba = jnp.zeros((lp, c), F32).at[dl:dl + al].set(b_a[l]).astype(BF16)
        bg = jnp.zeros((lp, c), F32).at[dl + al:dl + al + gl].set(b_g[l]).astype(BF16)
        w0l = w0[l].reshape(1, c)
        a0l = a0[l].reshape(1, c)
        wpool = w_pool[l].astype(BF16)
        pscale = pool_scale[l].reshape(1, cp)
        wo1 = w_out[l][:c].astype(BF16)
        wo2 = w_out[l][c:].astype(BF16)
        wmk = w_mk[l].astype(BF16)
        wmv = w_mv[l].astype(BF16)
        wmq = w_mq[l].astype(BF16)
        wmo = w_mo[l].astype(BF16)
        wg = jnp.pad(w_gate[l], ((0, 0), (0, fp - f))).astype(BF16)
        wv = jnp.pad(w_val[l], ((0, 0), (0, fp - f))).astype(BF16)
        wd = jnp.pad(w_down[l], ((0, fp - f), (0, 0))).astype(BF16)
        cw = jnp.pad(conv_w[l], ((0, 0), (0, fp - f)))
        cb = jnp.pad(conv_b[l], (0, fp - f)).reshape(1, fp)

        def head_tile(p):
            return p.reshape(nh, HEAD_SIZE).T

        chain_p = [jnp.tile(head_tile(p), (1, bp)).reshape(1, HEAD_SIZE, bp * nh)
                   for p in (k_k[l], k_a[l], r_k[l].reshape(-1), gn_w[l], gn_b[l])]
        chain_s = [jnp.broadcast_to(p.reshape(nh, HEAD_SIZE, 1), (nh, HEAD_SIZE, bs))
                   for p in (k_k[l], k_a[l], r_k[l].reshape(-1), gn_w[l], gn_b[l])]

        mn = _rmsnorm(mem_prompt.reshape(bp * nmem, d), g_mem[l], BF16)
        mk = _matmul(mn, wmk, out_dtype=F32, name="mem_k")
        mv = _matmul(mn, wmv, out_dtype=F32, name="mem_v")
        outs["pm_k"].append(mk.reshape(bp, nmem, MEM_HEADS, d // MEM_HEADS))
        outs["pm_v"].append(mv.reshape(bp, nmem, MEM_HEADS, d // MEM_HEADS))

        def layer(x, *, prompt):
            m = x.shape[0]
            if prompt:
                nseq, shift, halo1, halo, pos0 = bp, 1, SUBLANE, SUBLANE, 0
                st_shift = jnp.zeros((bp, halo1, ca), F32)
                hist = jnp.zeros((bp, POOL_HIST, cp), F32)
                st_conv = jnp.zeros((bp, halo, fp), F32)
            else:
                nseq, shift, halo1, halo, pos0 = 1, bs, bs, (CONV_W - 1) * bs, PAST_LEN
                st_shift = jnp.pad(state_shift[l], ((0, 0), (0, ca - pw))).reshape(1, halo1, ca)
                hist = jnp.pad(state_pool[l].transpose(1, 0, 2), ((POOL_HIST - nbuf, 0), (0, 0), (0, 0)))
                hist = hist.reshape(1, POOL_HIST * bs, cp)
                st_conv = jnp.pad(state_conv[l].transpose(1, 0, 2), ((0, 0), (0, 0), (0, fp - f)))
                st_conv = st_conv.reshape(1, halo, fp)

            xn = _rmsnorm(x, g_mix[l], BF16)
            proj = _matmul(xn, wa, out_dtype=F32, name="in_proj_rwkv")
            u = _matmul(xn, wu, out_dtype=F32, name="in_proj_pool")
            r, k, v, dec, a, g = _rwkv_prep(proj, st_shift, mu, w0l, a0l, bw, ba, bg,
                                            nseq=nseq, shift=shift, halo=halo1, tt=128, c=c)
            if prompt:
                to_c = functools.partial(_to_chains_prompt, b=bp, t=tp, h=nh)
                s0 = jnp.zeros((1, HEAD_SIZE, HEAD_SIZE, bp * nh), F32)
                y, s_fin = _wkv(to_c(r), to_c(k), to_c(v), to_c(dec), to_c(a), s0, *chain_p, tt=32)
                y = _from_chains_prompt(y, bp, tp, nh)
                new_wkv = s_fin.reshape(HEAD_SIZE, HEAD_SIZE, bp, nh).transpose(2, 3, 0, 1)
            else:
                to_c = functools.partial(_to_chains_sample, b=bs, t=ts, h=nh)
                s0 = state_wkv[l].transpose(1, 2, 3, 0)
                y, s_fin = _wkv(to_c(r), to_c(k), to_c(v), to_c(dec), to_c(a), s0, *chain_s, tt=ts)
                y = _from_chains_sample(y, bs, ts, nh)
                new_wkv = s_fin.transpose(3, 0, 1, 2)
            r_out = _gate(y, g)
            p_out = _pool(u, hist, wpool, pscale, nseq=nseq, shift=shift,
                          tt=256 if prompt else m, pos0=pos0)
            x1 = _matmul2_res(r_out, p_out, wo1, wo2, x)

            hq = _rmsnorm(x1, g_attn[l], BF16)
            if prompt:
                q = _matmul(hq, wmq, out_dtype=BF16, name="attn_q")
                o = _attn_prompt(q, mk.reshape(bp, nmem, d), mv.reshape(bp, nmem, d),
                                 batch=bp, heads=MEM_HEADS)
            else:
                q = _matmul(hq, wmq, out_dtype=F32, name="attn_q")
                q = q.reshape(ts, bs, d).transpose(1, 0, 2)
                o = _attn_sample(q, cache_mem_k[l].reshape(bs, nmem, d), cache_mem_v[l].reshape(bs, nmem, d),
                                 heads=MEM_HEADS)
                o = o.transpose(1, 0, 2).reshape(m, d)
            x2 = _matmul(o, wmo, out_dtype=F32, res=x1, name="attn_o")

            hf = _rmsnorm(x2, g_ffn[l], BF16)
            act, ztail = _ffn_up(hf, wg, wv, cw, cb, st_conv, nseq=nseq, shift=shift, halo=halo,
                                 tm=1024, tn=512)
            x3 = _matmul(act, wd, out_dtype=F32, res=x2, tn=1024, tk=2816, name="ffn_down")

            if prompt:
                new_shift = proj.reshape(bp, tp, ca)[:, -1, :pw]
                new_pool = u.reshape(bp, tp, cp)[:, tp - nbuf:]
                new_conv = ztail.reshape(bp, -1, halo, fp)[:, -1, halo - (CONV_W - 1):, :f]
            else:
                new_shift = proj[(ts - 1) * bs:, :pw]
                u_b = u.reshape(ts, bs, cp).transpose(1, 0, 2)
                new_pool = jnp.concatenate([state_pool[l], u_b], axis=1)[:, -nbuf:]
                new_conv = ztail.reshape(CONV_W - 1, bs, fp)[:, :, :f].transpose(1, 0, 2)
            return x3, new_shift, new_wkv, new_pool, new_conv

        xp, sh, wk, po, co = layer(xp, prompt=True)
        outs["p_sh"].append(sh)
        outs["p_wkv"].append(wk)
        outs["p_pool"].append(po)
        outs["p_conv"].append(co)
        xs, sh, wk, po, co = layer(xs, prompt=False)
        outs["s_sh"].append(sh)
        outs["s_wkv"].append(wk)
        outs["s_pool"].append(po)
        outs["s_conv"].append(co)

    y_prompt = _rmsnorm(xp, g_final, F32).reshape(bp, tp, d)
    y_sample = _rmsnorm(xs, g_final, F32).reshape(ts, bs, d).transpose(1, 0, 2)
    stk = lambda n: jnp.stack(outs[n])
    return (y_prompt, y_sample, stk("pm_k"), stk("pm_v"), stk("p_sh"), stk("p_wkv"), stk("p_pool"),
            stk("p_conv"), stk("s_sh"), stk("s_wkv"), stk("s_pool"), stk("s_conv"))
```

```python
import functools

import jax
import jax.numpy as jnp
from jax import lax
from jax.experimental import pallas as pl
from jax.experimental.pallas import tpu as pltpu

F32 = jnp.float32
BF16 = jnp.bfloat16

HEAD_SIZE = 64
POOL_WINDOWS = (2, 4, 8, 16)
POOL_HIST = 16
MEM_HEADS = 4
CONV_W = 3
PAST_LEN = 16384
NORM_EPS = 1e-6
GN_EPS = 64e-5
LANE = 128
SUBLANE = 8
VMEM_LIMIT = 56 * 1024 * 1024


def _pick(dim, pref, mult=SUBLANE):
    if dim <= pref:
        return dim
    t = (pref // mult) * mult
    while t >= mult:
        if dim % t == 0:
            return t
        t -= mult
    return dim


def _round_up(x, m):
    return (x + m - 1) // m * m


def _params(*sem):
    return pltpu.CompilerParams(dimension_semantics=sem, vmem_limit_bytes=VMEM_LIMIT)


def _sigmoid(x):
    return 1.0 / (1.0 + jnp.exp(-x))


def _rmsnorm_kernel(x_ref, g_ref, o_ref):
    x = x_ref[...].astype(F32)
    ms = jnp.mean(x * x, axis=-1, keepdims=True)
    o_ref[...] = (x * lax.rsqrt(ms + NORM_EPS) * g_ref[...]).astype(o_ref.dtype)


def _rmsnorm(x, g, out_dtype):
    m, d = x.shape
    tr = _pick(m, 256)
    return pl.pallas_call(
        _rmsnorm_kernel,
        grid=(m // tr,),
        in_specs=[pl.BlockSpec((tr, d), lambda i: (i, 0)),
                  pl.BlockSpec((1, d), lambda i: (0, 0))],
        out_specs=pl.BlockSpec((tr, d), lambda i: (i, 0)),
        out_shape=jax.ShapeDtypeStruct((m, d), out_dtype),
        compiler_params=_params("arbitrary"),
        name="rmsnorm",
    )(x, g.reshape(1, d).astype(F32))


def _mm_kernel(*refs, nk, has_res):
    if has_res:
        x_ref, w_ref, res_ref, o_ref = refs[:4]
        scratch = refs[4:]
    else:
        x_ref, w_ref, o_ref = refs[:3]
        res_ref = None
        scratch = refs[3:]
    part = jnp.dot(x_ref[...].astype(BF16), w_ref[...], preferred_element_type=F32)
    if nk == 1:
        if has_res:
            part = part + res_ref[...]
        o_ref[...] = part.astype(o_ref.dtype)
        return
    acc_ref = scratch[0]
    k = pl.program_id(2)

    @pl.when(k == 0)
    def _():
        acc_ref[...] = part

    @pl.when(k > 0)
    def _():
        acc_ref[...] += part

    @pl.when(k == nk - 1)
    def _():
        r = acc_ref[...]
        if has_res:
            r = r + res_ref[...]
        o_ref[...] = r.astype(o_ref.dtype)


def _matmul(x, w, *, out_dtype, res=None, tm=1024, tn=512, tk=None, name="matmul"):
    m, kd = x.shape
    _, n = w.shape
    tm = _pick(m, tm)
    tn = _pick(n, tn, LANE)
    tk = kd if tk is None else _pick(kd, tk, LANE)
    nk = kd // tk
    in_specs = [pl.BlockSpec((tm, tk), lambda i, j, k: (i, k)),
                pl.BlockSpec((tk, tn), lambda i, j, k: (k, j))]
    args = [x, w]
    if res is not None:
        in_specs.append(pl.BlockSpec((tm, tn), lambda i, j, k: (i, j)))
        args.append(res)
    scratch = [pltpu.VMEM((tm, tn), F32)] if nk > 1 else []
    return pl.pallas_call(
        functools.partial(_mm_kernel, nk=nk, has_res=res is not None),
        grid=(m // tm, n // tn, nk),
        in_specs=in_specs,
        out_specs=pl.BlockSpec((tm, tn), lambda i, j, k: (i, j)),
        out_shape=jax.ShapeDtypeStruct((m, n), out_dtype),
        scratch_shapes=scratch,
        compiler_params=_params("arbitrary", "arbitrary", "arbitrary"),
        name=name,
    )(*args)


def _mm2_kernel(x1_ref, x2_ref, w1_ref, w2_ref, res_ref, o_ref):
    acc = jnp.dot(x1_ref[...], w1_ref[...], preferred_element_type=F32)
    acc = acc + jnp.dot(x2_ref[...], w2_ref[...], preferred_element_type=F32)
    o_ref[...] = (acc + res_ref[...]).astype(o_ref.dtype)


def _matmul2_res(x1, x2, w1, w2, res, *, tm=1024, tn=512):
    m, k1 = x1.shape
    _, k2 = x2.shape
    _, n = w1.shape
    tm = _pick(m, tm)
    tn = _pick(n, tn, LANE)
    return pl.pallas_call(
        _mm2_kernel,
        grid=(m // tm, n // tn),
        in_specs=[pl.BlockSpec((tm, k1), lambda i, j: (i, 0)),
                  pl.BlockSpec((tm, k2), lambda i, j: (i, 0)),
                  pl.BlockSpec((k1, tn), lambda i, j: (0, j)),
                  pl.BlockSpec((k2, tn), lambda i, j: (0, j)),
                  pl.BlockSpec((tm, tn), lambda i, j: (i, j))],
        out_specs=pl.BlockSpec((tm, tn), lambda i, j: (i, j)),
        out_shape=jax.ShapeDtypeStruct((m, n), F32),
        compiler_params=_params("arbitrary", "arbitrary"),
        name="out_proj",
    )(x1, x2, w1, w2, res)


def _prep_kernel(p_ref, st_ref, mu_ref, w0_ref, a0_ref, bw_ref, ba_ref, bg_ref,
                 r_ref, k_ref, v_ref, d_ref, a_ref, g_ref, ext_ref, *, halo, tt, shift, c):
    @pl.when(pl.program_id(1) == 0)
    def _():
        ext_ref[0:halo, :] = st_ref[...]

    p = p_ref[...]
    ext_ref[halo:halo + tt, :] = p
    prev = ext_ref[halo - shift:halo - shift + tt, :]
    xs = p + (prev - p) * mu_ref[...]
    ext_ref[0:halo, :] = ext_ref[tt:tt + halo, :]

    r_ref[...] = xs[:, 0:c].T
    k_ref[...] = xs[:, c:2 * c].T
    v_ref[...] = xs[:, 2 * c:3 * c].T
    lo = xs[:, 3 * c:]
    wpre = jnp.dot(jnp.tanh(lo).astype(BF16), bw_ref[...], preferred_element_type=F32)
    apre = jnp.dot(lo.astype(BF16), ba_ref[...], preferred_element_type=F32)
    g_ref[...] = jnp.dot(_sigmoid(lo).astype(BF16), bg_ref[...], preferred_element_type=F32)
    z = -(w0_ref[...] + wpre)
    softplus = jnp.maximum(z, 0.0) + jnp.log(1.0 + jnp.exp(-jnp.abs(z)))
    d_ref[...] = jnp.exp(-jnp.exp(-softplus - 0.5)).T
    a_ref[...] = _sigmoid(a0_ref[...] + apre).T


def _rwkv_prep(proj, state, mu, w0, a0, bw, ba, bg, *, nseq, shift, halo, tt, c):
    m, ca = proj.shape
    rows = m // nseq
    assert rows % tt == 0
    nt = rows // tt
    row_spec = lambda width: pl.BlockSpec((tt, width), lambda s, j: (s * nt + j, 0))
    tr_spec = pl.BlockSpec((None, c, tt), lambda s, j: (s * nt + j, 0, 0))
    full = lambda arr: pl.BlockSpec(arr.shape, lambda s, j: (0,) * arr.ndim)
    out_t = jax.ShapeDtypeStruct((nseq * nt, c, tt), F32)
    return pl.pallas_call(
        functools.partial(_prep_kernel, halo=halo, tt=tt, shift=shift, c=c),
        grid=(nseq, nt),
        in_specs=[row_spec(ca),
                  pl.BlockSpec((None, halo, ca), lambda s, j: (s, 0, 0)),
                  full(mu), full(w0), full(a0), full(bw), full(ba), full(bg)],
        out_specs=[tr_spec] * 5 + [row_spec(c)],
        out_shape=[out_t] * 5 + [jax.ShapeDtypeStruct((m, c), F32)],
        scratch_shapes=[pltpu.VMEM((halo + tt, ca), F32)],
        compiler_params=_params("arbitrary", "arbitrary"),
        name="rwkv_prep",
    )(proj, state, mu, w0, a0, bw, ba, bg)


V_GROUP = 4
K_BLOCK = 32


def _wkv_kernel(r_ref, k_ref, v_ref, d_ref, a_ref, s0_ref, kk_ref, ka_ref, rk_ref, gw_ref, gb_ref,
                o_ref, sf_ref, s_ref, y_ref, row_ref, *, tt, n):
    @pl.when(pl.program_id(1) == 0)
    def _():
        s_ref[...] = s0_ref[...]

    k_k = kk_ref[...]
    k_a = ka_ref[...]
    r_k = rk_ref[...]
    gn_w = gw_ref[...]
    gn_b = gb_ref[...]
    inv_n = 1.0 / n
    lanes = s_ref.shape[-1]

    def token(t, carry):
        r = r_ref[t]
        k = k_ref[t]
        v = v_ref[t]
        w = d_ref[t]
        a = a_ref[t]
        kk = k * k_k
        norm = jnp.sqrt(jnp.sum(kk * kk, axis=0, keepdims=True))
        kk = kk / jnp.maximum(norm, 1e-12)
        k2 = k * (1.0 + (a - 1.0) * k_a)
        av = -kk
        bv = kk * a
        wr = w * r
        b_r = jnp.sum(bv * r, axis=0, keepdims=True)
        k_r = jnp.sum(k2 * r, axis=0, keepdims=True)
        bonus = jnp.sum(r * k2 * r_k, axis=0, keepdims=True)

        row_ref[0] = av
        row_ref[1] = wr
        row_ref[2] = w
        row_ref[3] = bv
        row_ref[4] = k2

        def vgroup(gidx, c2):
            vb0 = gidx * V_GROUP
            zero = jnp.zeros((SUBLANE, lanes), F32)

            def reduce_keys(kb, acc):
                sa_acc, q_acc = list(acc[0]), list(acc[1])
                for j in range(K_BLOCK):
                    kx = kb * K_BLOCK + j
                    a_row = row_ref[0, pl.ds(kx, 1), :]
                    q_row = row_ref[1, pl.ds(kx, 1), :]
                    for i in range(V_GROUP):
                        s_k = s_ref[vb0 + i, kx]
                        sa_acc[i] = sa_acc[i] + s_k * a_row
                        q_acc[i] = q_acc[i] + s_k * q_row
                return tuple(sa_acc), tuple(q_acc)

            sa8, q8 = lax.fori_loop(0, n // K_BLOCK, reduce_keys, ((zero,) * V_GROUP, (zero,) * V_GROUP))
            base = pl.multiple_of(vb0 * SUBLANE, SUBLANE)
            v8 = [v_ref[t, pl.ds(base + i * SUBLANE, SUBLANE), :] for i in range(V_GROUP)]

            def update_keys(kb, c3):
                for j in range(K_BLOCK):
                    kx = kb * K_BLOCK + j
                    w_row = row_ref[2, pl.ds(kx, 1), :]
                    b_row = row_ref[3, pl.ds(kx, 1), :]
                    k_row = row_ref[4, pl.ds(kx, 1), :]
                    for i in range(V_GROUP):
                        s_ref[vb0 + i, kx] = s_ref[vb0 + i, kx] * w_row + sa8[i] * b_row + v8[i] * k_row
                return c3

            lax.fori_loop(0, n // K_BLOCK, update_keys, 0)
            for i in range(V_GROUP):
                y_ref[pl.ds(base + i * SUBLANE, SUBLANE), :] = q8[i] + sa8[i] * b_r + v8[i] * k_r
            return c2

        lax.fori_loop(0, n // (SUBLANE * V_GROUP), vgroup, 0)
        y = y_ref[...]
        mean = jnp.sum(y, axis=0, keepdims=True) * inv_n
        yc = y - mean
        var = jnp.sum(yc * yc, axis=0, keepdims=True) * inv_n
        o_ref[t] = yc * lax.rsqrt(var + GN_EPS) * gn_w + gn_b + bonus * v
        return carry

    lax.fori_loop(0, tt, token, 0)

    @pl.when(pl.program_id(1) == pl.num_programs(1) - 1)
    def _():
        sf_ref[...] = s_ref[...]


def _wkv(r, k, v, d, a, s0, k_k, k_a, r_k, gn_w, gn_b, *, tt, group_major):
    if group_major:
        g, t, n, lanes = r.shape
    else:
        t, g, n, lanes = r.shape
    tt = _pick(t, tt, 1)
    nb = n // SUBLANE
    if group_major:
        tok = pl.BlockSpec((None, tt, n, lanes), lambda i, j: (i, j, 0, 0))
    else:
        tok = pl.BlockSpec((tt, None, n, lanes), lambda i, j: (j, i, 0, 0))
    st = pl.BlockSpec((None, nb, n, SUBLANE, lanes), lambda i, j: (i, 0, 0, 0, 0))
    gp = k_k.shape[0]
    par = pl.BlockSpec((None, n, lanes), (lambda i, j: (i, 0, 0)) if gp > 1 else (lambda i, j: (0, 0, 0)))
    return pl.pallas_call(
        functools.partial(_wkv_kernel, tt=tt, n=n),
        grid=(g, t // tt),
        in_specs=[tok] * 5 + [st] + [par] * 5,
        out_specs=[tok, st],
        out_shape=[jax.ShapeDtypeStruct(r.shape, F32),
                   jax.ShapeDtypeStruct((g, nb, n, SUBLANE, lanes), F32)],
        scratch_shapes=[pltpu.VMEM((nb, n, SUBLANE, lanes), F32), pltpu.VMEM((n, lanes), F32),
                        pltpu.VMEM((5, n, lanes), F32)],
        compiler_params=_params("arbitrary", "arbitrary"),
        name="wkv_recurrence",
    )(r, k, v, d, a, s0, k_k, k_a, r_k, gn_w, gn_b)


def _to_chain_kernel(x_ref, o_ref, *, nb, nh, n, tt):
    for k in range(n):
        rows = jnp.concatenate([x_ref[b, pl.ds(k, nh, stride=n), :] for b in range(nb)], axis=0)
        o_ref[pl.ds(k, tt, stride=n), :] = rows.T


def _to_chain(xt, *, nb, nh):
    tiles, c, tt = xt.shape
    nt = tiles // nb
    n = c // nh
    return pl.pallas_call(
        functools.partial(_to_chain_kernel, nb=nb, nh=nh, n=n, tt=tt),
        grid=(nt,),
        in_specs=[pl.BlockSpec((nb, None, c, tt), lambda j: (0, j, 0, 0))],
        out_specs=pl.BlockSpec((tt * n, nb * nh), lambda j: (j, 0)),
        out_shape=jax.ShapeDtypeStruct((nt * tt * n, nb * nh), F32),
        compiler_params=_params("arbitrary"),
        name="to_chain",
    )(xt.reshape(nb, nt, c, tt))


def _from_chain_gate_kernel(y_ref, g_ref, o_ref, xt_ref, *, nb, nh, n, tt):
    for v in range(n):
        yv = y_ref[pl.ds(v, tt, stride=n), :].T
        for b in range(nb):
            xt_ref[b, pl.ds(v, nh, stride=n), :] = yv[b * nh:(b + 1) * nh, :]
    for b in range(nb):
        o_ref[b] = (xt_ref[b].T * g_ref[b]).astype(o_ref.dtype)


def _from_chain_gate(y, g, *, nb, nh, tt):
    m, c = g.shape
    t = m // nb
    n = c // nh
    nt = t // tt
    row = pl.BlockSpec((nb, tt, c), lambda j: (0, j, 0))
    out = pl.pallas_call(
        functools.partial(_from_chain_gate_kernel, nb=nb, nh=nh, n=n, tt=tt),
        grid=(nt,),
        in_specs=[pl.BlockSpec((tt * n, nb * nh), lambda j: (j, 0)), row],
        out_specs=row,
        out_shape=jax.ShapeDtypeStruct((nb, t, c), BF16),
        scratch_shapes=[pltpu.VMEM((nb, c, tt), F32)],
        compiler_params=_params("arbitrary"),
        name="from_chain_gate",
    )(y, g.reshape(nb, t, c))
    return out.reshape(m, c)


def _transpose_gate_kernel(y_ref, g_ref, o_ref):
    o_ref[...] = (y_ref[...].T * g_ref[...]).astype(o_ref.dtype)


def _transpose_gate(yt, g):
    tiles, c, tt = yt.shape
    return pl.pallas_call(
        _transpose_gate_kernel,
        grid=(tiles,),
        in_specs=[pl.BlockSpec((None, c, tt), lambda i: (i, 0, 0)),
                  pl.BlockSpec((tt, c), lambda i: (i, 0))],
        out_specs=pl.BlockSpec((tt, c), lambda i: (i, 0)),
        out_shape=jax.ShapeDtypeStruct((tiles * tt, c), BF16),
        compiler_params=_params("arbitrary"),
        name="transpose_gate",
    )(yt, g)


def _pool_kernel(u_ref, h_ref, w_ref, sc_ref, o_ref, ext_ref, *, halo, tt, shift, pos0):
    gi = pl.program_id(1)
    j = pl.program_id(2)

    @pl.when(j == 0)
    def _():
        ext_ref[0:halo, :] = h_ref[...]

    x = u_ref[...]
    ext_ref[halo:halo + tt, :] = x
    row = lax.broadcasted_iota(jnp.int32, (tt, 1), 0) + j * tt
    time = row if shift == 1 else row // shift

    for g, win in enumerate(POOL_WINDOWS):
        @pl.when(gi == g)
        def _(win=win):
            acc = x
            for back in range(1, win):
                lo = halo - back * shift
                acc = acc + ext_ref[lo:lo + tt, :]
            cnt = jnp.minimum(pos0 + time + 1, win).astype(F32)
            d = acc / cnt - x
            y = jnp.dot(d.astype(BF16), w_ref[...], preferred_element_type=F32)
            o_ref[...] = (y * sc_ref[...]).astype(o_ref.dtype)

    if tt >= halo:
        ext_ref[0:halo, :] = ext_ref[tt:tt + halo, :]


def _pool(u, hist, w_pool, scale, *, nseq, shift, tt, pos0):
    m, cp = u.shape
    ng = len(POOL_WINDOWS)
    gw = cp // ng
    rows = m // nseq
    tt = _pick(rows, tt)
    nt = rows // tt
    halo = POOL_HIST * shift
    assert nt == 1 or tt >= halo
    return pl.pallas_call(
        functools.partial(_pool_kernel, halo=halo, tt=tt, shift=shift, pos0=pos0),
        grid=(nseq, ng, nt),
        in_specs=[pl.BlockSpec((tt, gw), lambda s, g, j: (s * nt + j, g)),
                  pl.BlockSpec((None, halo, gw), lambda s, g, j: (s, 0, g)),
                  pl.BlockSpec((None, gw, gw), lambda s, g, j: (g, 0, 0)),
                  pl.BlockSpec((1, gw), lambda s, g, j: (0, g))],
        out_specs=pl.BlockSpec((tt, gw), lambda s, g, j: (s * nt + j, g)),
        out_shape=jax.ShapeDtypeStruct((m, cp), BF16),
        scratch_shapes=[pltpu.VMEM((halo + tt, gw), F32)],
        compiler_params=_params("arbitrary", "arbitrary", "arbitrary"),
        name="pool_mix",
    )(u, hist, w_pool, scale)


def _attend(q, k, v, scale):
    s = lax.dot_general(q, k, (((1,), (1,)), ((), ())), preferred_element_type=F32) * scale
    s = s - jnp.max(s, axis=-1, keepdims=True)
    e = jnp.exp(s)
    pr = e / jnp.sum(e, axis=-1, keepdims=True)
    return jnp.dot(pr.astype(BF16), v, preferred_element_type=F32)


def _attn_prompt_kernel(q_ref, k_ref, v_ref, o_ref, *, scale):
    o = _attend(q_ref[...], k_ref[...].astype(BF16), v_ref[...].astype(BF16), scale)
    o_ref[...] = o.astype(o_ref.dtype)


def _attn_prompt(q, mk, mv, *, batch, heads):
    m, d = q.shape
    t = m // batch
    nm = mk.shape[1]
    hd = d // heads
    tq = _pick(t, 512)
    nq = t // tq
    return pl.pallas_call(
        functools.partial(_attn_prompt_kernel, scale=hd ** -0.5),
        grid=(batch, nq, heads),
        in_specs=[pl.BlockSpec((tq, hd), lambda b, i, h: (b * nq + i, h)),
                  pl.BlockSpec((None, nm, hd), lambda b, i, h: (b, 0, h)),
                  pl.BlockSpec((None, nm, hd), lambda b, i, h: (b, 0, h))],
        out_specs=pl.BlockSpec((tq, hd), lambda b, i, h: (b * nq + i, h)),
        out_shape=jax.ShapeDtypeStruct((m, d), BF16),
        compiler_params=_params("arbitrary", "arbitrary", "arbitrary"),
        name="attn_prompt",
    )(q, mk, mv)


def _attn_sample_kernel(q_ref, k_ref, v_ref, o_ref, *, scale, heads, hd):
    for h in range(heads):
        cols = slice(h * hd, (h + 1) * hd)
        o = _attend(q_ref[:, cols].astype(BF16), k_ref[:, cols], v_ref[:, cols], scale)
        o_ref[:, cols] = o.astype(o_ref.dtype)


def _attn_sample(q, mk, mv, *, heads):
    b, t, d = q.shape
    nm = mk.shape[1]
    hd = d // heads
    return pl.pallas_call(
        functools.partial(_attn_sample_kernel, scale=hd ** -0.5, heads=heads, hd=hd),
        grid=(b,),
        in_specs=[pl.BlockSpec((None, t, d), lambda i: (i, 0, 0)),
                  pl.BlockSpec((None, nm, d), lambda i: (i, 0, 0)),
                  pl.BlockSpec((None, nm, d), lambda i: (i, 0, 0))],
        out_specs=pl.BlockSpec((None, t, d), lambda i: (i, 0, 0)),
        out_shape=jax.ShapeDtypeStruct((b, t, d), F32),
        compiler_params=_params("arbitrary"),
        name="attn_sample",
    )(q, mk, mv)


def _ffn_up_kernel(h_ref, hp_ref, wg_ref, wv_ref, cw_ref, cb_ref, st_ref, act_ref, zt_ref, zext_ref,
                   *, halo, tm, shift, recompute_halo):
    h = h_ref[...]
    z = jnp.dot(h, wg_ref[...], preferred_element_type=F32)
    val = jnp.dot(h, wv_ref[...], preferred_element_type=F32)
    zprev = st_ref[...]
    if recompute_halo:
        zhalo = jnp.dot(hp_ref[...], wg_ref[...], preferred_element_type=F32)
        zprev = jnp.where(pl.program_id(1) == 0, zprev, zhalo)
    zext_ref[0:halo, :] = zprev
    zext_ref[halo:halo + tm, :] = z
    zm1 = zext_ref[halo - shift:halo - shift + tm, :]
    zm2 = zext_ref[halo - 2 * shift:halo - 2 * shift + tm, :]
    zc = cb_ref[...] + cw_ref[0:1, :] * zm2 + cw_ref[1:2, :] * zm1 + cw_ref[2:3, :] * z
    act_ref[...] = (zc * _sigmoid(zc) * val).astype(act_ref.dtype)
    zt_ref[...] = zext_ref[tm:tm + halo, :]


def _ffn_up(h, wg, wv, cw, cb, state, *, nseq, shift, halo, tm, tn):
    m, d = h.shape
    _, fp = wg.shape
    rows = m // nseq
    tm = _pick(rows, tm)
    nt = rows // tm
    tn = _pick(fp, tn, LANE)
    assert tm >= halo and (nt == 1 or halo == SUBLANE)
    hb = tm // SUBLANE
    return pl.pallas_call(
        functools.partial(_ffn_up_kernel, halo=halo, tm=tm, shift=shift, recompute_halo=nt > 1),
        grid=(nseq, nt, fp // tn),
        in_specs=[pl.BlockSpec((tm, d), lambda s, i, j: (s * nt + i, 0)),
                  pl.BlockSpec((SUBLANE, d), lambda s, i, j: (jnp.maximum((s * nt + i) * hb - 1, 0), 0)),
                  pl.BlockSpec((d, tn), lambda s, i, j: (0, j)),
                  pl.BlockSpec((d, tn), lambda s, i, j: (0, j)),
                  pl.BlockSpec((CONV_W, tn), lambda s, i, j: (0, j)),
                  pl.BlockSpec((1, tn), lambda s, i, j: (0, j)),
                  pl.BlockSpec((None, halo, tn), lambda s, i, j: (s, 0, j))],
        out_specs=[pl.BlockSpec((tm, tn), lambda s, i, j: (s * nt + i, j)),
                   pl.BlockSpec((halo, tn), lambda s, i, j: (s * nt + i, j))],
        out_shape=[jax.ShapeDtypeStruct((m, fp), BF16),
                   jax.ShapeDtypeStruct((nseq * nt * halo, fp), F32)],
        scratch_shapes=[pltpu.VMEM((halo + tm, tn), F32)],
        compiler_params=_params("arbitrary", "arbitrary", "arbitrary"),
        name="ffn_up",
    )(h, h, wg, wv, cw, cb, state)


def kernel(x_prompt, x_sample, mem_prompt, cache_mem_k, cache_mem_v, state_shift, state_wkv, state_pool,
           state_conv, g_mix, w_in, mu_shift, w0, b_w, a0, b_a, b_g, k_k, k_a, r_k, gn_w, gn_b, w_pool,
           pool_scale, w_out, g_mem, w_mk, w_mv, g_attn, w_mq, w_mo, g_ffn, w_gate, w_val, conv_w, conv_b,
           w_down, g_final):
    bp, tp, d = x_prompt.shape
    bs, ts, _ = x_sample.shape
    depth = w_in.shape[0]
    c = w0.shape[1]
    nh = c // HEAD_SIZE
    nvb = HEAD_SIZE // SUBLANE
    cp = pool_scale.shape[1]
    pw = mu_shift.shape[1]
    lora = pw - 3 * c
    lp = _round_up(lora, LANE)
    ca = 3 * c + lp
    dl, al, gl = b_w.shape[1], b_a.shape[1], b_g.shape[1]
    f = w_gate.shape[2]
    fp = _round_up(f, 512)
    nmem = mem_prompt.shape[1]
    nbuf = state_pool.shape[2]
    mp, ms = bp * tp, bs * ts

    xp = x_prompt.reshape(mp, d)
    xs = x_sample.transpose(1, 0, 2).reshape(ms, d)

    outs = {n: [] for n in ("pm_k", "pm_v", "p_sh", "p_wkv", "p_pool", "p_conv",
                            "s_sh", "s_wkv", "s_pool", "s_conv")}

    for l in range(depth):
        wa = jnp.pad(w_in[l][:, :pw], ((0, 0), (0, ca - pw))).astype(BF16)
        wu = w_in[l][:, pw:].astype(BF16)
        mu = jnp.pad(mu_shift[l], (0, ca - pw)).reshape(1, ca)
        bw = jnp.zeros((lp, c), F32).at[0:dl].set(b_w[l]).astype(BF16)
        ba = jnp.zeros((lp, c), F32).at[dl:dl + al].set(b_a[l]).astype(BF16)
        bg = jnp.zeros((lp, c), F32).at[dl + al:dl + al + gl].set(b_g[l]).astype(BF16)
        w0l = w0[l].reshape(1, c)
        a0l = a0[l].reshape(1, c)
        wpool = w_pool[l].astype(BF16)
        pscale = pool_scale[l].reshape(1, cp)
        wo1 = w_out[l][:c].astype(BF16)
        wo2 = w_out[l][c:].astype(BF16)
        wmk = w_mk[l].astype(BF16)
        wmv = w_mv[l].astype(BF16)
        wmq = w_mq[l].astype(BF16)
        wmo = w_mo[l].astype(BF16)
        wg = jnp.pad(w_gate[l], ((0, 0), (0, fp - f))).astype(BF16)
        wv = jnp.pad(w_val[l], ((0, 0), (0, fp - f))).astype(BF16)
        wd = jnp.pad(w_down[l], ((0, fp - f), (0, 0))).astype(BF16)
        cw = jnp.pad(conv_w[l], ((0, 0), (0, fp - f)))
        cb = jnp.pad(conv_b[l], (0, fp - f)).reshape(1, fp)

        def head_tile(p):
            return p.reshape(nh, HEAD_SIZE).T

        chain_p = [jnp.tile(head_tile(p), (1, bp)).reshape(1, HEAD_SIZE, bp * nh)
                   for p in (k_k[l], k_a[l], r_k[l].reshape(-1), gn_w[l], gn_b[l])]
        chain_s = [jnp.broadcast_to(p.reshape(nh, HEAD_SIZE, 1), (nh, HEAD_SIZE, bs))
                   for p in (k_k[l], k_a[l], r_k[l].reshape(-1), gn_w[l], gn_b[l])]

        mn = _rmsnorm(mem_prompt.reshape(bp * nmem, d), g_mem[l], BF16)
        mk = _matmul(mn, wmk, out_dtype=F32, name="mem_k")
        mv = _matmul(mn, wmv, out_dtype=F32, name="mem_v")
        outs["pm_k"].append(mk.reshape(bp, nmem, MEM_HEADS, d // MEM_HEADS))
        outs["pm_v"].append(mv.reshape(bp, nmem, MEM_HEADS, d // MEM_HEADS))

        def layer(x, *, prompt):
            m = x.shape[0]
            if prompt:
                nseq, shift, halo1, halo, pos0 = bp, 1, SUBLANE, SUBLANE, 0
                st_shift = jnp.zeros((bp, halo1, ca), F32)
                hist = jnp.zeros((bp, POOL_HIST, cp), F32)
                st_conv = jnp.zeros((bp, halo, fp), F32)
            else:
                nseq, shift, halo1, halo, pos0 = 1, bs, bs, (CONV_W - 1) * bs, PAST_LEN
                st_shift = jnp.pad(state_shift[l], ((0, 0), (0, ca - pw))).reshape(1, halo1, ca)
                hist = jnp.pad(state_pool[l].transpose(1, 0, 2), ((POOL_HIST - nbuf, 0), (0, 0), (0, 0)))
                hist = hist.reshape(1, POOL_HIST * bs, cp)
                st_conv = jnp.pad(state_conv[l].transpose(1, 0, 2), ((0, 0), (0, 0), (0, fp - f)))
                st_conv = st_conv.reshape(1, halo, fp)

            xn = _rmsnorm(x, g_mix[l], BF16)
            proj = _matmul(xn, wa, out_dtype=F32, name="in_proj_rwkv")
            u = _matmul(xn, wu, out_dtype=F32, name="in_proj_pool")
            tile = min(LANE, tp) if prompt else bs
            *rkvda, g = _rwkv_prep(proj, st_shift, mu, w0l, a0l, bw, ba, bg,
                                   nseq=nseq, shift=shift, halo=halo1, tt=tile, c=c)
            if prompt:
                chains = [_to_chain(x, nb=bp, nh=nh).reshape(1, tp, HEAD_SIZE, bp * nh) for x in rkvda]
                s0 = jnp.zeros((1, nvb, HEAD_SIZE, SUBLANE, bp * nh), F32)
                y, s_fin = _wkv(*chains, s0, *chain_p, tt=32, group_major=True)
                r_out = _from_chain_gate(y.reshape(tp * HEAD_SIZE, bp * nh), g, nb=bp, nh=nh, tt=tile)
                new_wkv = s_fin.reshape(nvb, HEAD_SIZE, SUBLANE, bp, nh).transpose(3, 4, 0, 2, 1)
                new_wkv = new_wkv.reshape(bp, nh, HEAD_SIZE, HEAD_SIZE)
            else:
                chains = [x.reshape(ts, nh, HEAD_SIZE, bs) for x in rkvda]
                s0 = state_wkv[l].reshape(bs, nh, nvb, SUBLANE, HEAD_SIZE).transpose(1, 2, 4, 3, 0)
                y, s_fin = _wkv(*chains, s0, *chain_s, tt=ts, group_major=False)
                r_out = _transpose_gate(y.reshape(ts, c, bs), g)
                new_wkv = s_fin.transpose(4, 0, 1, 3, 2).reshape(bs, nh, HEAD_SIZE, HEAD_SIZE)
            p_out = _pool(u, hist, wpool, pscale, nseq=nseq, shift=shift,
                          tt=256 if prompt else m, pos0=pos0)
            x1 = _matmul2_res(r_out, p_out, wo1, wo2, x)

            hq = _rmsnorm(x1, g_attn[l], BF16)
            if prompt:
                q = _matmul(hq, wmq, out_dtype=BF16, name="attn_q")
                o = _attn_prompt(q, mk.reshape(bp, nmem, d), mv.reshape(bp, nmem, d),
                                 batch=bp, heads=MEM_HEADS)
            else:
                q = _matmul(hq, wmq, out_dtype=F32, name="attn_q")
                q = q.reshape(ts, bs, d).transpose(1, 0, 2)
                o = _attn_sample(q, cache_mem_k[l].astype(BF16).reshape(bs, nmem, d),
                                 cache_mem_v[l].astype(BF16).reshape(bs, nmem, d), heads=MEM_HEADS)
                o = o.transpose(1, 0, 2).reshape(m, d)
            x2 = _matmul(o, wmo, out_dtype=F32, res=x1, name="attn_o")

            hf = _rmsnorm(x2, g_ffn[l], BF16)
            act, ztail = _ffn_up(hf, wg, wv, cw, cb, st_conv, nseq=nseq, shift=shift, halo=halo,
                                 tm=1024, tn=512)
            x3 = _matmul(act, wd, out_dtype=F32, res=x2, tn=1024, tk=2816, name="ffn_down")

            if prompt:
                new_shift = proj.reshape(bp, tp, ca)[:, -1, :pw]
                new_pool = u.reshape(bp, tp, cp)[:, tp - nbuf:]
                new_conv = ztail.reshape(bp, -1, halo, fp)[:, -1, halo - (CONV_W - 1):, :f]
            else:
                new_shift = proj[(ts - 1) * bs:, :pw]
                u_b = u.reshape(ts, bs, cp).transpose(1, 0, 2)
                new_pool = jnp.concatenate([state_pool[l], u_b], axis=1)[:, -nbuf:]
                new_conv = ztail.reshape(CONV_W - 1, bs, fp)[:, :, :f].transpose(1, 0, 2)
            return x3, new_shift, new_wkv, new_pool, new_conv

        xp, sh, wk, po, co = layer(xp, prompt=True)
        outs["p_sh"].append(sh)
        outs["p_wkv"].append(wk)
        outs["p_pool"].append(po)
        outs["p_conv"].append(co)
        xs, sh, wk, po, co = layer(xs, prompt=False)
        outs["s_sh"].append(sh)
        outs["s_wkv"].append(wk)
        outs["s_pool"].append(po)
        outs["s_conv"].append(co)

    y_prompt = _rmsnorm(xp, g_final, F32).reshape(bp, tp, d)
    y_sample = _rmsnorm(xs, g_final, F32).reshape(ts, bs, d).transpose(1, 0, 2)
    stk = lambda n: jnp.stack(outs[n])
    return (y_prompt, y_sample, stk("pm_k"), stk("pm_v"), stk("p_sh"), stk("p_wkv"), stk("p_pool"),
            stk("p_conv"), stk("s_sh"), stk("s_wkv"), stk("s_pool"), stk("s_conv"))
```

```python
import functools

import jax
import jax.numpy as jnp
from jax import lax
from jax.experimental import pallas as pl
from jax.experimental.pallas import tpu as pltpu

F32 = jnp.float32
BF16 = jnp.bfloat16

HEAD_SIZE = 64
POOL_WINDOWS = (2, 4, 8, 16)
POOL_HIST = 16
MEM_HEADS = 4
CONV_W = 3
PAST_LEN = 16384
NORM_EPS = 1e-6
GN_EPS = 64e-5
LANE = 128
SUBLANE = 8
VMEM_LIMIT = 56 * 1024 * 1024


def _pick(dim, pref, mult=SUBLANE):
    if dim <= pref:
        return dim
    t = (pref // mult) * mult
    while t >= mult:
        if dim % t == 0:
            return t
        t -= mult
    return dim


def _round_up(x, m):
    return (x + m - 1) // m * m


def _params(*sem):
    return pltpu.CompilerParams(dimension_semantics=sem, vmem_limit_bytes=VMEM_LIMIT)


def _sigmoid(x):
    return 1.0 / (1.0 + jnp.exp(-x))


def _rmsnorm_kernel(x_ref, g_ref, o_ref):
    x = x_ref[...].astype(F32)
    ms = jnp.mean(x * x, axis=-1, keepdims=True)
    o_ref[...] = (x * lax.rsqrt(ms + NORM_EPS) * g_ref[...]).astype(o_ref.dtype)


def _rmsnorm(x, g, out_dtype):
    m, d = x.shape
    tr = _pick(m, 256)
    return pl.pallas_call(
        _rmsnorm_kernel,
        grid=(m // tr,),
        in_specs=[pl.BlockSpec((tr, d), lambda i: (i, 0)),
                  pl.BlockSpec((1, d), lambda i: (0, 0))],
        out_specs=pl.BlockSpec((tr, d), lambda i: (i, 0)),
        out_shape=jax.ShapeDtypeStruct((m, d), out_dtype),
        compiler_params=_params("arbitrary"),
        name="rmsnorm",
    )(x, g.reshape(1, d).astype(F32))


def _mm_kernel(*refs, nk, has_res):
    if has_res:
        x_ref, w_ref, res_ref, o_ref = refs[:4]
        scratch = refs[4:]
    else:
        x_ref, w_ref, o_ref = refs[:3]
        res_ref = None
        scratch = refs[3:]
    part = jnp.dot(x_ref[...].astype(BF16), w_ref[...], preferred_element_type=F32)
    if nk == 1:
        if has_res:
            part = part + res_ref[...]
        o_ref[...] = part.astype(o_ref.dtype)
        return
    acc_ref = scratch[0]
    k = pl.program_id(2)

    @pl.when(k == 0)
    def _():
        acc_ref[...] = part

    @pl.when(k > 0)
    def _():
        acc_ref[...] += part

    @pl.when(k == nk - 1)
    def _():
        r = acc_ref[...]
        if has_res:
            r = r + res_ref[...]
        o_ref[...] = r.astype(o_ref.dtype)


def _matmul(x, w, *, out_dtype, res=None, tm=1024, tn=512, tk=None, name="matmul"):
    m, kd = x.shape
    _, n = w.shape
    tm = _pick(m, tm)
    tn = _pick(n, tn, LANE)
    tk = kd if tk is None else _pick(kd, tk, LANE)
    nk = kd // tk
    in_specs = [pl.BlockSpec((tm, tk), lambda i, j, k: (i, k)),
                pl.BlockSpec((tk, tn), lambda i, j, k: (k, j))]
    args = [x, w]
    if res is not None:
        in_specs.append(pl.BlockSpec((tm, tn), lambda i, j, k: (i, j)))
        args.append(res)
    scratch = [pltpu.VMEM((tm, tn), F32)] if nk > 1 else []
    return pl.pallas_call(
        functools.partial(_mm_kernel, nk=nk, has_res=res is not None),
        grid=(m // tm, n // tn, nk),
        in_specs=in_specs,
        out_specs=pl.BlockSpec((tm, tn), lambda i, j, k: (i, j)),
        out_shape=jax.ShapeDtypeStruct((m, n), out_dtype),
        scratch_shapes=scratch,
        compiler_params=_params("arbitrary", "arbitrary", "arbitrary"),
        name=name,
    )(*args)


def _mm2_kernel(x1_ref, x2_ref, w1_ref, w2_ref, res_ref, o_ref):
    acc = jnp.dot(x1_ref[...], w1_ref[...], preferred_element_type=F32)
    acc = acc + jnp.dot(x2_ref[...], w2_ref[...], preferred_element_type=F32)
    o_ref[...] = (acc + res_ref[...]).astype(o_ref.dtype)


def _matmul2_res(x1, x2, w1, w2, res, *, tm=1024, tn=512):
    m, k1 = x1.shape
    _, k2 = x2.shape
    _, n = w1.shape
    tm = _pick(m, tm)
    tn = _pick(n, tn, LANE)
    return pl.pallas_call(
        _mm2_kernel,
        grid=(m // tm, n // tn),
        in_specs=[pl.BlockSpec((tm, k1), lambda i, j: (i, 0)),
                  pl.BlockSpec((tm, k2), lambda i, j: (i, 0)),
                  pl.BlockSpec((k1, tn), lambda i, j: (0, j)),
                  pl.BlockSpec((k2, tn), lambda i, j: (0, j)),
                  pl.BlockSpec((tm, tn), lambda i, j: (i, j))],
        out_specs=pl.BlockSpec((tm, tn), lambda i, j: (i, j)),
        out_shape=jax.ShapeDtypeStruct((m, n), F32),
        compiler_params=_params("arbitrary", "arbitrary"),
        name="out_proj",
    )(x1, x2, w1, w2, res)


def _prep_kernel(p_ref, st_ref, mu_ref, w0_ref, a0_ref, bw_ref, ba_ref, bg_ref,
                 r_ref, k_ref, v_ref, d_ref, a_ref, g_ref, ext_ref, *, halo, tt, shift, c):
    @pl.when(pl.program_id(1) == 0)
    def _():
        ext_ref[0:halo, :] = st_ref[...]

    p = p_ref[...]
    ext_ref[halo:halo + tt, :] = p
    prev = ext_ref[halo - shift:halo - shift + tt, :]
    xs = p + (prev - p) * mu_ref[...]
    ext_ref[0:halo, :] = ext_ref[tt:tt + halo, :]

    r_ref[...] = xs[:, 0:c].T
    k_ref[...] = xs[:, c:2 * c].T
    v_ref[...] = xs[:, 2 * c:3 * c].T
    lo = xs[:, 3 * c:]
    wpre = jnp.dot(jnp.tanh(lo).astype(BF16), bw_ref[...], preferred_element_type=F32)
    apre = jnp.dot(lo.astype(BF16), ba_ref[...], preferred_element_type=F32)
    g_ref[...] = jnp.dot(_sigmoid(lo).astype(BF16), bg_ref[...], preferred_element_type=F32)
    z = -(w0_ref[...] + wpre)
    softplus = jnp.maximum(z, 0.0) + jnp.log(1.0 + jnp.exp(-jnp.abs(z)))
    d_ref[...] = jnp.exp(-jnp.exp(-softplus - 0.5)).T
    a_ref[...] = _sigmoid(a0_ref[...] + apre).T


def _rwkv_prep(proj, state, mu, w0, a0, bw, ba, bg, *, nseq, shift, halo, tt, c):
    m, ca = proj.shape
    rows = m // nseq
    assert rows % tt == 0
    nt = rows // tt
    row_spec = lambda width: pl.BlockSpec((tt, width), lambda s, j: (s * nt + j, 0))
    tr_spec = pl.BlockSpec((None, c, tt), lambda s, j: (s * nt + j, 0, 0))
    full = lambda arr: pl.BlockSpec(arr.shape, lambda s, j: (0,) * arr.ndim)
    out_t = jax.ShapeDtypeStruct((nseq * nt, c, tt), F32)
    return pl.pallas_call(
        functools.partial(_prep_kernel, halo=halo, tt=tt, shift=shift, c=c),
        grid=(nseq, nt),
        in_specs=[row_spec(ca),
                  pl.BlockSpec((None, halo, ca), lambda s, j: (s, 0, 0)),
                  full(mu), full(w0), full(a0), full(bw), full(ba), full(bg)],
        out_specs=[tr_spec] * 5 + [row_spec(c)],
        out_shape=[out_t] * 5 + [jax.ShapeDtypeStruct((m, c), F32)],
        scratch_shapes=[pltpu.VMEM((halo + tt, ca), F32)],
        compiler_params=_params("arbitrary", "arbitrary"),
        name="rwkv_prep",
    )(proj, state, mu, w0, a0, bw, ba, bg)


V_GROUP = 4
K_BLOCK = 32


def _wkv_kernel(r_ref, k_ref, v_ref, d_ref, a_ref, s0_ref, kk_ref, ka_ref, rk_ref, gw_ref, gb_ref,
                o_ref, sf_ref, s_ref, y_ref, row_ref, *, tt, n):
    @pl.when(pl.program_id(1) == 0)
    def _():
        s_ref[...] = s0_ref[...]

    k_k = kk_ref[...]
    k_a = ka_ref[...]
    r_k = rk_ref[...]
    gn_w = gw_ref[...]
    gn_b = gb_ref[...]
    inv_n = 1.0 / n
    lanes = s_ref.shape[-1]

    def token(t, carry):
        r = r_ref[t]
        k = k_ref[t]
        v = v_ref[t]
        w = d_ref[t]
        a = a_ref[t]
        kk = k * k_k
        norm = jnp.sqrt(jnp.sum(kk * kk, axis=0, keepdims=True))
        kk = kk / jnp.maximum(norm, 1e-12)
        k2 = k * (1.0 + (a - 1.0) * k_a)
        av = -kk
        bv = kk * a
        wr = w * r
        b_r = jnp.sum(bv * r, axis=0, keepdims=True)
        k_r = jnp.sum(k2 * r, axis=0, keepdims=True)
        bonus = jnp.sum(r * k2 * r_k, axis=0, keepdims=True)

        row_ref[0] = av
        row_ref[1] = wr
        row_ref[2] = w
        row_ref[3] = bv
        row_ref[4] = k2

        def vgroup(gidx, c2):
            vb0 = gidx * V_GROUP
            zero = jnp.zeros((SUBLANE, lanes), F32)

            def reduce_keys(kb, acc):
                sa_acc, q_acc = list(acc[0]), list(acc[1])
                for j in range(K_BLOCK):
                    kx = kb * K_BLOCK + j
                    a_row = row_ref[0, pl.ds(kx, 1), :]
                    q_row = row_ref[1, pl.ds(kx, 1), :]
                    for i in range(V_GROUP):
                        s_k = s_ref[vb0 + i, kx]
                        sa_acc[i] = sa_acc[i] + s_k * a_row
                        q_acc[i] = q_acc[i] + s_k * q_row
                return tuple(sa_acc), tuple(q_acc)

            sa8, q8 = lax.fori_loop(0, n // K_BLOCK, reduce_keys, ((zero,) * V_GROUP, (zero,) * V_GROUP))
            base = pl.multiple_of(vb0 * SUBLANE, SUBLANE)
            v8 = [v_ref[t, pl.ds(base + i * SUBLANE, SUBLANE), :] for i in range(V_GROUP)]

            def update_keys(kb, c3):
                for j in range(K_BLOCK):
                    kx = kb * K_BLOCK + j
                    w_row = row_ref[2, pl.ds(kx, 1), :]
                    b_row = row_ref[3, pl.ds(kx, 1), :]
                    k_row = row_ref[4, pl.ds(kx, 1), :]
                    for i in range(V_GROUP):
                        s_ref[vb0 + i, kx] = s_ref[vb0 + i, kx] * w_row + sa8[i] * b_row + v8[i] * k_row
                return c3

            lax.fori_loop(0, n // K_BLOCK, update_keys, 0)
            for i in range(V_GROUP):
                y_ref[pl.ds(base + i * SUBLANE, SUBLANE), :] = q8[i] + sa8[i] * b_r + v8[i] * k_r
            return c2

        lax.fori_loop(0, n // (SUBLANE * V_GROUP), vgroup, 0)
        y = y_ref[...]
        mean = jnp.sum(y, axis=0, keepdims=True) * inv_n
        yc = y - mean
        var = jnp.sum(yc * yc, axis=0, keepdims=True) * inv_n
        o_ref[t] = yc * lax.rsqrt(var + GN_EPS) * gn_w + gn_b + bonus * v
        return carry

    lax.fori_loop(0, tt, token, 0)

    @pl.when(pl.program_id(1) == pl.num_programs(1) - 1)
    def _():
        sf_ref[...] = s_ref[...]


def _wkv(r, k, v, d, a, s0, k_k, k_a, r_k, gn_w, gn_b, *, tt, group_major):
    if group_major:
        g, t, n, lanes = r.shape
    else:
        t, g, n, lanes = r.shape
    tt = _pick(t, tt, 1)
    nb = n // SUBLANE
    if group_major:
        tok = pl.BlockSpec((None, tt, n, lanes), lambda i, j: (i, j, 0, 0))
    else:
        tok = pl.BlockSpec((tt, None, n, lanes), lambda i, j: (j, i, 0, 0))
    st = pl.BlockSpec((None, nb, n, SUBLANE, lanes), lambda i, j: (i, 0, 0, 0, 0))
    gp = k_k.shape[0]
    par = pl.BlockSpec((None, n, lanes), (lambda i, j: (i, 0, 0)) if gp > 1 else (lambda i, j: (0, 0, 0)))
    return pl.pallas_call(
        functools.partial(_wkv_kernel, tt=tt, n=n),
        grid=(g, t // tt),
        in_specs=[tok] * 5 + [st] + [par] * 5,
        out_specs=[tok, st],
        out_shape=[jax.ShapeDtypeStruct(r.shape, F32),
                   jax.ShapeDtypeStruct((g, nb, n, SUBLANE, lanes), F32)],
        scratch_shapes=[pltpu.VMEM((nb, n, SUBLANE, lanes), F32), pltpu.VMEM((n, lanes), F32),
                        pltpu.VMEM((5, n, lanes), F32)],
        compiler_params=_params("arbitrary", "arbitrary"),
        name="wkv_recurrence",
    )(r, k, v, d, a, s0, k_k, k_a, r_k, gn_w, gn_b)


def _to_chain_kernel(x_ref, o_ref, *, nb, nh, n, tt):
    for k in range(n):
        rows = jnp.concatenate([x_ref[b, pl.ds(k, nh, stride=n), :] for b in range(nb)], axis=0)
        o_ref[pl.ds(k, tt, stride=n), :] = rows.T


def _to_chain(xt, *, nb, nh):
    tiles, c, tt = xt.shape
    nt = tiles // nb
    n = c // nh
    return pl.pallas_call(
        functools.partial(_to_chain_kernel, nb=nb, nh=nh, n=n, tt=tt),
        grid=(nt,),
        in_specs=[pl.BlockSpec((nb, None, c, tt), lambda j: (0, j, 0, 0))],
        out_specs=pl.BlockSpec((tt * n, nb * nh), lambda j: (j, 0)),
        out_shape=jax.ShapeDtypeStruct((nt * tt * n, nb * nh), F32),
        compiler_params=_params("arbitrary"),
        name="to_chain",
    )(xt.reshape(nb, nt, c, tt))


def _from_chain_gate_kernel(y_ref, g_ref, o_ref, xt_ref, *, nb, nh, n, tt):
    for v in range(n):
        yv = y_ref[pl.ds(v, tt, stride=n), :].T
        for b in range(nb):
            xt_ref[b, pl.ds(v, nh, stride=n), :] = yv[b * nh:(b + 1) * nh, :]
    for b in range(nb):
        o_ref[b] = (xt_ref[b].T * g_ref[b]).astype(o_ref.dtype)


def _from_chain_gate(y, g, *, nb, nh, tt):
    m, c = g.shape
    t = m // nb
    n = c // nh
    nt = t // tt
    row = pl.BlockSpec((nb, tt, c), lambda j: (0, j, 0))
    out = pl.pallas_call(
        functools.partial(_from_chain_gate_kernel, nb=nb, nh=nh, n=n, tt=tt),
        grid=(nt,),
        in_specs=[pl.BlockSpec((tt * n, nb * nh), lambda j: (j, 0)), row],
        out_specs=row,
        out_shape=jax.ShapeDtypeStruct((nb, t, c), BF16),
        scratch_shapes=[pltpu.VMEM((nb, c, tt), F32)],
        compiler_params=_params("arbitrary"),
        name="from_chain_gate",
    )(y, g.reshape(nb, t, c))
    return out.reshape(m, c)


def _transpose_gate_kernel(y_ref, g_ref, o_ref):
    o_ref[...] = (y_ref[...].T * g_ref[...]).astype(o_ref.dtype)


def _transpose_gate(yt, g):
    tiles, c, tt = yt.shape
    return pl.pallas_call(
        _transpose_gate_kernel,
        grid=(tiles,),
        in_specs=[pl.BlockSpec((None, c, tt), lambda i: (i, 0, 0)),
                  pl.BlockSpec((tt, c), lambda i: (i, 0))],
        out_specs=pl.BlockSpec((tt, c), lambda i: (i, 0)),
        out_shape=jax.ShapeDtypeStruct((tiles * tt, c), BF16),
        compiler_params=_params("arbitrary"),
        name="transpose_gate",
    )(yt, g)


def _pool_kernel(u_ref, h_ref, w_ref, sc_ref, o_ref, ext_ref, *, halo, tt, shift, pos0):
    gi = pl.program_id(1)
    j = pl.program_id(2)

    @pl.when(j == 0)
    def _():
        ext_ref[0:halo, :] = h_ref[...]

    x = u_ref[...]
    ext_ref[halo:halo + tt, :] = x
    row = lax.broadcasted_iota(jnp.int32, (tt, 1), 0) + j * tt
    time = row if shift == 1 else row // shift

    for g, win in enumerate(POOL_WINDOWS):
        @pl.when(gi == g)
        def _(win=win):
            acc = x
            for back in range(1, win):
                lo = halo - back * shift
                acc = acc + ext_ref[lo:lo + tt, :]
            cnt = jnp.minimum(pos0 + time + 1, win).astype(F32)
            d = acc / cnt - x
            y = jnp.dot(d.astype(BF16), w_ref[...], preferred_element_type=F32)
            o_ref[...] = (y * sc_ref[...]).astype(o_ref.dtype)

    if tt >= halo:
        ext_ref[0:halo, :] = ext_ref[tt:tt + halo, :]


def _pool(u, hist, w_pool, scale, *, nseq, shift, tt, pos0):
    m, cp = u.shape
    ng = len(POOL_WINDOWS)
    gw = cp // ng
    rows = m // nseq
    tt = _pick(rows, tt)
    nt = rows // tt
    halo = POOL_HIST * shift
    assert nt == 1 or tt >= halo
    return pl.pallas_call(
        functools.partial(_pool_kernel, halo=halo, tt=tt, shift=shift, pos0=pos0),
        grid=(nseq, ng, nt),
        in_specs=[pl.BlockSpec((tt, gw), lambda s, g, j: (s * nt + j, g)),
                  pl.BlockSpec((None, halo, gw), lambda s, g, j: (s, 0, g)),
                  pl.BlockSpec((None, gw, gw), lambda s, g, j: (g, 0, 0)),
                  pl.BlockSpec((1, gw), lambda s, g, j: (0, g))],
        out_specs=pl.BlockSpec((tt, gw), lambda s, g, j: (s * nt + j, g)),
        out_shape=jax.ShapeDtypeStruct((m, cp), BF16),
        scratch_shapes=[pltpu.VMEM((halo + tt, gw), F32)],
        compiler_params=_params("arbitrary", "arbitrary", "arbitrary"),
        name="pool_mix",
    )(u, hist, w_pool, scale)


def _attend(q, k, v, scale):
    s = lax.dot_general(q, k, (((1,), (1,)), ((), ())), preferred_element_type=F32) * scale
    s = s - jnp.max(s, axis=-1, keepdims=True)
    e = jnp.exp(s)
    pr = e / jnp.sum(e, axis=-1, keepdims=True)
    return jnp.dot(pr.astype(BF16), v, preferred_element_type=F32)


def _attn_prompt_kernel(q_ref, k_ref, v_ref, o_ref, *, scale):
    o = _attend(q_ref[...], k_ref[...].astype(BF16), v_ref[...].astype(BF16), scale)
    o_ref[...] = o.astype(o_ref.dtype)


def _attn_prompt(q, mk, mv, *, batch, heads):
    m, d = q.shape
    t = m // batch
    nm = mk.shape[1]
    hd = d // heads
    tq = _pick(t, 512)
    nq = t // tq
    return pl.pallas_call(
        functools.partial(_attn_prompt_kernel, scale=hd ** -0.5),
        grid=(batch, nq, heads),
        in_specs=[pl.BlockSpec((tq, hd), lambda b, i, h: (b * nq + i, h)),
                  pl.BlockSpec((None, nm, hd), lambda b, i, h: (b, 0, h)),
                  pl.BlockSpec((None, nm, hd), lambda b, i, h: (b, 0, h))],
        out_specs=pl.BlockSpec((tq, hd), lambda b, i, h: (b * nq + i, h)),
        out_shape=jax.ShapeDtypeStruct((m, d), BF16),
        compiler_params=_params("arbitrary", "arbitrary", "arbitrary"),
        name="attn_prompt",
    )(q, mk, mv)


def _attn_sample_kernel(q_ref, k_ref, v_ref, o_ref, *, scale, heads, hd):
    for h in range(heads):
        cols = slice(h * hd, (h + 1) * hd)
        s = lax.dot_general(q_ref[:, cols], k_ref[:, h, :], (((1,), (1,)), ((), ())),
                            preferred_element_type=F32) * scale
        s = s - jnp.max(s, axis=-1, keepdims=True)
        e = jnp.exp(s)
        pr = e / jnp.sum(e, axis=-1, keepdims=True)
        o_ref[:, cols] = jnp.dot(pr, v_ref[:, h, :], preferred_element_type=F32)


def _attn_sample(q, mk, mv, layer):
    b, t, d = q.shape
    _, _, nm, heads, hd = mk.shape
    kv_spec = pl.BlockSpec((None, None, nm, heads, hd), lambda i: (layer, i, 0, 0, 0))
    return pl.pallas_call(
        functools.partial(_attn_sample_kernel, scale=hd ** -0.5, heads=heads, hd=hd),
        grid=(b,),
        in_specs=[pl.BlockSpec((None, t, d), lambda i: (i, 0, 0)), kv_spec, kv_spec],
        out_specs=pl.BlockSpec((None, t, d), lambda i: (i, 0, 0)),
        out_shape=jax.ShapeDtypeStruct((b, t, d), F32),
        compiler_params=_params("arbitrary"),
        name="attn_sample",
    )(q, mk, mv)


def _ffn_up_kernel(h_ref, hp_ref, wg_ref, wv_ref, cw_ref, cb_ref, st_ref, act_ref, zt_ref, zext_ref,
                   *, halo, tm, shift, recompute_halo):
    h = h_ref[...]
    z = jnp.dot(h, wg_ref[...], preferred_element_type=F32)
    val = jnp.dot(h, wv_ref[...], preferred_element_type=F32)
    zprev = st_ref[...]
    if recompute_halo:
        zhalo = jnp.dot(hp_ref[...], wg_ref[...], preferred_element_type=F32)
        zprev = jnp.where(pl.program_id(1) == 0, zprev, zhalo)
    zext_ref[0:halo, :] = zprev
    zext_ref[halo:halo + tm, :] = z
    zm1 = zext_ref[halo - shift:halo - shift + tm, :]
    zm2 = zext_ref[halo - 2 * shift:halo - 2 * shift + tm, :]
    zc = cb_ref[...] + cw_ref[0:1, :] * zm2 + cw_ref[1:2, :] * zm1 + cw_ref[2:3, :] * z
    act_ref[...] = (zc * _sigmoid(zc) * val).astype(act_ref.dtype)
    zt_ref[...] = zext_ref[tm:tm + halo, :]


def _ffn_up(h, wg, wv, cw, cb, state, *, nseq, shift, halo, tm, tn):
    m, d = h.shape
    _, fp = wg.shape
    rows = m // nseq
    tm = _pick(rows, tm)
    nt = rows // tm
    tn = min(tn, fp)
    assert tm >= halo and (nt == 1 or halo == SUBLANE)
    hb = tm // SUBLANE
    return pl.pallas_call(
        functools.partial(_ffn_up_kernel, halo=halo, tm=tm, shift=shift, recompute_halo=nt > 1),
        grid=(nseq, nt, pl.cdiv(fp, tn)),
        in_specs=[pl.BlockSpec((tm, d), lambda s, i, j: (s * nt + i, 0)),
                  pl.BlockSpec((SUBLANE, d), lambda s, i, j: (jnp.maximum((s * nt + i) * hb - 1, 0), 0)),
                  pl.BlockSpec((d, tn), lambda s, i, j: (0, j)),
                  pl.BlockSpec((d, tn), lambda s, i, j: (0, j)),
                  pl.BlockSpec((CONV_W, tn), lambda s, i, j: (0, j)),
                  pl.BlockSpec((1, tn), lambda s, i, j: (0, j)),
                  pl.BlockSpec((None, halo, tn), lambda s, i, j: (s, 0, j))],
        out_specs=[pl.BlockSpec((tm, tn), lambda s, i, j: (s * nt + i, j)),
                   pl.BlockSpec((halo, tn), lambda s, i, j: (s * nt + i, j))],
        out_shape=[jax.ShapeDtypeStruct((m, fp), BF16),
                   jax.ShapeDtypeStruct((nseq * nt * halo, fp), F32)],
        scratch_shapes=[pltpu.VMEM((halo + tm, tn), F32)],
        compiler_params=_params("arbitrary", "arbitrary", "arbitrary"),
        name="ffn_up",
    )(h, h, wg, wv, cw, cb, state)


def kernel(x_prompt, x_sample, mem_prompt, cache_mem_k, cache_mem_v, state_shift, state_wkv, state_pool,
           state_conv, g_mix, w_in, mu_shift, w0, b_w, a0, b_a, b_g, k_k, k_a, r_k, gn_w, gn_b, w_pool,
           pool_scale, w_out, g_mem, w_mk, w_mv, g_attn, w_mq, w_mo, g_ffn, w_gate, w_val, conv_w, conv_b,
           w_down, g_final):
    bp, tp, d = x_prompt.shape
    bs, ts, _ = x_sample.shape
    depth = w_in.shape[0]
    c = w0.shape[1]
    nh = c // HEAD_SIZE
    nvb = HEAD_SIZE // SUBLANE
    cp = pool_scale.shape[1]
    pw = mu_shift.shape[1]
    lora = pw - 3 * c
    lp = _round_up(lora, LANE)
    ca = 3 * c + lp
    dl, al, gl = b_w.shape[1], b_a.shape[1], b_g.shape[1]
    f = w_gate.shape[2]
    fp = f
    nmem = mem_prompt.shape[1]
    nbuf = state_pool.shape[2]
    mp, ms = bp * tp, bs * ts

    xp = x_prompt.reshape(mp, d)
    xs = x_sample.transpose(1, 0, 2).reshape(ms, d)

    outs = {n: [] for n in ("pm_k", "pm_v", "p_sh", "p_wkv", "p_pool", "p_conv",
                            "s_sh", "s_wkv", "s_pool", "s_conv")}

    for l in range(depth):
        wa = jnp.pad(w_in[l][:, :pw], ((0, 0), (0, ca - pw))).astype(BF16)
        wu = w_in[l][:, pw:].astype(BF16)
        mu = jnp.pad(mu_shift[l], (0, ca - pw)).reshape(1, ca)
        bw = jnp.zeros((lp, c), F32).at[0:dl].set(b_w[l]).astype(BF16)
        ba = jnp.zeros((lp, c), F32).at[dl:dl + al].set(b_a[l]).astype(BF16)
        bg = jnp.zeros((lp, c), F32).at[dl + al:dl + al + gl].set(b_g[l]).astype(BF16)
        w0l = w0[l].reshape(1, c)
        a0l = a0[l].reshape(1, c)
        wpool = w_pool[l].astype(BF16)
        pscale = pool_scale[l].reshape(1, cp)
        wo1 = w_out[l][:c].astype(BF16)
        wo2 = w_out[l][c:].astype(BF16)
        wmk = w_mk[l].astype(BF16)
        wmv = w_mv[l].astype(BF16)
        wmq = w_mq[l].astype(BF16)
        wmo = w_mo[l].astype(BF16)
        wg = w_gate[l].astype(BF16)
        wv = w_val[l].astype(BF16)
        wd = w_down[l].astype(BF16)
        cw = conv_w[l]
        cb = conv_b[l].reshape(1, fp)

        def head_tile(p):
            return p.reshape(nh, HEAD_SIZE).T

        chain_p = [jnp.tile(head_tile(p), (1, bp)).reshape(1, HEAD_SIZE, bp * nh)
                   for p in (k_k[l], k_a[l], r_k[l].reshape(-1), gn_w[l], gn_b[l])]
        chain_s = [jnp.broadcast_to(p.reshape(nh, HEAD_SIZE, 1), (nh, HEAD_SIZE, bs))
                   for p in (k_k[l], k_a[l], r_k[l].reshape(-1), gn_w[l], gn_b[l])]

        mn = _rmsnorm(mem_prompt.reshape(bp * nmem, d), g_mem[l], BF16)
        mk = _matmul(mn, wmk, out_dtype=F32, name="mem_k")
        mv = _matmul(mn, wmv, out_dtype=F32, name="mem_v")
        outs["pm_k"].append(mk.reshape(bp, nmem, MEM_HEADS, d // MEM_HEADS))
        outs["pm_v"].append(mv.reshape(bp, nmem, MEM_HEADS, d // MEM_HEADS))

        def layer(x, *, prompt):
            m = x.shape[0]
            if prompt:
                nseq, shift, halo1, halo, pos0 = bp, 1, SUBLANE, SUBLANE, 0
                st_shift = jnp.zeros((bp, halo1, ca), F32)
                hist = jnp.zeros((bp, POOL_HIST, cp), F32)
                st_conv = jnp.zeros((bp, halo, fp), F32)
            else:
                nseq, shift, halo1, halo, pos0 = 1, bs, bs, (CONV_W - 1) * bs, PAST_LEN
                st_shift = jnp.pad(state_shift[l], ((0, 0), (0, ca - pw))).reshape(1, halo1, ca)
                hist = jnp.pad(state_pool[l].transpose(1, 0, 2), ((POOL_HIST - nbuf, 0), (0, 0), (0, 0)))
                hist = hist.reshape(1, POOL_HIST * bs, cp)
                st_conv = state_conv[l].transpose(1, 0, 2).reshape(1, halo, fp)

            xn = _rmsnorm(x, g_mix[l], BF16)
            proj = _matmul(xn, wa, out_dtype=F32, name="in_proj_rwkv")
            u = _matmul(xn, wu, out_dtype=F32, name="in_proj_pool")
            tile = min(LANE, tp) if prompt else bs
            *rkvda, g = _rwkv_prep(proj, st_shift, mu, w0l, a0l, bw, ba, bg,
                                   nseq=nseq, shift=shift, halo=halo1, tt=tile, c=c)
            if prompt:
                chains = [_to_chain(x, nb=bp, nh=nh).reshape(1, tp, HEAD_SIZE, bp * nh) for x in rkvda]
                s0 = jnp.zeros((1, nvb, HEAD_SIZE, SUBLANE, bp * nh), F32)
                y, s_fin = _wkv(*chains, s0, *chain_p, tt=32, group_major=True)
                r_out = _from_chain_gate(y.reshape(tp * HEAD_SIZE, bp * nh), g, nb=bp, nh=nh, tt=tile)
                new_wkv = s_fin.reshape(nvb, HEAD_SIZE, SUBLANE, bp, nh).transpose(3, 4, 0, 2, 1)
                new_wkv = new_wkv.reshape(bp, nh, HEAD_SIZE, HEAD_SIZE)
            else:
                chains = [x.reshape(ts, nh, HEAD_SIZE, bs) for x in rkvda]
                s0 = state_wkv[l].reshape(bs, nh, nvb, SUBLANE, HEAD_SIZE).transpose(1, 2, 4, 3, 0)
                y, s_fin = _wkv(*chains, s0, *chain_s, tt=ts, group_major=False)
                r_out = _transpose_gate(y.reshape(ts, c, bs), g)
                new_wkv = s_fin.transpose(4, 0, 1, 3, 2).reshape(bs, nh, HEAD_SIZE, HEAD_SIZE)
            p_out = _pool(u, hist, wpool, pscale, nseq=nseq, shift=shift,
                          tt=256 if prompt else m, pos0=pos0)
            x1 = _matmul2_res(r_out, p_out, wo1, wo2, x)

            hq = _rmsnorm(x1, g_attn[l], BF16)
            if prompt:
                q = _matmul(hq, wmq, out_dtype=BF16, name="attn_q")
                o = _attn_prompt(q, mk.reshape(bp, nmem, d), mv.reshape(bp, nmem, d),
                                 batch=bp, heads=MEM_HEADS)
            else:
                q = _matmul(hq, wmq, out_dtype=F32, name="attn_q")
                q = q.reshape(ts, bs, d).transpose(1, 0, 2)
                o = _attn_sample(q, cache_mem_k, cache_mem_v, l)
                o = o.transpose(1, 0, 2).reshape(m, d)
            x2 = _matmul(o, wmo, out_dtype=F32, res=x1, name="attn_o")

            hf = _rmsnorm(x2, g_ffn[l], BF16)
            act, ztail = _ffn_up(hf, wg, wv, cw, cb, st_conv, nseq=nseq, shift=shift, halo=halo,
                                 tm=1024, tn=512)
            x3 = _matmul(act, wd, out_dtype=F32, res=x2, tm=512, tn=512, name="ffn_down")

            if prompt:
                new_shift = proj.reshape(bp, tp, ca)[:, -1, :pw]
                new_pool = u.reshape(bp, tp, cp)[:, tp - nbuf:]
                new_conv = ztail.reshape(bp, -1, halo, fp)[:, -1, halo - (CONV_W - 1):, :f]
            else:
                new_shift = proj[(ts - 1) * bs:, :pw]
                u_b = u.reshape(ts, bs, cp).transpose(1, 0, 2)
                new_pool = jnp.concatenate([state_pool[l], u_b], axis=1)[:, -nbuf:]
                new_conv = ztail.reshape(CONV_W - 1, bs, fp)[:, :, :f].transpose(1, 0, 2)
            return x3, new_shift, new_wkv, new_pool, new_conv

        xp, sh, wk, po, co = layer(xp, prompt=True)
        outs["p_sh"].append(sh)
        outs["p_wkv"].append(wk)
        outs["p_pool"].append(po)
        outs["p_conv"].append(co)
        xs, sh, wk, po, co = layer(xs, prompt=False)
        outs["s_sh"].append(sh)
        outs["s_wkv"].append(wk)
        outs["s_pool"].append(po)
        outs["s_conv"].append(co)

    y_prompt = _rmsnorm(xp, g_final, F32).reshape(bp, tp, d)
    y_sample = _rmsnorm(xs, g_final, F32).reshape(ts, bs, d).transpose(1, 0, 2)
    stk = lambda n: jnp.stack(outs[n])
    return (y_prompt, y_sample, stk("pm_k"), stk("pm_v"), stk("p_sh"), stk("p_wkv"), stk("p_pool"),
            stk("p_conv"), stk("s_sh"), stk("s_wkv"), stk("s_pool"), stk("s_conv"))
```

```python
import functools

import jax
import jax.numpy as jnp
from jax import lax
from jax.experimental import pallas as pl
from jax.experimental.pallas import tpu as pltpu

F32 = jnp.float32
BF16 = jnp.bfloat16

HEAD_SIZE = 64
POOL_WINDOWS = (2, 4, 8, 16)
POOL_HIST = 16
MEM_HEADS = 4
CONV_W = 3
PAST_LEN = 16384
NORM_EPS = 1e-6
GN_EPS = 64e-5
EXP_M_HALF = 0.6065306597126334
LANE = 128
SUBLANE = 8
VMEM_LIMIT = 56 * 1024 * 1024


def _pick(dim, pref, mult=SUBLANE):
    if dim <= pref:
        return dim
    t = (pref // mult) * mult
    while t >= mult:
        if dim % t == 0:
            return t
        t -= mult
    return dim


def _round_up(x, m):
    return (x + m - 1) // m * m


def _params(*sem):
    return pltpu.CompilerParams(dimension_semantics=sem, vmem_limit_bytes=VMEM_LIMIT)


def _sigmoid(x):
    return 1.0 / (1.0 + jnp.exp(-x))


def _rmsnorm_kernel(x_ref, g_ref, o_ref):
    x = x_ref[...].astype(F32)
    ms = jnp.mean(x * x, axis=-1, keepdims=True)
    o_ref[...] = (x * lax.rsqrt(ms + NORM_EPS) * g_ref[...]).astype(o_ref.dtype)


def _rmsnorm(x, g, out_dtype):
    m, d = x.shape
    tr = _pick(m, 256)
    return pl.pallas_call(
        _rmsnorm_kernel,
        grid=(m // tr,),
        in_specs=[pl.BlockSpec((tr, d), lambda i: (i, 0)),
                  pl.BlockSpec((1, d), lambda i: (0, 0))],
        out_specs=pl.BlockSpec((tr, d), lambda i: (i, 0)),
        out_shape=jax.ShapeDtypeStruct((m, d), out_dtype),
        compiler_params=_params("arbitrary"),
        name="rmsnorm",
    )(x, g.reshape(1, d).astype(F32))


def _mm_kernel(*refs, nk, has_res):
    if has_res:
        x_ref, w_ref, res_ref, o_ref = refs[:4]
        scratch = refs[4:]
    else:
        x_ref, w_ref, o_ref = refs[:3]
        res_ref = None
        scratch = refs[3:]
    part = jnp.dot(x_ref[...].astype(BF16), w_ref[...], preferred_element_type=F32)
    if nk == 1:
        if has_res:
            part = part + res_ref[...]
        o_ref[...] = part.astype(o_ref.dtype)
        return
    acc_ref = scratch[0]
    k = pl.program_id(2)

    @pl.when(k == 0)
    def _():
        acc_ref[...] = part

    @pl.when(k > 0)
    def _():
        acc_ref[...] += part

    @pl.when(k == nk - 1)
    def _():
        r = acc_ref[...]
        if has_res:
            r = r + res_ref[...]
        o_ref[...] = r.astype(o_ref.dtype)


def _matmul(x, w, *, out_dtype, res=None, tm=1024, tn=512, tk=None, name="matmul"):
    m, kd = x.shape
    _, n = w.shape
    tm = _pick(m, tm)
    tn = _pick(n, tn, LANE)
    tk = kd if tk is None else _pick(kd, tk, LANE)
    nk = kd // tk
    in_specs = [pl.BlockSpec((tm, tk), lambda i, j, k: (i, k)),
                pl.BlockSpec((tk, tn), lambda i, j, k: (k, j))]
    args = [x, w]
    if res is not None:
        in_specs.append(pl.BlockSpec((tm, tn), lambda i, j, k: (i, j)))
        args.append(res)
    scratch = [pltpu.VMEM((tm, tn), F32)] if nk > 1 else []
    return pl.pallas_call(
        functools.partial(_mm_kernel, nk=nk, has_res=res is not None),
        grid=(m // tm, n // tn, nk),
        in_specs=in_specs,
        out_specs=pl.BlockSpec((tm, tn), lambda i, j, k: (i, j)),
        out_shape=jax.ShapeDtypeStruct((m, n), out_dtype),
        scratch_shapes=scratch,
        compiler_params=_params("arbitrary", "arbitrary", "arbitrary"),
        name=name,
    )(*args)


def _mm2_kernel(x1_ref, x2_ref, w1_ref, w2_ref, res_ref, o_ref):
    acc = jnp.dot(x1_ref[...], w1_ref[...], preferred_element_type=F32)
    acc = acc + jnp.dot(x2_ref[...], w2_ref[...], preferred_element_type=F32)
    o_ref[...] = (acc + res_ref[...]).astype(o_ref.dtype)


def _matmul2_res(x1, x2, w, res, *, tm=1024, tn=512):
    m, k1 = x1.shape
    _, k2 = x2.shape
    kw, n = w.shape
    assert k1 == k2 and kw == k1 + k2
    tm = _pick(m, tm)
    tn = _pick(n, tn, LANE)
    return pl.pallas_call(
        _mm2_kernel,
        grid=(m // tm, n // tn),
        in_specs=[pl.BlockSpec((tm, k1), lambda i, j: (i, 0)),
                  pl.BlockSpec((tm, k2), lambda i, j: (i, 0)),
                  pl.BlockSpec((k1, tn), lambda i, j: (0, j)),
                  pl.BlockSpec((k2, tn), lambda i, j: (1, j)),
                  pl.BlockSpec((tm, tn), lambda i, j: (i, j))],
        out_specs=pl.BlockSpec((tm, tn), lambda i, j: (i, j)),
        out_shape=jax.ShapeDtypeStruct((m, n), F32),
        compiler_params=_params("arbitrary", "arbitrary"),
        name="out_proj",
    )(x1, x2, w, w, res)


def _prep_kernel(p_ref, st_ref, mu_ref, w0_ref, a0_ref, bw_ref, ba_ref, bg_ref,
                 r_ref, k_ref, v_ref, d_ref, a_ref, g_ref, ext_ref, *, halo, tt, shift, c):
    @pl.when(pl.program_id(1) == 0)
    def _():
        ext_ref[0:halo, :] = st_ref[...]

    p = p_ref[...]
    ext_ref[halo:halo + tt, :] = p
    prev = ext_ref[halo - shift:halo - shift + tt, :]
    xs = p + (prev - p) * mu_ref[...]
    ext_ref[0:halo, :] = ext_ref[tt:tt + halo, :]

    r_ref[...] = xs[:, 0:c].T
    k_ref[...] = xs[:, c:2 * c].T
    v_ref[...] = xs[:, 2 * c:3 * c].T
    lo = xs[:, 3 * c:]
    wpre = jnp.dot(jnp.tanh(lo).astype(BF16), bw_ref[...], preferred_element_type=F32)
    apre = jnp.dot(lo.astype(BF16), ba_ref[...], preferred_element_type=F32)
    g_ref[...] = jnp.dot(_sigmoid(lo).astype(BF16), bg_ref[...], preferred_element_type=F32)
    ez = jnp.exp(-(w0_ref[...] + wpre))
    d_ref[...] = jnp.exp(-EXP_M_HALF / (1.0 + ez)).T
    a_ref[...] = _sigmoid(a0_ref[...] + apre).T


def _rwkv_prep(proj, state, mu, w0, a0, bw, ba, bg, *, nseq, shift, halo, tt, c):
    m, ca = proj.shape
    rows = m // nseq
    assert rows % tt == 0
    nt = rows // tt
    row_spec = lambda width: pl.BlockSpec((tt, width), lambda s, j: (s * nt + j, 0))
    tr_spec = pl.BlockSpec((None, c, tt), lambda s, j: (s * nt + j, 0, 0))
    full = lambda arr: pl.BlockSpec(arr.shape, lambda s, j: (0,) * arr.ndim)
    out_t = jax.ShapeDtypeStruct((nseq * nt, c, tt), F32)
    return pl.pallas_call(
        functools.partial(_prep_kernel, halo=halo, tt=tt, shift=shift, c=c),
        grid=(nseq, nt),
        in_specs=[row_spec(ca),
                  pl.BlockSpec((None, halo, ca), lambda s, j: (s, 0, 0)),
                  full(mu), full(w0), full(a0), full(bw), full(ba), full(bg)],
        out_specs=[tr_spec] * 5 + [row_spec(c)],
        out_shape=[out_t] * 5 + [jax.ShapeDtypeStruct((m, c), F32)],
        scratch_shapes=[pltpu.VMEM((halo + tt, ca), F32)],
        compiler_params=_params("arbitrary", "arbitrary"),
        name="rwkv_prep",
    )(proj, state, mu, w0, a0, bw, ba, bg)


V_GROUP = 4
K_BLOCK = 32


def _wkv_kernel(*refs, tt, n, cast_steps):
    (r_ref, k_ref, v_ref, d_ref, a_ref, s0_ref, kk_ref, ka_ref, rk_ref, gw_ref, gb_ref) = refs[:11]
    nc = len(cast_steps)
    w_refs = refs[11:11 + nc]
    o_ref, sf_ref = refs[11 + nc:13 + nc]
    wb_refs = refs[13 + nc:13 + 2 * nc]
    s_ref, y_ref, row_ref, dot_ref, plast_ref = refs[13 + 2 * nc:]

    @pl.when(pl.program_id(1) == 0)
    def _():
        s_ref[...] = s0_ref[...]

    for w_ref, wb_ref, steps in zip(w_refs, wb_refs, cast_steps):
        @pl.when(pl.program_id(1) < steps)
        def _(w_ref=w_ref, wb_ref=wb_ref):
            wb_ref[...] = w_ref[...].astype(wb_ref.dtype)

    k_k = kk_ref[...]
    k_a = ka_ref[...]
    r_k = rk_ref[...]
    gn_w = gw_ref[...]
    gn_b = gb_ref[...]
    inv_n = 1.0 / n
    lanes = s_ref.shape[-1]

    r = r_ref[...]
    k = k_ref[...]
    w = d_ref[...]
    a = a_ref[...]
    kk = k * k_k
    norm = jnp.sqrt(jnp.sum(kk * kk, axis=1, keepdims=True))
    kk = kk / jnp.maximum(norm, 1e-12)
    k2 = k * (1.0 + (a - 1.0) * k_a)
    bv = kk * a
    p = None
    for t in range(tt):
        row_ref[0, t] = -kk[t] if p is None else -kk[t] * p
        p = w[t] if p is None else p * w[t]
        inv_p = 1.0 / p
        row_ref[1, t] = r[t] * p
        row_ref[2, t] = bv[t] * inv_p
        row_ref[3, t] = k2[t] * inv_p
    plast_ref[...] = p
    dot_ref[0] = jnp.sum(bv * r, axis=1, keepdims=True)
    dot_ref[1] = jnp.sum(k2 * r, axis=1, keepdims=True)
    dot_ref[2] = jnp.sum(r * k2 * r_k, axis=1, keepdims=True)

    def token(t, carry):
        b_r = dot_ref[0, t]
        k_r = dot_ref[1, t]

        def vgroup(gidx, c2):
            vb0 = gidx * V_GROUP
            zero = jnp.zeros((SUBLANE, lanes), F32)

            def reduce_keys(kb, acc):
                sa_acc, q_acc = list(acc[0]), list(acc[1])
                for j in range(K_BLOCK):
                    kx = kb * K_BLOCK + j
                    a_row = row_ref[0, t, pl.ds(kx, 1), :]
                    q_row = row_ref[1, t, pl.ds(kx, 1), :]
                    for i in range(V_GROUP):
                        s_k = s_ref[vb0 + i, kx]
                        sa_acc[i] = sa_acc[i] + s_k * a_row
                        q_acc[i] = q_acc[i] + s_k * q_row
                return tuple(sa_acc), tuple(q_acc)

            sa8, q8 = lax.fori_loop(0, n // K_BLOCK, reduce_keys, ((zero,) * V_GROUP, (zero,) * V_GROUP))
            base = pl.multiple_of(vb0 * SUBLANE, SUBLANE)
            v8 = [v_ref[t, pl.ds(base + i * SUBLANE, SUBLANE), :] for i in range(V_GROUP)]

            def update_keys(kb, c3):
                for j in range(K_BLOCK):
                    kx = kb * K_BLOCK + j
                    b_row = row_ref[2, t, pl.ds(kx, 1), :]
                    k_row = row_ref[3, t, pl.ds(kx, 1), :]
                    for i in range(V_GROUP):
                        s_ref[vb0 + i, kx] = s_ref[vb0 + i, kx] + sa8[i] * b_row + v8[i] * k_row
                return c3

            lax.fori_loop(0, n // K_BLOCK, update_keys, 0)
            for i in range(V_GROUP):
                y_ref[t, pl.ds(base + i * SUBLANE, SUBLANE), :] = q8[i] + sa8[i] * b_r + v8[i] * k_r
            return c2

        lax.fori_loop(0, n // (SUBLANE * V_GROUP), vgroup, 0)
        return carry

    lax.fori_loop(0, tt, token, 0)

    def rescale(vb, carry):
        for kx in range(n):
            s_ref[vb, kx] = s_ref[vb, kx] * plast_ref[kx:kx + 1, :]
        return carry

    lax.fori_loop(0, n // SUBLANE, rescale, 0)

    y = y_ref[...]
    mean = jnp.sum(y, axis=1, keepdims=True) * inv_n
    yc = y - mean
    var = jnp.sum(yc * yc, axis=1, keepdims=True) * inv_n
    o_ref[...] = yc * lax.rsqrt(var + GN_EPS) * gn_w + gn_b + dot_ref[2] * v_ref[...]

    @pl.when(pl.program_id(1) == pl.num_programs(1) - 1)
    def _():
        sf_ref[...] = s_ref[...]


def _cast_rows(rows, nsteps):
    blk = _round_up(-(-rows // nsteps), 2 * SUBLANE)
    while rows % blk:
        blk += 2 * SUBLANE
    return blk, rows // blk


def _wkv(r, k, v, d, a, s0, k_k, k_a, r_k, gn_w, gn_b, *, tt, group_major, cast=()):
    if group_major:
        g, t, n, lanes = r.shape
    else:
        t, g, n, lanes = r.shape
    tt = _pick(t, tt, 1)
    nb = n // SUBLANE
    assert not cast or g == 1
    cast_specs, cast_steps = [], []
    for w in cast:
        blk, steps = _cast_rows(w.shape[0], t // tt)
        cast_steps.append(steps)
        cast_specs.append(pl.BlockSpec((blk, w.shape[1]),
                                       lambda i, j, last=steps - 1: (jnp.minimum(j, last), 0)))
    if group_major:
        tok = pl.BlockSpec((None, tt, n, lanes), lambda i, j: (i, j, 0, 0))
    else:
        tok = pl.BlockSpec((tt, None, n, lanes), lambda i, j: (j, i, 0, 0))
    st = pl.BlockSpec((None, nb, n, SUBLANE, lanes), lambda i, j: (i, 0, 0, 0, 0))
    gp = k_k.shape[0]
    par = pl.BlockSpec((None, n, lanes), (lambda i, j: (i, 0, 0)) if gp > 1 else (lambda i, j: (0, 0, 0)))
    y, s_fin, *cast_out = pl.pallas_call(
        functools.partial(_wkv_kernel, tt=tt, n=n, cast_steps=tuple(cast_steps)),
        grid=(g, t // tt),
        in_specs=[tok] * 5 + [st] + [par] * 5 + cast_specs,
        out_specs=[tok, st] + cast_specs,
        out_shape=[jax.ShapeDtypeStruct(r.shape, F32),
                   jax.ShapeDtypeStruct((g, nb, n, SUBLANE, lanes), F32)]
                  + [jax.ShapeDtypeStruct(w.shape, BF16) for w in cast],
        scratch_shapes=[pltpu.VMEM((nb, n, SUBLANE, lanes), F32), pltpu.VMEM((tt, n, lanes), F32),
                        pltpu.VMEM((4, tt, n, lanes), F32), pltpu.VMEM((3, tt, 1, lanes), F32),
                        pltpu.VMEM((n, lanes), F32)],
        compiler_params=_params("arbitrary", "arbitrary"),
        name="wkv_recurrence",
    )(r, k, v, d, a, s0, k_k, k_a, r_k, gn_w, gn_b, *cast)
    return y, s_fin, cast_out


def _to_chain_kernel(x_ref, o_ref, *, nb, nh, n, tt):
    for k in range(n):
        rows = jnp.concatenate([x_ref[b, pl.ds(k, nh, stride=n), :] for b in range(nb)], axis=0)
        o_ref[pl.ds(k, tt, stride=n), :] = rows.T


def _to_chain(xt, *, nb, nh):
    tiles, c, tt = xt.shape
    nt = tiles // nb
    n = c // nh
    return pl.pallas_call(
        functools.partial(_to_chain_kernel, nb=nb, nh=nh, n=n, tt=tt),
        grid=(nt,),
        in_specs=[pl.BlockSpec((nb, None, c, tt), lambda j: (0, j, 0, 0))],
        out_specs=pl.BlockSpec((tt * n, nb * nh), lambda j: (j, 0)),
        out_shape=jax.ShapeDtypeStruct((nt * tt * n, nb * nh), F32),
        compiler_params=_params("arbitrary"),
        name="to_chain",
    )(xt.reshape(nb, nt, c, tt))


def _from_chain_gate_kernel(y_ref, g_ref, o_ref, xt_ref, *, nb, nh, n, tt):
    for v in range(n):
        yv = y_ref[pl.ds(v, tt, stride=n), :].T
        for b in range(nb):
            xt_ref[b, pl.ds(v, nh, stride=n), :] = yv[b * nh:(b + 1) * nh, :]
    for b in range(nb):
        o_ref[b] = (xt_ref[b].T * g_ref[b]).astype(o_ref.dtype)


def _from_chain_gate(y, g, *, nb, nh, tt):
    m, c = g.shape
    t = m // nb
    n = c // nh
    nt = t // tt
    row = pl.BlockSpec((nb, tt, c), lambda j: (0, j, 0))
    out = pl.pallas_call(
        functools.partial(_from_chain_gate_kernel, nb=nb, nh=nh, n=n, tt=tt),
        grid=(nt,),
        in_specs=[pl.BlockSpec((tt * n, nb * nh), lambda j: (j, 0)), row],
        out_specs=row,
        out_shape=jax.ShapeDtypeStruct((nb, t, c), BF16),
        scratch_shapes=[pltpu.VMEM((nb, c, tt), F32)],
        compiler_params=_params("arbitrary"),
        name="from_chain_gate",
    )(y, g.reshape(nb, t, c))
    return out.reshape(m, c)


def _transpose_gate_kernel(y_ref, g_ref, o_ref):
    o_ref[...] = (y_ref[...].T * g_ref[...]).astype(o_ref.dtype)


def _transpose_gate(yt, g):
    tiles, c, tt = yt.shape
    return pl.pallas_call(
        _transpose_gate_kernel,
        grid=(tiles,),
        in_specs=[pl.BlockSpec((None, c, tt), lambda i: (i, 0, 0)),
                  pl.BlockSpec((tt, c), lambda i: (i, 0))],
        out_specs=pl.BlockSpec((tt, c), lambda i: (i, 0)),
        out_shape=jax.ShapeDtypeStruct((tiles * tt, c), BF16),
        compiler_params=_params("arbitrary"),
        name="transpose_gate",
    )(yt, g)


def _pool_kernel(u_ref, h_ref, w_ref, sc_ref, o_ref, ext_ref, *, halo, tt, shift, pos0):
    gi = pl.program_id(1)
    j = pl.program_id(2)

    @pl.when(j == 0)
    def _():
        ext_ref[0:halo, :] = h_ref[...]

    x = u_ref[...]
    ext_ref[halo:halo + tt, :] = x
    row = lax.broadcasted_iota(jnp.int32, (tt, 1), 0) + j * tt
    time = row if shift == 1 else row // shift

    for g, win in enumerate(POOL_WINDOWS):
        @pl.when(gi == g)
        def _(win=win):
            acc = x
            for back in range(1, win):
                lo = halo - back * shift
                acc = acc + ext_ref[lo:lo + tt, :]
            cnt = jnp.minimum(pos0 + time + 1, win).astype(F32)
            d = acc / cnt - x
            y = jnp.dot(d.astype(BF16), w_ref[...], preferred_element_type=F32)
            o_ref[...] = (y * sc_ref[...]).astype(o_ref.dtype)

    if tt >= halo:
        ext_ref[0:halo, :] = ext_ref[tt:tt + halo, :]


def _pool(u, hist, w_pool, scale, *, nseq, shift, tt, pos0):
    m, cp = u.shape
    ng = len(POOL_WINDOWS)
    gw = cp // ng
    rows = m // nseq
    tt = _pick(rows, tt)
    nt = rows // tt
    halo = POOL_HIST * shift
    assert nt == 1 or tt >= halo
    return pl.pallas_call(
        functools.partial(_pool_kernel, halo=halo, tt=tt, shift=shift, pos0=pos0),
        grid=(nseq, ng, nt),
        in_specs=[pl.BlockSpec((tt, gw), lambda s, g, j: (s * nt + j, g)),
                  pl.BlockSpec((None, halo, gw), lambda s, g, j: (s, 0, g)),
                  pl.BlockSpec((None, gw, gw), lambda s, g, j: (g, 0, 0)),
                  pl.BlockSpec((1, gw), lambda s, g, j: (0, g))],
        out_specs=pl.BlockSpec((tt, gw), lambda s, g, j: (s * nt + j, g)),
        out_shape=jax.ShapeDtypeStruct((m, cp), BF16),
        scratch_shapes=[pltpu.VMEM((halo + tt, gw), F32)],
        compiler_params=_params("arbitrary", "arbitrary", "arbitrary"),
        name="pool_mix",
    )(u, hist, w_pool, scale)


def _attend(q, k, v, scale):
    s = lax.dot_general(q, k, (((1,), (1,)), ((), ())), preferred_element_type=F32) * scale
    s = s - jnp.max(s, axis=-1, keepdims=True)
    e = jnp.exp(s)
    pr = e / jnp.sum(e, axis=-1, keepdims=True)
    return jnp.dot(pr.astype(BF16), v, preferred_element_type=F32)


def _attn_prompt_kernel(q_ref, k_ref, v_ref, o_ref, *, scale, heads, hd):
    for h in range(heads):
        cols = slice(h * hd, (h + 1) * hd)
        o = _attend(q_ref[:, cols], k_ref[:, cols].astype(BF16), v_ref[:, cols].astype(BF16), scale)
        o_ref[:, cols] = o.astype(o_ref.dtype)


def _attn_prompt(q, mk, mv, *, batch, heads):
    m, d = q.shape
    t = m // batch
    nm = mk.shape[1]
    hd = d // heads
    tq = _pick(t, 512)
    nq = t // tq
    return pl.pallas_call(
        functools.partial(_attn_prompt_kernel, scale=hd ** -0.5, heads=heads, hd=hd),
        grid=(batch, nq),
        in_specs=[pl.BlockSpec((tq, d), lambda b, i: (b * nq + i, 0)),
                  pl.BlockSpec((None, nm, d), lambda b, i: (b, 0, 0)),
                  pl.BlockSpec((None, nm, d), lambda b, i: (b, 0, 0))],
        out_specs=pl.BlockSpec((tq, d), lambda b, i: (b * nq + i, 0)),
        out_shape=jax.ShapeDtypeStruct((m, d), BF16),
        compiler_params=_params("arbitrary", "arbitrary"),
        name="attn_prompt",
    )(q, mk, mv)


def _attn_sample_kernel(q_ref, k_ref, v_ref, o_ref, *, scale, heads, hd):
    for h in range(heads):
        cols = slice(h * hd, (h + 1) * hd)
        s = lax.dot_general(q_ref[:, cols], k_ref[:, h, :], (((1,), (1,)), ((), ())),
                            preferred_element_type=F32) * scale
        s = s - jnp.max(s, axis=-1, keepdims=True)
        e = jnp.exp(s)
        pr = e / jnp.sum(e, axis=-1, keepdims=True)
        o_ref[:, cols] = jnp.dot(pr, v_ref[:, h, :], preferred_element_type=F32)


def _attn_sample(q, mk, mv, layer):
    b, t, d = q.shape
    _, _, nm, heads, hd = mk.shape
    kv_spec = pl.BlockSpec((None, None, nm, heads, hd), lambda i: (layer, i, 0, 0, 0))
    return pl.pallas_call(
        functools.partial(_attn_sample_kernel, scale=hd ** -0.5, heads=heads, hd=hd),
        grid=(b,),
        in_specs=[pl.BlockSpec((None, t, d), lambda i: (i, 0, 0)), kv_spec, kv_spec],
        out_specs=pl.BlockSpec((None, t, d), lambda i: (i, 0, 0)),
        out_shape=jax.ShapeDtypeStruct((b, t, d), F32),
        compiler_params=_params("arbitrary"),
        name="attn_sample",
    )(q, mk, mv)


def _ffn_up_kernel(h_ref, hp_ref, wg_ref, wv_ref, cw_ref, cb_ref, st_ref, act_ref, zt_ref, zext_ref,
                   *, halo, tm, shift, recompute_halo):
    h = h_ref[...]
    z = jnp.dot(h, wg_ref[...], preferred_element_type=F32)
    val = jnp.dot(h, wv_ref[...], preferred_element_type=F32)
    zprev = st_ref[...]
    if recompute_halo:
        zhalo = jnp.dot(hp_ref[...], wg_ref[...], preferred_element_type=F32)
        zprev = jnp.where(pl.program_id(1) == 0, zprev, zhalo)
    zext_ref[0:halo, :] = zprev
    zext_ref[halo:halo + tm, :] = z
    zm1 = zext_ref[halo - shift:halo - shift + tm, :]
    zm2 = zext_ref[halo - 2 * shift:halo - 2 * shift + tm, :]
    zc = cb_ref[...] + cw_ref[0:1, :] * zm2 + cw_ref[1:2, :] * zm1 + cw_ref[2:3, :] * z
    act_ref[...] = (zc * _sigmoid(zc) * val).astype(act_ref.dtype)
    zt_ref[...] = zext_ref[tm:tm + halo, :]


def _ffn_up(h, wg, wv, cw, cb, state, *, nseq, shift, halo, tm, tn):
    m, d = h.shape
    _, fp = wg.shape
    rows = m // nseq
    tm = _pick(rows, tm)
    nt = rows // tm
    tn = min(tn, fp)
    assert tm >= halo and (nt == 1 or halo == SUBLANE)
    hb = tm // SUBLANE
    return pl.pallas_call(
        functools.partial(_ffn_up_kernel, halo=halo, tm=tm, shift=shift, recompute_halo=nt > 1),
        grid=(nseq, nt, pl.cdiv(fp, tn)),
        in_specs=[pl.BlockSpec((tm, d), lambda s, i, j: (s * nt + i, 0)),
                  pl.BlockSpec((SUBLANE, d), lambda s, i, j: (jnp.maximum((s * nt + i) * hb - 1, 0), 0)),
                  pl.BlockSpec((d, tn), lambda s, i, j: (0, j)),
                  pl.BlockSpec((d, tn), lambda s, i, j: (0, j)),
                  pl.BlockSpec((CONV_W, tn), lambda s, i, j: (0, j)),
                  pl.BlockSpec((1, tn), lambda s, i, j: (0, j)),
                  pl.BlockSpec((None, halo, tn), lambda s, i, j: (s, 0, j))],
        out_specs=[pl.BlockSpec((tm, tn), lambda s, i, j: (s * nt + i, j)),
                   pl.BlockSpec((halo, tn), lambda s, i, j: (s * nt + i, j))],
        out_shape=[jax.ShapeDtypeStruct((m, fp), BF16),
                   jax.ShapeDtypeStruct((nseq * nt * halo, fp), F32)],
        scratch_shapes=[pltpu.VMEM((halo + tm, tn), F32)],
        compiler_params=_params("arbitrary", "arbitrary", "arbitrary"),
        name="ffn_up",
    )(h, h, wg, wv, cw, cb, state)


def kernel(x_prompt, x_sample, mem_prompt, cache_mem_k, cache_mem_v, state_shift, state_wkv, state_pool,
           state_conv, g_mix, w_in, mu_shift, w0, b_w, a0, b_a, b_g, k_k, k_a, r_k, gn_w, gn_b, w_pool,
           pool_scale, w_out, g_mem, w_mk, w_mv, g_attn, w_mq, w_mo, g_ffn, w_gate, w_val, conv_w, conv_b,
           w_down, g_final):
    bp, tp, d = x_prompt.shape
    bs, ts, _ = x_sample.shape
    depth = w_in.shape[0]
    c = w0.shape[1]
    nh = c // HEAD_SIZE
    nvb = HEAD_SIZE // SUBLANE
    cp = pool_scale.shape[1]
    pw = mu_shift.shape[1]
    lora = pw - 3 * c
    lp = _round_up(lora, LANE)
    ca = 3 * c + lp
    dl, al, gl = b_w.shape[1], b_a.shape[1], b_g.shape[1]
    f = w_gate.shape[2]
    fp = f
    nmem = mem_prompt.shape[1]
    nbuf = state_pool.shape[2]
    mp, ms = bp * tp, bs * ts

    xp = x_prompt.reshape(mp, d)
    xs = x_sample.transpose(1, 0, 2).reshape(ms, d)

    outs = {n: [] for n in ("pm_k", "pm_v", "p_sh", "p_wkv", "p_pool", "p_conv",
                            "s_sh", "s_wkv", "s_pool", "s_conv")}

    for l in range(depth):
        wa = jnp.pad(w_in[l][:, :pw], ((0, 0), (0, ca - pw))).astype(BF16)
        wu = w_in[l][:, pw:].astype(BF16)
        mu = jnp.pad(mu_shift[l], (0, ca - pw)).reshape(1, ca)
        bw = jnp.zeros((lp, c), F32).at[0:dl].set(b_w[l]).astype(BF16)
        ba = jnp.zeros((lp, c), F32).at[dl:dl + al].set(b_a[l]).astype(BF16)
        bg = jnp.zeros((lp, c), F32).at[dl + al:dl + al + gl].set(b_g[l]).astype(BF16)
        w0l = w0[l].reshape(1, c)
        a0l = a0[l].reshape(1, c)
        wpool = w_pool[l].astype(BF16)
        pscale = pool_scale[l].reshape(1, cp)
        wmk = w_mk[l].astype(BF16)
        wmv = w_mv[l].astype(BF16)
        late = {}
        cw = conv_w[l]
        cb = conv_b[l].reshape(1, fp)

        def head_tile(p):
            return p.reshape(nh, HEAD_SIZE).T

        chain_p = [jnp.tile(head_tile(p), (1, bp)).reshape(1, HEAD_SIZE, bp * nh)
                   for p in (k_k[l], k_a[l], r_k[l].reshape(-1), gn_w[l], gn_b[l])]
        chain_s = [jnp.broadcast_to(p.reshape(nh, HEAD_SIZE, 1), (nh, HEAD_SIZE, bs))
                   for p in (k_k[l], k_a[l], r_k[l].reshape(-1), gn_w[l], gn_b[l])]

        mn = _rmsnorm(mem_prompt.reshape(bp * nmem, d), g_mem[l], BF16)
        mk = _matmul(mn, wmk, out_dtype=F32, name="mem_k")
        mv = _matmul(mn, wmv, out_dtype=F32, name="mem_v")
        outs["pm_k"].append(mk.reshape(bp, nmem, MEM_HEADS, d // MEM_HEADS))
        outs["pm_v"].append(mv.reshape(bp, nmem, MEM_HEADS, d // MEM_HEADS))

        def layer(x, *, prompt):
            m = x.shape[0]
            if prompt:
                nseq, shift, halo1, halo, pos0 = bp, 1, SUBLANE, SUBLANE, 0
                st_shift = jnp.zeros((bp, halo1, ca), F32)
                hist = jnp.zeros((bp, POOL_HIST, cp), F32)
                st_conv = jnp.zeros((bp, halo, fp), F32)
            else:
                nseq, shift, halo1, halo, pos0 = 1, bs, bs, (CONV_W - 1) * bs, PAST_LEN
                st_shift = jnp.pad(state_shift[l], ((0, 0), (0, ca - pw))).reshape(1, halo1, ca)
                hist = jnp.pad(state_pool[l].transpose(1, 0, 2), ((POOL_HIST - nbuf, 0), (0, 0), (0, 0)))
                hist = hist.reshape(1, POOL_HIST * bs, cp)
                st_conv = state_conv[l].transpose(1, 0, 2).reshape(1, halo, fp)

            xn = _rmsnorm(x, g_mix[l], BF16)
            proj = _matmul(xn, wa, out_dtype=F32, name="in_proj_rwkv")
            u = _matmul(xn, wu, out_dtype=F32, name="in_proj_pool")
            tile = min(LANE, tp) if prompt else bs
            *rkvda, g = _rwkv_prep(proj, st_shift, mu, w0l, a0l, bw, ba, bg,
                                   nseq=nseq, shift=shift, halo=halo1, tt=tile, c=c)
            if prompt:
                chains = [_to_chain(x, nb=bp, nh=nh).reshape(1, tp, HEAD_SIZE, bp * nh) for x in rkvda]
                s0 = jnp.zeros((1, nvb, HEAD_SIZE, SUBLANE, bp * nh), F32)
                y, s_fin, cast_out = _wkv(*chains, s0, *chain_p, tt=16, group_major=True,
                                          cast=(w_out[l], w_mq[l], w_mo[l], w_gate[l], w_val[l], w_down[l]))
                late.update(zip(("wo", "wmq", "wmo", "wg", "wv", "wd"), cast_out))
                r_out = _from_chain_gate(y.reshape(tp * HEAD_SIZE, bp * nh), g, nb=bp, nh=nh, tt=tile)
                new_wkv = s_fin.reshape(nvb, HEAD_SIZE, SUBLANE, bp, nh).transpose(3, 4, 0, 2, 1)
                new_wkv = new_wkv.reshape(bp, nh, HEAD_SIZE, HEAD_SIZE)
            else:
                chains = [x.reshape(ts, nh, HEAD_SIZE, bs) for x in rkvda]
                s0 = state_wkv[l].reshape(bs, nh, nvb, SUBLANE, HEAD_SIZE).transpose(1, 2, 4, 3, 0)
                y, s_fin, _ = _wkv(*chains, s0, *chain_s, tt=ts, group_major=False)
                r_out = _transpose_gate(y.reshape(ts, c, bs), g)
                new_wkv = s_fin.transpose(4, 0, 1, 3, 2).reshape(bs, nh, HEAD_SIZE, HEAD_SIZE)
            p_out = _pool(u, hist, wpool, pscale, nseq=nseq, shift=shift,
                          tt=256 if prompt else m, pos0=pos0)
            wmq, wmo, wg, wv, wd = (late[n] for n in ("wmq", "wmo", "wg", "wv", "wd"))
            x1 = _matmul2_res(r_out, p_out, late["wo"], x)

            hq = _rmsnorm(x1, g_attn[l], BF16)
            if prompt:
                q = _matmul(hq, wmq, out_dtype=BF16, name="attn_q")
                o = _attn_prompt(q, mk.reshape(bp, nmem, d), mv.reshape(bp, nmem, d),
                                 batch=bp, heads=MEM_HEADS)
            else:
                q = _matmul(hq, wmq, out_dtype=F32, name="attn_q")
                q = q.reshape(ts, bs, d).transpose(1, 0, 2)
                o = _attn_sample(q, cache_mem_k, cache_mem_v, l)
                o = o.transpose(1, 0, 2).reshape(m, d)
            x2 = _matmul(o, wmo, out_dtype=F32, res=x1, name="attn_o")

            hf = _rmsnorm(x2, g_ffn[l], BF16)
            act, ztail = _ffn_up(hf, wg, wv, cw, cb, st_conv, nseq=nseq, shift=shift, halo=halo,
                                 tm=1024, tn=512)
            x3 = _matmul(act, wd, out_dtype=F32, res=x2, tm=512, tn=512, name="ffn_down")

            if prompt:
                new_shift = proj.reshape(bp, tp, ca)[:, -1, :pw]
                new_pool = u.reshape(bp, tp, cp)[:, tp - nbuf:]
                new_conv = ztail.reshape(bp, -1, halo, fp)[:, -1, halo - (CONV_W - 1):, :f]
            else:
                new_shift = proj[(ts - 1) * bs:, :pw]
                u_b = u.reshape(ts, bs, cp).transpose(1, 0, 2)
                new_pool = jnp.concatenate([state_pool[l], u_b], axis=1)[:, -nbuf:]
                new_conv = ztail.reshape(CONV_W - 1, bs, fp)[:, :, :f].transpose(1, 0, 2)
            return x3, new_shift, new_wkv, new_pool, new_conv

        xp, sh, wk, po, co = layer(xp, prompt=True)
        outs["p_sh"].append(sh)
        outs["p_wkv"].append(wk)
        outs["p_pool"].append(po)
        outs["p_conv"].append(co)
        xs, sh, wk, po, co = layer(xs, prompt=False)
        outs["s_sh"].append(sh)
        outs["s_wkv"].append(wk)
        outs["s_pool"].append(po)
        outs["s_conv"].append(co)

    y_prompt = _rmsnorm(xp, g_final, F32).reshape(bp, tp, d)
    y_sample = _rmsnorm(xs, g_final, F32).reshape(ts, bs, d).transpose(1, 0, 2)
    stk = lambda n: jnp.stack(outs[n])
    return (y_prompt, y_sample, stk("pm_k"), stk("pm_v"), stk("p_sh"), stk("p_wkv"), stk("p_pool"),
            stk("p_conv"), stk("s_sh"), stk("s_wkv"), stk("s_pool"), stk("s_conv"))
```

```python
import functools

import jax
import jax.numpy as jnp
from jax import lax
from jax.experimental import pallas as pl
from jax.experimental.pallas import tpu as pltpu

F32 = jnp.float32
BF16 = jnp.bfloat16

HEAD_SIZE = 64
POOL_WINDOWS = (2, 4, 8, 16)
POOL_HIST = 16
MEM_HEADS = 4
CONV_W = 3
PAST_LEN = 16384
NORM_EPS = 1e-6
GN_EPS = 64e-5
EXP_M_HALF = 0.6065306597126334
LANE = 128
SUBLANE = 8
VMEM_LIMIT = 56 * 1024 * 1024


def _pick(dim, pref, mult=SUBLANE):
    if dim <= pref:
        return dim
    t = (pref // mult) * mult
    while t >= mult:
        if dim % t == 0:
            return t
        t -= mult
    return dim


def _round_up(x, m):
    return (x + m - 1) // m * m


def _params(*sem):
    return pltpu.CompilerParams(dimension_semantics=sem, vmem_limit_bytes=VMEM_LIMIT)


def _sigmoid(x):
    return 1.0 / (1.0 + jnp.exp(-x))


def _rmsnorm_kernel(x_ref, g_ref, o_ref):
    x = x_ref[...].astype(F32)
    ms = jnp.mean(x * x, axis=-1, keepdims=True)
    o_ref[...] = (x * lax.rsqrt(ms + NORM_EPS) * g_ref[...]).astype(o_ref.dtype)


def _rmsnorm(x, g, out_dtype):
    m, d = x.shape
    tr = _pick(m, 256)
    return pl.pallas_call(
        _rmsnorm_kernel,
        grid=(m // tr,),
        in_specs=[pl.BlockSpec((tr, d), lambda i: (i, 0)),
                  pl.BlockSpec((1, d), lambda i: (0, 0))],
        out_specs=pl.BlockSpec((tr, d), lambda i: (i, 0)),
        out_shape=jax.ShapeDtypeStruct((m, d), out_dtype),
        compiler_params=_params("arbitrary"),
        name="rmsnorm",
    )(x, g.reshape(1, d).astype(F32))


def _row_rstd(x):
    xf = x.astype(F32)
    return lax.rsqrt(jnp.mean(xf * xf, axis=-1, keepdims=True) + NORM_EPS)


def _mm_kernel(*refs, has_res, emit_bf16, row_norm):
    refs = list(refs)
    x_ref, w_ref = refs[:2]
    del refs[:2]
    res_ref = refs.pop(0) if has_res else None
    o_ref = refs.pop(0)
    ob_ref = refs.pop(0) if emit_bf16 else None
    rstd_ref = refs.pop(0) if row_norm else None
    if row_norm:
        @pl.when(pl.program_id(1) == 0)
        def _():
            rstd_ref[...] = _row_rstd(x_ref[...])
    acc = jnp.dot(x_ref[...].astype(BF16), w_ref[...], preferred_element_type=F32)
    if row_norm:
        acc = acc * rstd_ref[...]
    if has_res:
        acc = acc + res_ref[...]
    o_ref[...] = acc.astype(o_ref.dtype)
    if emit_bf16:
        ob_ref[...] = acc.astype(BF16)


def _matmul(x, w, *, out_dtype, res=None, emit_bf16=False, row_norm=False, tm=1024, tn=512, name="matmul"):
    m, kd = x.shape
    _, n = w.shape
    tm = _pick(m, tm)
    tn = _pick(n, tn, LANE)
    tile = pl.BlockSpec((tm, tn), lambda i, j: (i, j))
    in_specs = [pl.BlockSpec((tm, kd), lambda i, j: (i, 0)),
                pl.BlockSpec((kd, tn), lambda i, j: (0, j))]
    args = [x, w]
    if res is not None:
        in_specs.append(tile)
        args.append(res)
    out_specs, out_shape = [tile], [jax.ShapeDtypeStruct((m, n), out_dtype)]
    if emit_bf16:
        out_specs.append(tile)
        out_shape.append(jax.ShapeDtypeStruct((m, n), BF16))
    out = pl.pallas_call(
        functools.partial(_mm_kernel, has_res=res is not None, emit_bf16=emit_bf16, row_norm=row_norm),
        grid=(m // tm, n // tn),
        in_specs=in_specs,
        out_specs=out_specs,
        out_shape=out_shape,
        scratch_shapes=[pltpu.VMEM((tm, 1), F32)] if row_norm else [],
        compiler_params=_params("arbitrary", "arbitrary"),
        name=name,
    )(*args)
    return out if emit_bf16 else out[0]


def _mm2_kernel(x1_ref, x2_ref, w1_ref, w2_ref, res_ref, o_ref, ob_ref):
    acc = jnp.dot(x1_ref[...], w1_ref[...], preferred_element_type=F32)
    acc = acc + jnp.dot(x2_ref[...], w2_ref[...], preferred_element_type=F32)
    acc = acc + res_ref[...]
    o_ref[...] = acc
    ob_ref[...] = acc.astype(BF16)


def _matmul2_res(x1, x2, w, res, *, tm=1024, tn=512):
    m, k1 = x1.shape
    _, k2 = x2.shape
    kw, n = w.shape
    assert k1 == k2 and kw == k1 + k2
    tm = _pick(m, tm)
    tn = _pick(n, tn, LANE)
    return pl.pallas_call(
        _mm2_kernel,
        grid=(m // tm, n // tn),
        in_specs=[pl.BlockSpec((tm, k1), lambda i, j: (i, 0)),
                  pl.BlockSpec((tm, k2), lambda i, j: (i, 0)),
                  pl.BlockSpec((k1, tn), lambda i, j: (0, j)),
                  pl.BlockSpec((k2, tn), lambda i, j: (1, j)),
                  pl.BlockSpec((tm, tn), lambda i, j: (i, j))],
        out_specs=[pl.BlockSpec((tm, tn), lambda i, j: (i, j))] * 2,
        out_shape=[jax.ShapeDtypeStruct((m, n), F32), jax.ShapeDtypeStruct((m, n), BF16)],
        compiler_params=_params("arbitrary", "arbitrary"),
        name="out_proj",
    )(x1, x2, w, w, res)


def _prep_kernel(p_ref, st_ref, mu_ref, w0_ref, a0_ref, bw_ref, ba_ref, bg_ref,
                 r_ref, k_ref, v_ref, d_ref, a_ref, g_ref, ext_ref, *, halo, tt, shift, c):
    @pl.when(pl.program_id(1) == 0)
    def _():
        ext_ref[0:halo, :] = st_ref[...]

    p = p_ref[...]
    ext_ref[halo:halo + tt, :] = p
    prev = ext_ref[halo - shift:halo - shift + tt, :]
    xs = p + (prev - p) * mu_ref[...]
    ext_ref[0:halo, :] = ext_ref[tt:tt + halo, :]

    r_ref[...] = xs[:, 0:c].T
    k_ref[...] = xs[:, c:2 * c].T
    v_ref[...] = xs[:, 2 * c:3 * c].T
    lo = xs[:, 3 * c:]
    wpre = jnp.dot(jnp.tanh(lo).astype(BF16), bw_ref[...], preferred_element_type=F32)
    apre = jnp.dot(lo.astype(BF16), ba_ref[...], preferred_element_type=F32)
    g_ref[...] = jnp.dot(_sigmoid(lo).astype(BF16), bg_ref[...], preferred_element_type=F32)
    ez = jnp.exp(-(w0_ref[...] + wpre))
    d_ref[...] = jnp.exp(-EXP_M_HALF / (1.0 + ez)).T
    a_ref[...] = _sigmoid(a0_ref[...] + apre).T


def _rwkv_prep(proj, state, mu, w0, a0, bw, ba, bg, *, nseq, shift, halo, tt, c):
    m, ca = proj.shape
    rows = m // nseq
    assert rows % tt == 0
    nt = rows // tt
    row_spec = lambda width: pl.BlockSpec((tt, width), lambda s, j: (s * nt + j, 0))
    tr_spec = pl.BlockSpec((None, c, tt), lambda s, j: (s * nt + j, 0, 0))
    full = lambda arr: pl.BlockSpec(arr.shape, lambda s, j: (0,) * arr.ndim)
    out_t = jax.ShapeDtypeStruct((nseq * nt, c, tt), F32)
    return pl.pallas_call(
        functools.partial(_prep_kernel, halo=halo, tt=tt, shift=shift, c=c),
        grid=(nseq, nt),
        in_specs=[row_spec(ca),
                  pl.BlockSpec((None, halo, ca), lambda s, j: (s, 0, 0)),
                  full(mu), full(w0), full(a0), full(bw), full(ba), full(bg)],
        out_specs=[tr_spec] * 5 + [row_spec(c)],
        out_shape=[out_t] * 5 + [jax.ShapeDtypeStruct((m, c), F32)],
        scratch_shapes=[pltpu.VMEM((halo + tt, ca), F32)],
        compiler_params=_params("arbitrary", "arbitrary"),
        name="rwkv_prep",
    )(proj, state, mu, w0, a0, bw, ba, bg)


V_GROUP = 4
K_BLOCK = 32


def _wkv_kernel(*refs, tt, n, cast_steps):
    (r_ref, k_ref, v_ref, d_ref, a_ref, s0_ref, kk_ref, ka_ref, rk_ref, gw_ref, gb_ref) = refs[:11]
    nc = len(cast_steps)
    w_refs = refs[11:11 + nc]
    wg_refs = refs[11 + nc:11 + 2 * nc]
    o_ref, sf_ref = refs[11 + 2 * nc:13 + 2 * nc]
    wb_refs = refs[13 + 2 * nc:13 + 3 * nc]
    s_ref, y_ref, row_ref, dot_ref, plast_ref = refs[13 + 3 * nc:]

    @pl.when(pl.program_id(1) == 0)
    def _():
        s_ref[...] = s0_ref[...]

    for w_ref, wg_ref, wb_ref, steps in zip(w_refs, wg_refs, wb_refs, cast_steps):
        @pl.when(pl.program_id(1) < steps)
        def _(w_ref=w_ref, wg_ref=wg_ref, wb_ref=wb_ref):
            wb_ref[...] = (w_ref[...] * wg_ref[...]).astype(wb_ref.dtype)

    k_k = kk_ref[...]
    k_a = ka_ref[...]
    r_k = rk_ref[...]
    gn_w = gw_ref[...]
    gn_b = gb_ref[...]
    inv_n = 1.0 / n
    lanes = s_ref.shape[-1]

    r = r_ref[...]
    k = k_ref[...]
    w = d_ref[...]
    a = a_ref[...]
    kk = k * k_k
    norm = jnp.sqrt(jnp.sum(kk * kk, axis=1, keepdims=True))
    kk = kk / jnp.maximum(norm, 1e-12)
    k2 = k * (1.0 + (a - 1.0) * k_a)
    bv = kk * a
    p = None
    for t in range(tt):
        row_ref[0, t] = -kk[t] if p is None else -kk[t] * p
        p = w[t] if p is None else p * w[t]
        inv_p = 1.0 / p
        row_ref[1, t] = r[t] * p
        row_ref[2, t] = bv[t] * inv_p
        row_ref[3, t] = k2[t] * inv_p
    plast_ref[...] = p
    dot_ref[0] = jnp.sum(bv * r, axis=1, keepdims=True)
    dot_ref[1] = jnp.sum(k2 * r, axis=1, keepdims=True)
    dot_ref[2] = jnp.sum(r * k2 * r_k, axis=1, keepdims=True)

    def token(t, carry):
        b_r = dot_ref[0, t]
        k_r = dot_ref[1, t]

        def vgroup(gidx, c2):
            vb0 = gidx * V_GROUP
            zero = jnp.zeros((SUBLANE, lanes), F32)

            def reduce_keys(kb, acc):
                sa_acc, q_acc = list(acc[0]), list(acc[1])
                for j in range(K_BLOCK):
                    kx = kb * K_BLOCK + j
                    a_row = row_ref[0, t, pl.ds(kx, 1), :]
                    q_row = row_ref[1, t, pl.ds(kx, 1), :]
                    for i in range(V_GROUP):
                        s_k = s_ref[vb0 + i, kx]
                        sa_acc[i] = sa_acc[i] + s_k * a_row
                        q_acc[i] = q_acc[i] + s_k * q_row
                return tuple(sa_acc), tuple(q_acc)

            sa8, q8 = lax.fori_loop(0, n // K_BLOCK, reduce_keys, ((zero,) * V_GROUP, (zero,) * V_GROUP))
            base = pl.multiple_of(vb0 * SUBLANE, SUBLANE)
            v8 = [v_ref[t, pl.ds(base + i * SUBLANE, SUBLANE), :] for i in range(V_GROUP)]

            def update_keys(kb, c3):
                for j in range(K_BLOCK):
                    kx = kb * K_BLOCK + j
                    b_row = row_ref[2, t, pl.ds(kx, 1), :]
                    k_row = row_ref[3, t, pl.ds(kx, 1), :]
                    for i in range(V_GROUP):
                        s_ref[vb0 + i, kx] = s_ref[vb0 + i, kx] + sa8[i] * b_row + v8[i] * k_row
                return c3

            lax.fori_loop(0, n // K_BLOCK, update_keys, 0)
            for i in range(V_GROUP):
                y_ref[t, pl.ds(base + i * SUBLANE, SUBLANE), :] = q8[i] + sa8[i] * b_r + v8[i] * k_r
            return c2

        lax.fori_loop(0, n // (SUBLANE * V_GROUP), vgroup, 0)
        return carry

    lax.fori_loop(0, tt, token, 0)

    def rescale(vb, carry):
        for kx in range(n):
            s_ref[vb, kx] = s_ref[vb, kx] * plast_ref[kx:kx + 1, :]
        return carry

    lax.fori_loop(0, n // SUBLANE, rescale, 0)

    y = y_ref[...]
    mean = jnp.sum(y, axis=1, keepdims=True) * inv_n
    yc = y - mean
    var = jnp.sum(yc * yc, axis=1, keepdims=True) * inv_n
    o_ref[...] = yc * lax.rsqrt(var + GN_EPS) * gn_w + gn_b + dot_ref[2] * v_ref[...]

    @pl.when(pl.program_id(1) == pl.num_programs(1) - 1)
    def _():
        sf_ref[...] = s_ref[...]


def _cast_rows(rows, nsteps):
    blk = _round_up(-(-rows // nsteps), 2 * SUBLANE)
    while rows % blk:
        blk += 2 * SUBLANE
    return blk, rows // blk


def _wkv(r, k, v, d, a, s0, k_k, k_a, r_k, gn_w, gn_b, *, tt, group_major, cast=()):
    if group_major:
        g, t, n, lanes = r.shape
    else:
        t, g, n, lanes = r.shape
    tt = _pick(t, tt, 1)
    nb = n // SUBLANE
    assert not cast or g == 1
    cast_specs, gain_specs, cast_steps = [], [], []
    for w, _ in cast:
        blk, steps = _cast_rows(w.shape[0], t // tt)
        cast_steps.append(steps)
        index = lambda i, j, last=steps - 1: (jnp.minimum(j, last), 0)
        cast_specs.append(pl.BlockSpec((blk, w.shape[1]), index))
        gain_specs.append(pl.BlockSpec((blk, 1), index))
    weights = [w for w, _ in cast]
    gains = [jnp.ones((w.shape[0], 1), F32) if gain is None else gain.reshape(-1, 1).astype(F32)
             for w, gain in cast]
    if group_major:
        tok = pl.BlockSpec((None, tt, n, lanes), lambda i, j: (i, j, 0, 0))
    else:
        tok = pl.BlockSpec((tt, None, n, lanes), lambda i, j: (j, i, 0, 0))
    st = pl.BlockSpec((None, nb, n, SUBLANE, lanes), lambda i, j: (i, 0, 0, 0, 0))
    gp = k_k.shape[0]
    par = pl.BlockSpec((None, n, lanes), (lambda i, j: (i, 0, 0)) if gp > 1 else (lambda i, j: (0, 0, 0)))
    y, s_fin, *cast_out = pl.pallas_call(
        functools.partial(_wkv_kernel, tt=tt, n=n, cast_steps=tuple(cast_steps)),
        grid=(g, t // tt),
        in_specs=[tok] * 5 + [st] + [par] * 5 + cast_specs + gain_specs,
        out_specs=[tok, st] + cast_specs,
        out_shape=[jax.ShapeDtypeStruct(r.shape, F32),
                   jax.ShapeDtypeStruct((g, nb, n, SUBLANE, lanes), F32)]
                  + [jax.ShapeDtypeStruct(w.shape, BF16) for w in weights],
        scratch_shapes=[pltpu.VMEM((nb, n, SUBLANE, lanes), F32), pltpu.VMEM((tt, n, lanes), F32),
                        pltpu.VMEM((4, tt, n, lanes), F32), pltpu.VMEM((3, tt, 1, lanes), F32),
                        pltpu.VMEM((n, lanes), F32)],
        compiler_params=_params("arbitrary", "arbitrary"),
        name="wkv_recurrence",
    )(r, k, v, d, a, s0, k_k, k_a, r_k, gn_w, gn_b, *weights, *gains)
    return y, s_fin, cast_out


def _to_chain_kernel(x_ref, o_ref, *, nb, nh, n, tt):
    for kb in range(n // SUBLANE):
        ks = slice(kb * SUBLANE, (kb + 1) * SUBLANE)
        per_key = jnp.concatenate([jnp.swapaxes(x_ref[b, :, ks, :], 0, 1) for b in range(nb)], axis=1)
        chains_last = jnp.swapaxes(per_key, 1, 2)
        o_ref[:, ks, :] = jnp.swapaxes(chains_last, 0, 1)


def _to_chain(xt, *, nb, nh):
    tiles, c, tt = xt.shape
    nt = tiles // nb
    n = c // nh
    return pl.pallas_call(
        functools.partial(_to_chain_kernel, nb=nb, nh=nh, n=n, tt=tt),
        grid=(nt,),
        in_specs=[pl.BlockSpec((nb, None, nh, n, tt), lambda j: (0, j, 0, 0, 0))],
        out_specs=pl.BlockSpec((tt, n, nb * nh), lambda j: (j, 0, 0)),
        out_shape=jax.ShapeDtypeStruct((nt * tt, n, nb * nh), F32),
        compiler_params=_params("arbitrary"),
        name="to_chain",
    )(xt.reshape(nb, nt, nh, n, tt))


def _from_chain_gate_kernel(y_ref, g_ref, o_ref, xt_ref, *, nb, nh, n, tt):
    for vb in range(n // SUBLANE):
        vs = slice(vb * SUBLANE, (vb + 1) * SUBLANE)
        chains_mid = jnp.swapaxes(jnp.swapaxes(y_ref[:, vs, :], 0, 1), 1, 2)
        for b in range(nb):
            xt_ref[b, :, vs, :] = jnp.swapaxes(chains_mid[:, b * nh:(b + 1) * nh, :], 0, 1)
    for b in range(nb):
        o_ref[b] = (xt_ref[b].reshape(nh * n, tt).T * g_ref[b]).astype(o_ref.dtype)


def _from_chain_gate(y, g, *, nb, nh, tt):
    m, c = g.shape
    t = m // nb
    n = c // nh
    nt = t // tt
    row = pl.BlockSpec((nb, tt, c), lambda j: (0, j, 0))
    out = pl.pallas_call(
        functools.partial(_from_chain_gate_kernel, nb=nb, nh=nh, n=n, tt=tt),
        grid=(nt,),
        in_specs=[pl.BlockSpec((tt, n, nb * nh), lambda j: (j, 0, 0)), row],
        out_specs=row,
        out_shape=jax.ShapeDtypeStruct((nb, t, c), BF16),
        scratch_shapes=[pltpu.VMEM((nb, nh, n, tt), F32)],
        compiler_params=_params("arbitrary"),
        name="from_chain_gate",
    )(y, g.reshape(nb, t, c))
    return out.reshape(m, c)


def _transpose_gate_kernel(y_ref, g_ref, o_ref):
    o_ref[...] = (y_ref[...].T * g_ref[...]).astype(o_ref.dtype)


def _transpose_gate(yt, g):
    tiles, c, tt = yt.shape
    return pl.pallas_call(
        _transpose_gate_kernel,
        grid=(tiles,),
        in_specs=[pl.BlockSpec((None, c, tt), lambda i: (i, 0, 0)),
                  pl.BlockSpec((tt, c), lambda i: (i, 0))],
        out_specs=pl.BlockSpec((tt, c), lambda i: (i, 0)),
        out_shape=jax.ShapeDtypeStruct((tiles * tt, c), BF16),
        compiler_params=_params("arbitrary"),
        name="transpose_gate",
    )(yt, g)


def _pool_kernel(u_ref, h_ref, w_ref, sc_ref, o_ref, ext_ref, *, halo, tt, shift, pos0):
    gi = pl.program_id(1)
    j = pl.program_id(2)

    @pl.when(j == 0)
    def _():
        ext_ref[0:halo, :] = h_ref[...]

    x = u_ref[...]
    ext_ref[halo:halo + tt, :] = x
    row = lax.broadcasted_iota(jnp.int32, (tt, 1), 0) + j * tt
    time = row if shift == 1 else row // shift

    for g, win in enumerate(POOL_WINDOWS):
        @pl.when(gi == g)
        def _(win=win):
            acc = x
            for back in range(1, win):
                lo = halo - back * shift
                acc = acc + ext_ref[lo:lo + tt, :]
            cnt = jnp.minimum(pos0 + time + 1, win).astype(F32)
            d = acc / cnt - x
            y = jnp.dot(d.astype(BF16), w_ref[...], preferred_element_type=F32)
            o_ref[...] = (y * sc_ref[...]).astype(o_ref.dtype)

    if tt >= halo:
        ext_ref[0:halo, :] = ext_ref[tt:tt + halo, :]


def _pool(u, hist, w_pool, scale, *, nseq, shift, tt, pos0):
    m, cp = u.shape
    ng = len(POOL_WINDOWS)
    gw = cp // ng
    rows = m // nseq
    tt = _pick(rows, tt)
    nt = rows // tt
    halo = POOL_HIST * shift
    assert nt == 1 or tt >= halo
    return pl.pallas_call(
        functools.partial(_pool_kernel, halo=halo, tt=tt, shift=shift, pos0=pos0),
        grid=(nseq, ng, nt),
        in_specs=[pl.BlockSpec((tt, gw), lambda s, g, j: (s * nt + j, g)),
                  pl.BlockSpec((None, halo, gw), lambda s, g, j: (s, 0, g)),
                  pl.BlockSpec((None, gw, gw), lambda s, g, j: (g, 0, 0)),
                  pl.BlockSpec((1, gw), lambda s, g, j: (0, g))],
        out_specs=pl.BlockSpec((tt, gw), lambda s, g, j: (s * nt + j, g)),
        out_shape=jax.ShapeDtypeStruct((m, cp), BF16),
        scratch_shapes=[pltpu.VMEM((halo + tt, gw), F32)],
        compiler_params=_params("arbitrary", "arbitrary", "arbitrary"),
        name="pool_mix",
    )(u, hist, w_pool, scale)


def _attend(q, k, v, scale):
    s = lax.dot_general(q, k, (((1,), (1,)), ((), ())), preferred_element_type=F32) * scale
    s = s - jnp.max(s, axis=-1, keepdims=True)
    e = jnp.exp(s)
    pr = e / jnp.sum(e, axis=-1, keepdims=True)
    return jnp.dot(pr.astype(BF16), v, preferred_element_type=F32)


def _attn_prompt_kernel(q_ref, k_ref, v_ref, o_ref, *, scale, heads, hd):
    for h in range(heads):
        cols = slice(h * hd, (h + 1) * hd)
        o = _attend(q_ref[:, cols], k_ref[:, cols].astype(BF16), v_ref[:, cols].astype(BF16), scale)
        o_ref[:, cols] = o.astype(o_ref.dtype)


def _attn_prompt(q, mk, mv, *, batch, heads):
    m, d = q.shape
    t = m // batch
    nm = mk.shape[1]
    hd = d // heads
    tq = _pick(t, 512)
    nq = t // tq
    return pl.pallas_call(
        functools.partial(_attn_prompt_kernel, scale=hd ** -0.5, heads=heads, hd=hd),
        grid=(batch, nq),
        in_specs=[pl.BlockSpec((tq, d), lambda b, i: (b * nq + i, 0)),
                  pl.BlockSpec((None, nm, d), lambda b, i: (b, 0, 0)),
                  pl.BlockSpec((None, nm, d), lambda b, i: (b, 0, 0))],
        out_specs=pl.BlockSpec((tq, d), lambda b, i: (b * nq + i, 0)),
        out_shape=jax.ShapeDtypeStruct((m, d), BF16),
        compiler_params=_params("arbitrary", "arbitrary"),
        name="attn_prompt",
    )(q, mk, mv)


def _attn_sample_kernel(q_ref, k_ref, v_ref, o_ref, *, scale, heads, hd):
    for h in range(heads):
        cols = slice(h * hd, (h + 1) * hd)
        s = lax.dot_general(q_ref[:, cols], k_ref[:, h, :], (((1,), (1,)), ((), ())),
                            preferred_element_type=F32) * scale
        s = s - jnp.max(s, axis=-1, keepdims=True)
        e = jnp.exp(s)
        pr = e / jnp.sum(e, axis=-1, keepdims=True)
        o_ref[:, cols] = jnp.dot(pr, v_ref[:, h, :], preferred_element_type=F32)


def _attn_sample(q, mk, mv, layer):
    b, t, d = q.shape
    _, _, nm, heads, hd = mk.shape
    kv_spec = pl.BlockSpec((None, None, nm, heads, hd), lambda i: (layer, i, 0, 0, 0))
    return pl.pallas_call(
        functools.partial(_attn_sample_kernel, scale=hd ** -0.5, heads=heads, hd=hd),
        grid=(b,),
        in_specs=[pl.BlockSpec((None, t, d), lambda i: (i, 0, 0)), kv_spec, kv_spec],
        out_specs=pl.BlockSpec((None, t, d), lambda i: (i, 0, 0)),
        out_shape=jax.ShapeDtypeStruct((b, t, d), F32),
        compiler_params=_params("arbitrary"),
        name="attn_sample",
    )(q, mk, mv)


def _ffn_up_kernel(h_ref, hp_ref, wg_ref, wv_ref, cw_ref, cb_ref, st_ref, act_ref, zt_ref, zext_ref, rstd_ref,
                   *, halo, tm, shift, recompute_halo):
    @pl.when(pl.program_id(2) == 0)
    def _():
        rstd_ref[...] = _row_rstd(h_ref[...])

    h = h_ref[...]
    rstd = rstd_ref[...]
    z = jnp.dot(h, wg_ref[...], preferred_element_type=F32) * rstd
    val = jnp.dot(h, wv_ref[...], preferred_element_type=F32) * rstd
    zprev = st_ref[...]
    if recompute_halo:
        hp = hp_ref[...]
        zhalo = jnp.dot(hp, wg_ref[...], preferred_element_type=F32) * _row_rstd(hp)
        zprev = jnp.where(pl.program_id(1) == 0, zprev, zhalo)
    zext_ref[0:halo, :] = zprev
    zext_ref[halo:halo + tm, :] = z
    zm1 = zext_ref[halo - shift:halo - shift + tm, :]
    zm2 = zext_ref[halo - 2 * shift:halo - 2 * shift + tm, :]
    zc = cb_ref[...] + cw_ref[0:1, :] * zm2 + cw_ref[1:2, :] * zm1 + cw_ref[2:3, :] * z
    act_ref[...] = (zc * _sigmoid(zc) * val).astype(act_ref.dtype)
    zt_ref[...] = zext_ref[tm:tm + halo, :]


def _ffn_up(h, wg, wv, cw, cb, state, *, nseq, shift, halo, tm, tn):
    m, d = h.shape
    _, fp = wg.shape
    rows = m // nseq
    tm = _pick(rows, tm)
    nt = rows // tm
    tn = min(tn, fp)
    assert tm >= halo and (nt == 1 or halo == SUBLANE)
    hb = tm // SUBLANE
    return pl.pallas_call(
        functools.partial(_ffn_up_kernel, halo=halo, tm=tm, shift=shift, recompute_halo=nt > 1),
        grid=(nseq, nt, pl.cdiv(fp, tn)),
        in_specs=[pl.BlockSpec((tm, d), lambda s, i, j: (s * nt + i, 0)),
                  pl.BlockSpec((SUBLANE, d), lambda s, i, j: (jnp.maximum((s * nt + i) * hb - 1, 0), 0)),
                  pl.BlockSpec((d, tn), lambda s, i, j: (0, j)),
                  pl.BlockSpec((d, tn), lambda s, i, j: (0, j)),
                  pl.BlockSpec((CONV_W, tn), lambda s, i, j: (0, j)),
                  pl.BlockSpec((1, tn), lambda s, i, j: (0, j)),
                  pl.BlockSpec((None, halo, tn), lambda s, i, j: (s, 0, j))],
        out_specs=[pl.BlockSpec((tm, tn), lambda s, i, j: (s * nt + i, j)),
                   pl.BlockSpec((halo, tn), lambda s, i, j: (s * nt + i, j))],
        out_shape=[jax.ShapeDtypeStruct((m, fp), BF16),
                   jax.ShapeDtypeStruct((nseq * nt * halo, fp), F32)],
        scratch_shapes=[pltpu.VMEM((halo + tm, tn), F32), pltpu.VMEM((tm, 1), F32)],
        compiler_params=_params("arbitrary", "arbitrary", "arbitrary"),
        name="ffn_up",
    )(h, h, wg, wv, cw, cb, state)


def kernel(x_prompt, x_sample, mem_prompt, cache_mem_k, cache_mem_v, state_shift, state_wkv, state_pool,
           state_conv, g_mix, w_in, mu_shift, w0, b_w, a0, b_a, b_g, k_k, k_a, r_k, gn_w, gn_b, w_pool,
           pool_scale, w_out, g_mem, w_mk, w_mv, g_attn, w_mq, w_mo, g_ffn, w_gate, w_val, conv_w, conv_b,
           w_down, g_final):
    bp, tp, d = x_prompt.shape
    bs, ts, _ = x_sample.shape
    depth = w_in.shape[0]
    c = w0.shape[1]
    nh = c // HEAD_SIZE
    nvb = HEAD_SIZE // SUBLANE
    cp = pool_scale.shape[1]
    pw = mu_shift.shape[1]
    lora = pw - 3 * c
    lp = _round_up(lora, LANE)
    ca = 3 * c + lp
    dl, al, gl = b_w.shape[1], b_a.shape[1], b_g.shape[1]
    f = w_gate.shape[2]
    fp = f
    nmem = mem_prompt.shape[1]
    nbuf = state_pool.shape[2]
    mp, ms = bp * tp, bs * ts

    xp = x_prompt.reshape(mp, d)
    xs = x_sample.transpose(1, 0, 2).reshape(ms, d)

    outs = {n: [] for n in ("pm_k", "pm_v", "p_sh", "p_wkv", "p_pool", "p_conv",
                            "s_sh", "s_wkv", "s_pool", "s_conv")}

    for l in range(depth):
        wa = jnp.pad(w_in[l][:, :pw], ((0, 0), (0, ca - pw))).astype(BF16)
        wu = w_in[l][:, pw:].astype(BF16)
        mu = jnp.pad(mu_shift[l], (0, ca - pw)).reshape(1, ca)
        bw = jnp.zeros((lp, c), F32).at[0:dl].set(b_w[l]).astype(BF16)
        ba = jnp.zeros((lp, c), F32).at[dl:dl + al].set(b_a[l]).astype(BF16)
        bg = jnp.zeros((lp, c), F32).at[dl + al:dl + al + gl].set(b_g[l]).astype(BF16)
        w0l = w0[l].reshape(1, c)
        a0l = a0[l].reshape(1, c)
        wpool = w_pool[l].astype(BF16)
        pscale = pool_scale[l].reshape(1, cp)
        wmk = w_mk[l].astype(BF16)
        wmv = w_mv[l].astype(BF16)
        late = {}
        cw = conv_w[l]
        cb = conv_b[l].reshape(1, fp)

        def head_tile(p):
            return p.reshape(nh, HEAD_SIZE).T

        chain_p = [jnp.tile(head_tile(p), (1, bp)).reshape(1, HEAD_SIZE, bp * nh)
                   for p in (k_k[l], k_a[l], r_k[l].reshape(-1), gn_w[l], gn_b[l])]
        chain_s = [jnp.broadcast_to(p.reshape(nh, HEAD_SIZE, 1), (nh, HEAD_SIZE, bs))
                   for p in (k_k[l], k_a[l], r_k[l].reshape(-1), gn_w[l], gn_b[l])]

        mn = _rmsnorm(mem_prompt.reshape(bp * nmem, d), g_mem[l], BF16)
        mk = _matmul(mn, wmk, out_dtype=F32, name="mem_k")
        mv = _matmul(mn, wmv, out_dtype=F32, name="mem_v")
        outs["pm_k"].append(mk.reshape(bp, nmem, MEM_HEADS, d // MEM_HEADS))
        outs["pm_v"].append(mv.reshape(bp, nmem, MEM_HEADS, d // MEM_HEADS))

        def layer(x, *, prompt):
            m = x.shape[0]
            if prompt:
                nseq, shift, halo1, halo, pos0 = bp, 1, SUBLANE, SUBLANE, 0
                st_shift = jnp.zeros((bp, halo1, ca), F32)
                hist = jnp.zeros((bp, POOL_HIST, cp), F32)
                st_conv = jnp.zeros((bp, halo, fp), F32)
            else:
                nseq, shift, halo1, halo, pos0 = 1, bs, bs, (CONV_W - 1) * bs, PAST_LEN
                st_shift = jnp.pad(state_shift[l], ((0, 0), (0, ca - pw))).reshape(1, halo1, ca)
                hist = jnp.pad(state_pool[l].transpose(1, 0, 2), ((POOL_HIST - nbuf, 0), (0, 0), (0, 0)))
                hist = hist.reshape(1, POOL_HIST * bs, cp)
                st_conv = state_conv[l].transpose(1, 0, 2).reshape(1, halo, fp)

            xn = _rmsnorm(x, g_mix[l], BF16)
            proj = _matmul(xn, wa, out_dtype=F32, name="in_proj_rwkv")
            u = _matmul(xn, wu, out_dtype=F32, name="in_proj_pool")
            tile = min(LANE, tp) if prompt else bs
            *rkvda, g = _rwkv_prep(proj, st_shift, mu, w0l, a0l, bw, ba, bg,
                                   nseq=nseq, shift=shift, halo=halo1, tt=tile, c=c)
            if prompt:
                chains = [_to_chain(x, nb=bp, nh=nh).reshape(1, tp, HEAD_SIZE, bp * nh) for x in rkvda]
                s0 = jnp.zeros((1, nvb, HEAD_SIZE, SUBLANE, bp * nh), F32)
                y, s_fin, cast_out = _wkv(*chains, s0, *chain_p, tt=16, group_major=True,
                                          cast=((w_out[l], None), (w_mq[l], g_attn[l]), (w_mo[l], None),
                                                (w_gate[l], g_ffn[l]), (w_val[l], g_ffn[l]),
                                                (w_down[l], None)))
                late.update(zip(("wo", "wmq", "wmo", "wg", "wv", "wd"), cast_out))
                r_out = _from_chain_gate(y.reshape(tp, HEAD_SIZE, bp * nh), g, nb=bp, nh=nh, tt=tile)
                new_wkv = s_fin.reshape(nvb, HEAD_SIZE, SUBLANE, bp, nh).transpose(3, 4, 0, 2, 1)
                new_wkv = new_wkv.reshape(bp, nh, HEAD_SIZE, HEAD_SIZE)
            else:
                chains = [x.reshape(ts, nh, HEAD_SIZE, bs) for x in rkvda]
                s0 = state_wkv[l].reshape(bs, nh, nvb, SUBLANE, HEAD_SIZE).transpose(1, 2, 4, 3, 0)
                y, s_fin, _ = _wkv(*chains, s0, *chain_s, tt=ts, group_major=False)
                r_out = _transpose_gate(y.reshape(ts, c, bs), g)
                new_wkv = s_fin.transpose(4, 0, 1, 3, 2).reshape(bs, nh, HEAD_SIZE, HEAD_SIZE)
            p_out = _pool(u, hist, wpool, pscale, nseq=nseq, shift=shift,
                          tt=256 if prompt else m, pos0=pos0)
            wmq, wmo, wg, wv, wd = (late[n] for n in ("wmq", "wmo", "wg", "wv", "wd"))
            x1, x1b = _matmul2_res(r_out, p_out, late["wo"], x)

            if prompt:
                q = _matmul(x1b, wmq, out_dtype=BF16, row_norm=True, name="attn_q")
                o = _attn_prompt(q, mk.reshape(bp, nmem, d), mv.reshape(bp, nmem, d),
                                 batch=bp, heads=MEM_HEADS)
            else:
                q = _matmul(x1b, wmq, out_dtype=F32, row_norm=True, name="attn_q")
                q = q.reshape(ts, bs, d).transpose(1, 0, 2)
                o = _attn_sample(q, cache_mem_k, cache_mem_v, l)
                o = o.transpose(1, 0, 2).reshape(m, d)
            x2, x2b = _matmul(o, wmo, out_dtype=F32, res=x1, emit_bf16=True, name="attn_o")

            act, ztail = _ffn_up(x2b, wg, wv, cw, cb, st_conv, nseq=nseq, shift=shift, halo=halo,
                                 tm=1024, tn=512)
            x3 = _matmul(act, wd, out_dtype=F32, res=x2, tm=512, tn=512, name="ffn_down")

            if prompt:
                new_shift = proj.reshape(bp, tp, ca)[:, -1, :pw]
                new_pool = u.reshape(bp, tp, cp)[:, tp - nbuf:]
                new_conv = ztail.reshape(bp, -1, halo, fp)[:, -1, halo - (CONV_W - 1):, :f]
            else:
                new_shift = proj[(ts - 1) * bs:, :pw]
                u_b = u.reshape(ts, bs, cp).transpose(1, 0, 2)
                new_pool = jnp.concatenate([state_pool[l], u_b], axis=1)[:, -nbuf:]
                new_conv = ztail.reshape(CONV_W - 1, bs, fp)[:, :, :f].transpose(1, 0, 2)
            return x3, new_shift, new_wkv, new_pool, new_conv

        xp, sh, wk, po, co = layer(xp, prompt=True)
        outs["p_sh"].append(sh)
        outs["p_wkv"].append(wk)
        outs["p_pool"].append(po)
        outs["p_conv"].append(co)
        xs, sh, wk, po, co = layer(xs, prompt=False)
        outs["s_sh"].append(sh)
        outs["s_wkv"].append(wk)
        outs["s_pool"].append(po)
        outs["s_conv"].append(co)

    y_prompt = _rmsnorm(xp, g_final, F32).reshape(bp, tp, d)
    y_sample = _rmsnorm(xs, g_final, F32).reshape(ts, bs, d).transpose(1, 0, 2)
    stk = lambda n: jnp.stack(outs[n])
    return (y_prompt, y_sample, stk("pm_k"), stk("pm_v"), stk("p_sh"), stk("p_wkv"), stk("p_pool"),
            stk("p_conv"), stk("s_sh"), stk("s_wkv"), stk("s_pool"), stk("s_conv"))
```

```python
import functools

import jax
import jax.numpy as jnp
from jax import lax
from jax.experimental import pallas as pl
from jax.experimental.pallas import tpu as pltpu

F32 = jnp.float32
BF16 = jnp.bfloat16

HEAD_SIZE = 64
POOL_WINDOWS = (2, 4, 8, 16)
POOL_HIST = 16
MEM_HEADS = 4
CONV_W = 3
PAST_LEN = 16384
NORM_EPS = 1e-6
GN_EPS = 64e-5
EXP_M_HALF = 0.6065306597126334
LANE = 128
SUBLANE = 8
VMEM_LIMIT = 56 * 1024 * 1024


def _pick(dim, pref, mult=SUBLANE):
    if dim <= pref:
        return dim
    t = (pref // mult) * mult
    while t >= mult:
        if dim % t == 0:
            return t
        t -= mult
    return dim


def _round_up(x, m):
    return (x + m - 1) // m * m


def _params(*sem):
    return pltpu.CompilerParams(dimension_semantics=sem, vmem_limit_bytes=VMEM_LIMIT)


def _sigmoid(x):
    return 1.0 / (1.0 + jnp.exp(-x))


def _rmsnorm_kernel(x_ref, g_ref, o_ref):
    x = x_ref[...].astype(F32)
    ms = jnp.mean(x * x, axis=-1, keepdims=True)
    o_ref[...] = (x * lax.rsqrt(ms + NORM_EPS) * g_ref[...]).astype(o_ref.dtype)


def _rmsnorm(x, g, out_dtype):
    m, d = x.shape
    tr = _pick(m, 256)
    return pl.pallas_call(
        _rmsnorm_kernel,
        grid=(m // tr,),
        in_specs=[pl.BlockSpec((tr, d), lambda i: (i, 0)),
                  pl.BlockSpec((1, d), lambda i: (0, 0))],
        out_specs=pl.BlockSpec((tr, d), lambda i: (i, 0)),
        out_shape=jax.ShapeDtypeStruct((m, d), out_dtype),
        compiler_params=_params("arbitrary"),
        name="rmsnorm",
    )(x, g.reshape(1, d).astype(F32))


def _row_rstd(x):
    xf = x.astype(F32)
    return lax.rsqrt(jnp.mean(xf * xf, axis=-1, keepdims=True) + NORM_EPS)


def _mm_kernel(*refs, has_res, emit_bf16, row_norm):
    refs = list(refs)
    x_ref, w_ref = refs[:2]
    del refs[:2]
    res_ref = refs.pop(0) if has_res else None
    o_ref = refs.pop(0)
    ob_ref = refs.pop(0) if emit_bf16 else None
    rstd_ref = refs.pop(0) if row_norm else None
    if row_norm:
        @pl.when(pl.program_id(1) == 0)
        def _():
            rstd_ref[...] = _row_rstd(x_ref[...])
    acc = jnp.dot(x_ref[...].astype(BF16), w_ref[...], preferred_element_type=F32)
    if row_norm:
        acc = acc * rstd_ref[...]
    if has_res:
        acc = acc + res_ref[...]
    o_ref[...] = acc.astype(o_ref.dtype)
    if emit_bf16:
        ob_ref[...] = acc.astype(BF16)


def _matmul(x, w, *, out_dtype, res=None, emit_bf16=False, row_norm=False, tm=1024, tn=512, name="matmul"):
    m, kd = x.shape
    _, n = w.shape
    tm = _pick(m, tm)
    tn = _pick(n, tn, LANE)
    tile = pl.BlockSpec((tm, tn), lambda i, j: (i, j))
    in_specs = [pl.BlockSpec((tm, kd), lambda i, j: (i, 0)),
                pl.BlockSpec((kd, tn), lambda i, j: (0, j))]
    args = [x, w]
    if res is not None:
        in_specs.append(tile)
        args.append(res)
    out_specs, out_shape = [tile], [jax.ShapeDtypeStruct((m, n), out_dtype)]
    if emit_bf16:
        out_specs.append(tile)
        out_shape.append(jax.ShapeDtypeStruct((m, n), BF16))
    out = pl.pallas_call(
        functools.partial(_mm_kernel, has_res=res is not None, emit_bf16=emit_bf16, row_norm=row_norm),
        grid=(m // tm, n // tn),
        in_specs=in_specs,
        out_specs=out_specs,
        out_shape=out_shape,
        scratch_shapes=[pltpu.VMEM((tm, 1), F32)] if row_norm else [],
        compiler_params=_params("arbitrary", "arbitrary"),
        name=name,
    )(*args)
    return out if emit_bf16 else out[0]


def _mm2_kernel(x1_ref, x2_ref, w1_ref, w2_ref, res_ref, o_ref, ob_ref):
    acc = jnp.dot(x1_ref[...], w1_ref[...], preferred_element_type=F32)
    acc = acc + jnp.dot(x2_ref[...], w2_ref[...], preferred_element_type=F32)
    acc = acc + res_ref[...]
    o_ref[...] = acc
    ob_ref[...] = acc.astype(BF16)


def _matmul2_res(x1, x2, w, res, *, tm=1024, tn=512):
    m, k1 = x1.shape
    _, k2 = x2.shape
    kw, n = w.shape
    assert k1 == k2 and kw == k1 + k2
    tm = _pick(m, tm)
    tn = _pick(n, tn, LANE)
    return pl.pallas_call(
        _mm2_kernel,
        grid=(m // tm, n // tn),
        in_specs=[pl.BlockSpec((tm, k1), lambda i, j: (i, 0)),
                  pl.BlockSpec((tm, k2), lambda i, j: (i, 0)),
                  pl.BlockSpec((k1, tn), lambda i, j: (0, j)),
                  pl.BlockSpec((k2, tn), lambda i, j: (1, j)),
                  pl.BlockSpec((tm, tn), lambda i, j: (i, j))],
        out_specs=[pl.BlockSpec((tm, tn), lambda i, j: (i, j))] * 2,
        out_shape=[jax.ShapeDtypeStruct((m, n), F32), jax.ShapeDtypeStruct((m, n), BF16)],
        compiler_params=_params("arbitrary", "arbitrary"),
        name="out_proj",
    )(x1, x2, w, w, res)


def _prep_kernel(p_ref, st_ref, mu_ref, w0_ref, a0_ref, bw_ref, ba_ref, bg_ref,
                 r_ref, k_ref, v_ref, d_ref, a_ref, g_ref, ext_ref, *, halo, tt, shift, c):
    @pl.when(pl.program_id(1) == 0)
    def _():
        ext_ref[0:halo, :] = st_ref[...]

    p = p_ref[...]
    ext_ref[halo:halo + tt, :] = p
    prev = ext_ref[halo - shift:halo - shift + tt, :]
    xs = p + (prev - p) * mu_ref[...]
    ext_ref[0:halo, :] = ext_ref[tt:tt + halo, :]

    r_ref[...] = xs[:, 0:c].T
    k_ref[...] = xs[:, c:2 * c].T
    v_ref[...] = xs[:, 2 * c:3 * c].T
    lo = xs[:, 3 * c:]
    wpre = jnp.dot(jnp.tanh(lo).astype(BF16), bw_ref[...], preferred_element_type=F32)
    apre = jnp.dot(lo.astype(BF16), ba_ref[...], preferred_element_type=F32)
    g_ref[...] = jnp.dot(_sigmoid(lo).astype(BF16), bg_ref[...], preferred_element_type=F32)
    ez = jnp.exp(-(w0_ref[...] + wpre))
    d_ref[...] = jnp.exp(-EXP_M_HALF / (1.0 + ez)).T
    a_ref[...] = _sigmoid(a0_ref[...] + apre).T


def _rwkv_prep(proj, state, mu, w0, a0, bw, ba, bg, *, nseq, shift, halo, tt, c, ca):
    m = proj.shape[0]
    rows = m // nseq
    assert rows % tt == 0
    nt = rows // tt
    row_spec = lambda width: pl.BlockSpec((tt, width), lambda s, j: (s * nt + j, 0))
    tr_spec = pl.BlockSpec((None, c, tt), lambda s, j: (s * nt + j, 0, 0))
    full = lambda arr: pl.BlockSpec(arr.shape, lambda s, j: (0,) * arr.ndim)
    out_t = jax.ShapeDtypeStruct((nseq * nt, c, tt), F32)
    return pl.pallas_call(
        functools.partial(_prep_kernel, halo=halo, tt=tt, shift=shift, c=c),
        grid=(nseq, nt),
        in_specs=[row_spec(ca),
                  pl.BlockSpec((None, halo, ca), lambda s, j: (s, 0, 0)),
                  full(mu), full(w0), full(a0), full(bw), full(ba), full(bg)],
        out_specs=[tr_spec] * 5 + [row_spec(c)],
        out_shape=[out_t] * 5 + [jax.ShapeDtypeStruct((m, c), F32)],
        scratch_shapes=[pltpu.VMEM((halo + tt, ca), F32)],
        compiler_params=_params("arbitrary", "arbitrary"),
        name="rwkv_prep",
    )(proj, state, mu, w0, a0, bw, ba, bg)


V_GROUP = 4
K_BLOCK = 32


def _wkv_kernel(*refs, tt, n, cast_steps, cast_gain):
    refs = list(refs)
    (r_ref, k_ref, v_ref, d_ref, a_ref, s0_ref, kk_ref, ka_ref, rk_ref, gw_ref, gb_ref) = refs[:11]
    del refs[:11]
    nc = len(cast_steps)
    w_refs = [refs.pop(0) for _ in range(nc)]
    wg_refs = [refs.pop(0) if has_gain else None for has_gain in cast_gain]
    o_ref, sf_ref = refs.pop(0), refs.pop(0)
    wb_refs = [refs.pop(0) for _ in range(nc)]
    s_ref, y_ref, row_ref, dot_ref, plast_ref = refs

    @pl.when(pl.program_id(1) == 0)
    def _():
        s_ref[...] = s0_ref[...]

    for w_ref, wg_ref, wb_ref, steps in zip(w_refs, wg_refs, wb_refs, cast_steps):
        @pl.when(pl.program_id(1) < steps)
        def _(w_ref=w_ref, wg_ref=wg_ref, wb_ref=wb_ref):
            w = w_ref[...] if wg_ref is None else w_ref[...] * wg_ref[...]
            wb_ref[...] = w.astype(wb_ref.dtype)

    k_k = kk_ref[...]
    k_a = ka_ref[...]
    r_k = rk_ref[...]
    gn_w = gw_ref[...]
    gn_b = gb_ref[...]
    inv_n = 1.0 / n
    lanes = s_ref.shape[-1]

    r = r_ref[...]
    k = k_ref[...]
    w = d_ref[...]
    a = a_ref[...]
    kk = k * k_k
    norm = jnp.sqrt(jnp.sum(kk * kk, axis=1, keepdims=True))
    kk = kk / jnp.maximum(norm, 1e-12)
    k2 = k * (1.0 + (a - 1.0) * k_a)
    bv = kk * a
    p = None
    for t in range(tt):
        row_ref[0, t] = -kk[t] if p is None else -kk[t] * p
        p = w[t] if p is None else p * w[t]
        inv_p = 1.0 / p
        row_ref[1, t] = r[t] * p
        row_ref[2, t] = bv[t] * inv_p
        row_ref[3, t] = k2[t] * inv_p
    plast_ref[...] = p
    dot_ref[0] = jnp.sum(bv * r, axis=1, keepdims=True)
    dot_ref[1] = jnp.sum(k2 * r, axis=1, keepdims=True)
    dot_ref[2] = jnp.sum(r * k2 * r_k, axis=1, keepdims=True)

    def token(t, carry):
        b_r = dot_ref[0, t]
        k_r = dot_ref[1, t]

        def vgroup(gidx, c2):
            vb0 = gidx * V_GROUP
            zero = jnp.zeros((SUBLANE, lanes), F32)

            def reduce_keys(kb, acc):
                sa_acc, q_acc = list(acc[0]), list(acc[1])
                for j in range(K_BLOCK):
                    kx = kb * K_BLOCK + j
                    a_row = row_ref[0, t, pl.ds(kx, 1), :]
                    q_row = row_ref[1, t, pl.ds(kx, 1), :]
                    for i in range(V_GROUP):
                        s_k = s_ref[vb0 + i, kx]
                        sa_acc[i] = sa_acc[i] + s_k * a_row
                        q_acc[i] = q_acc[i] + s_k * q_row
                return tuple(sa_acc), tuple(q_acc)

            sa8, q8 = lax.fori_loop(0, n // K_BLOCK, reduce_keys, ((zero,) * V_GROUP, (zero,) * V_GROUP))
            base = pl.multiple_of(vb0 * SUBLANE, SUBLANE)
            v8 = [v_ref[t, pl.ds(base + i * SUBLANE, SUBLANE), :] for i in range(V_GROUP)]

            def update_keys(kb, c3):
                for j in range(K_BLOCK):
                    kx = kb * K_BLOCK + j
                    b_row = row_ref[2, t, pl.ds(kx, 1), :]
                    k_row = row_ref[3, t, pl.ds(kx, 1), :]
                    for i in range(V_GROUP):
                        s_ref[vb0 + i, kx] = s_ref[vb0 + i, kx] + sa8[i] * b_row + v8[i] * k_row
                return c3

            lax.fori_loop(0, n // K_BLOCK, update_keys, 0)
            for i in range(V_GROUP):
                y_ref[t, pl.ds(base + i * SUBLANE, SUBLANE), :] = q8[i] + sa8[i] * b_r + v8[i] * k_r
            return c2

        lax.fori_loop(0, n // (SUBLANE * V_GROUP), vgroup, 0)
        return carry

    lax.fori_loop(0, tt, token, 0)

    def rescale(vb, carry):
        for kx in range(n):
            s_ref[vb, kx] = s_ref[vb, kx] * plast_ref[kx:kx + 1, :]
        return carry

    lax.fori_loop(0, n // SUBLANE, rescale, 0)

    y = y_ref[...]
    mean = jnp.sum(y, axis=1, keepdims=True) * inv_n
    yc = y - mean
    var = jnp.sum(yc * yc, axis=1, keepdims=True) * inv_n
    o_ref[...] = yc * lax.rsqrt(var + GN_EPS) * gn_w + gn_b + dot_ref[2] * v_ref[...]

    @pl.when(pl.program_id(1) == pl.num_programs(1) - 1)
    def _():
        sf_ref[...] = s_ref[...]


def _cast_rows(rows, nsteps):
    blk = _round_up(-(-rows // nsteps), 2 * SUBLANE)
    while rows % blk:
        blk += 2 * SUBLANE
    return blk, rows // blk


def _wkv(r, k, v, d, a, s0, k_k, k_a, r_k, gn_w, gn_b, *, tt, group_major, cast=()):
    if group_major:
        g, t, n, lanes = r.shape
    else:
        t, g, n, lanes = r.shape
    tt = _pick(t, tt, 1)
    nb = n // SUBLANE
    assert not cast or g == 1
    cast_specs, gain_specs, cast_steps = [], [], []
    for w, gain in cast:
        blk, steps = _cast_rows(w.shape[0], t // tt)
        cast_steps.append(steps)
        index = lambda i, j, last=steps - 1: (jnp.minimum(j, last), 0)
        cast_specs.append(pl.BlockSpec((blk, w.shape[1]), index))
        if gain is not None:
            gain_specs.append(pl.BlockSpec((blk, 1), index))
    weights = [w for w, _ in cast]
    gains = [gain.reshape(-1, 1).astype(F32) for _, gain in cast if gain is not None]
    cast_gain = tuple(gain is not None for _, gain in cast)
    if group_major:
        tok = pl.BlockSpec((None, tt, n, lanes), lambda i, j: (i, j, 0, 0))
    else:
        tok = pl.BlockSpec((tt, None, n, lanes), lambda i, j: (j, i, 0, 0))
    st = pl.BlockSpec((None, nb, n, SUBLANE, lanes), lambda i, j: (i, 0, 0, 0, 0))
    gp = k_k.shape[0]
    par = pl.BlockSpec((None, n, lanes), (lambda i, j: (i, 0, 0)) if gp > 1 else (lambda i, j: (0, 0, 0)))
    y, s_fin, *cast_out = pl.pallas_call(
        functools.partial(_wkv_kernel, tt=tt, n=n, cast_steps=tuple(cast_steps), cast_gain=cast_gain),
        grid=(g, t // tt),
        in_specs=[tok] * 5 + [st] + [par] * 5 + cast_specs + gain_specs,
        out_specs=[tok, st] + cast_specs,
        out_shape=[jax.ShapeDtypeStruct(r.shape, F32),
                   jax.ShapeDtypeStruct((g, nb, n, SUBLANE, lanes), F32)]
                  + [jax.ShapeDtypeStruct(w.shape, BF16) for w in weights],
        scratch_shapes=[pltpu.VMEM((nb, n, SUBLANE, lanes), F32), pltpu.VMEM((tt, n, lanes), F32),
                        pltpu.VMEM((4, tt, n, lanes), F32), pltpu.VMEM((3, tt, 1, lanes), F32),
                        pltpu.VMEM((n, lanes), F32)],
        compiler_params=_params("arbitrary", "arbitrary"),
        name="wkv_recurrence",
    )(r, k, v, d, a, s0, k_k, k_a, r_k, gn_w, gn_b, *weights, *gains)
    return y, s_fin, cast_out


def _to_chain_kernel(x_ref, o_ref, *, nb, nh, n, tt):
    for kb in range(n // SUBLANE):
        ks = slice(kb * SUBLANE, (kb + 1) * SUBLANE)
        per_key = jnp.concatenate([jnp.swapaxes(x_ref[b, :, ks, :], 0, 1) for b in range(nb)], axis=1)
        chains_last = jnp.swapaxes(per_key, 1, 2)
        o_ref[:, ks, :] = jnp.swapaxes(chains_last, 0, 1)


def _to_chain(xt, *, nb, nh):
    tiles, c, tt = xt.shape
    nt = tiles // nb
    n = c // nh
    return pl.pallas_call(
        functools.partial(_to_chain_kernel, nb=nb, nh=nh, n=n, tt=tt),
        grid=(nt,),
        in_specs=[pl.BlockSpec((nb, None, nh, n, tt), lambda j: (0, j, 0, 0, 0))],
        out_specs=pl.BlockSpec((tt, n, nb * nh), lambda j: (j, 0, 0)),
        out_shape=jax.ShapeDtypeStruct((nt * tt, n, nb * nh), F32),
        compiler_params=_params("arbitrary"),
        name="to_chain",
    )(xt.reshape(nb, nt, nh, n, tt))


def _from_chain_gate_kernel(y_ref, g_ref, o_ref, xt_ref, *, nb, nh, n, tt):
    for vb in range(n // SUBLANE):
        vs = slice(vb * SUBLANE, (vb + 1) * SUBLANE)
        chains_mid = jnp.swapaxes(jnp.swapaxes(y_ref[:, vs, :], 0, 1), 1, 2)
        for b in range(nb):
            xt_ref[b, :, vs, :] = jnp.swapaxes(chains_mid[:, b * nh:(b + 1) * nh, :], 0, 1)
    for b in range(nb):
        o_ref[b] = (xt_ref[b].reshape(nh * n, tt).T * g_ref[b]).astype(o_ref.dtype)


def _from_chain_gate(y, g, *, nb, nh, tt):
    m, c = g.shape
    t = m // nb
    n = c // nh
    nt = t // tt
    row = pl.BlockSpec((nb, tt, c), lambda j: (0, j, 0))
    out = pl.pallas_call(
        functools.partial(_from_chain_gate_kernel, nb=nb, nh=nh, n=n, tt=tt),
        grid=(nt,),
        in_specs=[pl.BlockSpec((tt, n, nb * nh), lambda j: (j, 0, 0)), row],
        out_specs=row,
        out_shape=jax.ShapeDtypeStruct((nb, t, c), BF16),
        scratch_shapes=[pltpu.VMEM((nb, nh, n, tt), F32)],
        compiler_params=_params("arbitrary"),
        name="from_chain_gate",
    )(y, g.reshape(nb, t, c))
    return out.reshape(m, c)


def _transpose_gate_kernel(y_ref, g_ref, o_ref):
    o_ref[...] = (y_ref[...].T * g_ref[...]).astype(o_ref.dtype)


def _transpose_gate(yt, g):
    tiles, c, tt = yt.shape
    return pl.pallas_call(
        _transpose_gate_kernel,
        grid=(tiles,),
        in_specs=[pl.BlockSpec((None, c, tt), lambda i: (i, 0, 0)),
                  pl.BlockSpec((tt, c), lambda i: (i, 0))],
        out_specs=pl.BlockSpec((tt, c), lambda i: (i, 0)),
        out_shape=jax.ShapeDtypeStruct((tiles * tt, c), BF16),
        compiler_params=_params("arbitrary"),
        name="transpose_gate",
    )(yt, g)


def _pool_kernel(u_ref, h_ref, w_ref, sc_ref, o_ref, ext_ref, *, halo, tt, shift, pos0):
    gi = pl.program_id(1)
    j = pl.program_id(2)

    @pl.when(j == 0)
    def _():
        ext_ref[0:halo, :] = h_ref[...]

    x = u_ref[...]
    ext_ref[halo:halo + tt, :] = x
    row = lax.broadcasted_iota(jnp.int32, (tt, 1), 0) + j * tt
    time = row if shift == 1 else row // shift

    for g, win in enumerate(POOL_WINDOWS):
        @pl.when(gi == g)
        def _(win=win):
            acc = x
            for back in range(1, win):
                lo = halo - back * shift
                acc = acc + ext_ref[lo:lo + tt, :]
            cnt = jnp.minimum(pos0 + time + 1, win).astype(F32)
            d = acc / cnt - x
            y = jnp.dot(d.astype(BF16), w_ref[...], preferred_element_type=F32)
            o_ref[...] = (y * sc_ref[...]).astype(o_ref.dtype)

    if tt >= halo:
        ext_ref[0:halo, :] = ext_ref[tt:tt + halo, :]


def _pool(proj, col0, hist, w_pool, scale, *, nseq, shift, tt, pos0):
    m = proj.shape[0]
    cp = scale.shape[1]
    ng = len(POOL_WINDOWS)
    gw = cp // ng
    assert col0 % gw == 0
    g0 = col0 // gw
    rows = m // nseq
    tt = _pick(rows, tt)
    nt = rows // tt
    halo = POOL_HIST * shift
    assert nt == 1 or tt >= halo
    return pl.pallas_call(
        functools.partial(_pool_kernel, halo=halo, tt=tt, shift=shift, pos0=pos0),
        grid=(nseq, ng, nt),
        in_specs=[pl.BlockSpec((tt, gw), lambda s, g, j: (s * nt + j, g0 + g)),
                  pl.BlockSpec((None, halo, gw), lambda s, g, j: (s, 0, g)),
                  pl.BlockSpec((None, gw, gw), lambda s, g, j: (g, 0, 0)),
                  pl.BlockSpec((1, gw), lambda s, g, j: (0, g))],
        out_specs=pl.BlockSpec((tt, gw), lambda s, g, j: (s * nt + j, g)),
        out_shape=jax.ShapeDtypeStruct((m, cp), BF16),
        scratch_shapes=[pltpu.VMEM((halo + tt, gw), F32)],
        compiler_params=_params("arbitrary", "arbitrary", "arbitrary"),
        name="pool_mix",
    )(proj, hist, w_pool, scale)


def _attend(q, k, v, scale):
    s = lax.dot_general(q, k, (((1,), (1,)), ((), ())), preferred_element_type=F32) * scale
    s = s - jnp.max(s, axis=-1, keepdims=True)
    e = jnp.exp(s)
    pr = e / jnp.sum(e, axis=-1, keepdims=True)
    return jnp.dot(pr.astype(BF16), v, preferred_element_type=F32)


def _attn_prompt_kernel(q_ref, k_ref, v_ref, o_ref, *, scale, heads, hd):
    for h in range(heads):
        cols = slice(h * hd, (h + 1) * hd)
        o = _attend(q_ref[:, cols], k_ref[:, cols].astype(BF16), v_ref[:, cols].astype(BF16), scale)
        o_ref[:, cols] = o.astype(o_ref.dtype)


def _attn_prompt(q, mk, mv, *, batch, heads):
    m, d = q.shape
    t = m // batch
    nm = mk.shape[1]
    hd = d // heads
    tq = _pick(t, 512)
    nq = t // tq
    return pl.pallas_call(
        functools.partial(_attn_prompt_kernel, scale=hd ** -0.5, heads=heads, hd=hd),
        grid=(batch, nq),
        in_specs=[pl.BlockSpec((tq, d), lambda b, i: (b * nq + i, 0)),
                  pl.BlockSpec((None, nm, d), lambda b, i: (b, 0, 0)),
                  pl.BlockSpec((None, nm, d), lambda b, i: (b, 0, 0))],
        out_specs=pl.BlockSpec((tq, d), lambda b, i: (b * nq + i, 0)),
        out_shape=jax.ShapeDtypeStruct((m, d), BF16),
        compiler_params=_params("arbitrary", "arbitrary"),
        name="attn_prompt",
    )(q, mk, mv)


def _attn_sample_kernel(q_ref, k_ref, v_ref, o_ref, *, scale, heads, hd):
    t = q_ref.shape[0]
    nm = k_ref.shape[0]
    q = jnp.concatenate([q_ref[:, h * hd:(h + 1) * hd] for h in range(heads)], axis=0)
    k = k_ref[...].reshape(nm * heads, hd)
    v = v_ref[...].reshape(nm * heads, hd)
    s = lax.dot_general(q, k, (((1,), (1,)), ((), ())), preferred_element_type=F32) * scale
    row_head = lax.broadcasted_iota(jnp.int32, s.shape, 0) // t
    col_head = lax.broadcasted_iota(jnp.int32, s.shape, 1) % heads
    s = jnp.where(row_head == col_head, s, -jnp.inf)
    s = s - jnp.max(s, axis=-1, keepdims=True)
    e = jnp.exp(s)
    pr = e / jnp.sum(e, axis=-1, keepdims=True)
    o = jnp.dot(pr, v, preferred_element_type=F32)
    for h in range(heads):
        o_ref[:, h * hd:(h + 1) * hd] = o[h * t:(h + 1) * t, :]


def _attn_sample(q, mk, mv, layer):
    b, t, d = q.shape
    _, _, nm, heads, hd = mk.shape
    kv_spec = pl.BlockSpec((None, None, nm, heads, hd), lambda i: (layer, i, 0, 0, 0))
    return pl.pallas_call(
        functools.partial(_attn_sample_kernel, scale=hd ** -0.5, heads=heads, hd=hd),
        grid=(b,),
        in_specs=[pl.BlockSpec((None, t, d), lambda i: (i, 0, 0)), kv_spec, kv_spec],
        out_specs=pl.BlockSpec((None, t, d), lambda i: (i, 0, 0)),
        out_shape=jax.ShapeDtypeStruct((b, t, d), F32),
        compiler_params=_params("arbitrary"),
        name="attn_sample",
    )(q, mk, mv)


def _ffn_up_kernel(h_ref, hp_ref, wg_ref, wv_ref, cw_ref, cb_ref, st_ref, act_ref, zt_ref, zext_ref, rstd_ref,
                   *, halo, tm, shift, recompute_halo):
    @pl.when(pl.program_id(2) == 0)
    def _():
        rstd_ref[...] = _row_rstd(h_ref[...])

    h = h_ref[...]
    rstd = rstd_ref[...]
    z = jnp.dot(h, wg_ref[...], preferred_element_type=F32) * rstd
    val = jnp.dot(h, wv_ref[...], preferred_element_type=F32) * rstd
    zprev = st_ref[...]
    if recompute_halo:
        hp = hp_ref[...]
        zhalo = jnp.dot(hp, wg_ref[...], preferred_element_type=F32) * _row_rstd(hp)
        zprev = jnp.where(pl.program_id(1) == 0, zprev, zhalo)
    zext_ref[0:halo, :] = zprev
    zext_ref[halo:halo + tm, :] = z
    zm1 = zext_ref[halo - shift:halo - shift + tm, :]
    zm2 = zext_ref[halo - 2 * shift:halo - 2 * shift + tm, :]
    zc = cb_ref[...] + cw_ref[0:1, :] * zm2 + cw_ref[1:2, :] * zm1 + cw_ref[2:3, :] * z
    act_ref[...] = (zc * _sigmoid(zc) * val).astype(act_ref.dtype)
    zt_ref[...] = zext_ref[tm:tm + halo, :]


def _ffn_up(h, wg, wv, cw, cb, state, *, nseq, shift, halo, tm, tn):
    m, d = h.shape
    _, fp = wg.shape
    rows = m // nseq
    tm = _pick(rows, tm)
    nt = rows // tm
    tn = min(tn, fp)
    assert tm >= halo and (nt == 1 or halo == SUBLANE)
    hb = tm // SUBLANE
    return pl.pallas_call(
        functools.partial(_ffn_up_kernel, halo=halo, tm=tm, shift=shift, recompute_halo=nt > 1),
        grid=(nseq, nt, pl.cdiv(fp, tn)),
        in_specs=[pl.BlockSpec((tm, d), lambda s, i, j: (s * nt + i, 0)),
                  pl.BlockSpec((SUBLANE, d), lambda s, i, j: (jnp.maximum((s * nt + i) * hb - 1, 0), 0)),
                  pl.BlockSpec((d, tn), lambda s, i, j: (0, j)),
                  pl.BlockSpec((d, tn), lambda s, i, j: (0, j)),
                  pl.BlockSpec((CONV_W, tn), lambda s, i, j: (0, j)),
                  pl.BlockSpec((1, tn), lambda s, i, j: (0, j)),
                  pl.BlockSpec((None, halo, tn), lambda s, i, j: (s, 0, j))],
        out_specs=[pl.BlockSpec((tm, tn), lambda s, i, j: (s * nt + i, j)),
                   pl.BlockSpec((halo, tn), lambda s, i, j: (s * nt + i, j))],
        out_shape=[jax.ShapeDtypeStruct((m, fp), BF16),
                   jax.ShapeDtypeStruct((nseq * nt * halo, fp), F32)],
        scratch_shapes=[pltpu.VMEM((halo + tm, tn), F32), pltpu.VMEM((tm, 1), F32)],
        compiler_params=_params("arbitrary", "arbitrary", "arbitrary"),
        name="ffn_up",
    )(h, h, wg, wv, cw, cb, state)


def kernel(x_prompt, x_sample, mem_prompt, cache_mem_k, cache_mem_v, state_shift, state_wkv, state_pool,
           state_conv, g_mix, w_in, mu_shift, w0, b_w, a0, b_a, b_g, k_k, k_a, r_k, gn_w, gn_b, w_pool,
           pool_scale, w_out, g_mem, w_mk, w_mv, g_attn, w_mq, w_mo, g_ffn, w_gate, w_val, conv_w, conv_b,
           w_down, g_final):
    bp, tp, d = x_prompt.shape
    bs, ts, _ = x_sample.shape
    depth = w_in.shape[0]
    c = w0.shape[1]
    nh = c // HEAD_SIZE
    nvb = HEAD_SIZE // SUBLANE
    cp = pool_scale.shape[1]
    pw = mu_shift.shape[1]
    lora = pw - 3 * c
    lp = _round_up(lora, LANE)
    ca = 3 * c + lp
    dl, al, gl = b_w.shape[1], b_a.shape[1], b_g.shape[1]
    f = w_gate.shape[2]
    fp = f
    nmem = mem_prompt.shape[1]
    nbuf = state_pool.shape[2]
    mp, ms = bp * tp, bs * ts

    xp = x_prompt.reshape(mp, d)
    xs = x_sample.transpose(1, 0, 2).reshape(ms, d)

    outs = {n: [] for n in ("pm_k", "pm_v", "p_sh", "p_wkv", "p_pool", "p_conv",
                            "s_sh", "s_wkv", "s_pool", "s_conv")}

    for l in range(depth):
        w_all = jnp.concatenate([w_in[l][:, :pw], jnp.zeros((d, ca - pw), F32), w_in[l][:, pw:]],
                                axis=1).astype(BF16)
        mu = jnp.pad(mu_shift[l], (0, ca - pw)).reshape(1, ca)
        bw = jnp.zeros((lp, c), F32).at[0:dl].set(b_w[l]).astype(BF16)
        ba = jnp.zeros((lp, c), F32).at[dl:dl + al].set(b_a[l]).astype(BF16)
        bg = jnp.zeros((lp, c), F32).at[dl + al:dl + al + gl].set(b_g[l]).astype(BF16)
        w0l = w0[l].reshape(1, c)
        a0l = a0[l].reshape(1, c)
        wpool = w_pool[l].astype(BF16)
        pscale = pool_scale[l].reshape(1, cp)
        wmk = w_mk[l].astype(BF16)
        wmv = w_mv[l].astype(BF16)
        late = {}
        cw = conv_w[l]
        cb = conv_b[l].reshape(1, fp)

        def head_tile(p):
            return p.reshape(nh, HEAD_SIZE).T

        chain_p = [jnp.tile(head_tile(p), (1, bp)).reshape(1, HEAD_SIZE, bp * nh)
                   for p in (k_k[l], k_a[l], r_k[l].reshape(-1), gn_w[l], gn_b[l])]
        chain_s = [jnp.broadcast_to(p.reshape(nh, HEAD_SIZE, 1), (nh, HEAD_SIZE, bs))
                   for p in (k_k[l], k_a[l], r_k[l].reshape(-1), gn_w[l], gn_b[l])]

        mn = _rmsnorm(mem_prompt.reshape(bp * nmem, d), g_mem[l], BF16)
        mk = _matmul(mn, wmk, out_dtype=F32, name="mem_k")
        mv = _matmul(mn, wmv, out_dtype=F32, name="mem_v")
        outs["pm_k"].append(mk.reshape(bp, nmem, MEM_HEADS, d // MEM_HEADS))
        outs["pm_v"].append(mv.reshape(bp, nmem, MEM_HEADS, d // MEM_HEADS))

        def layer(x, *, prompt):
            m = x.shape[0]
            if prompt:
                nseq, shift, halo1, halo, pos0 = bp, 1, SUBLANE, SUBLANE, 0
                st_shift = jnp.zeros((bp, halo1, ca), F32)
                hist = jnp.zeros((bp, POOL_HIST, cp), F32)
                st_conv = jnp.zeros((bp, halo, fp), F32)
            else:
                nseq, shift, halo1, halo, pos0 = 1, bs, bs, (CONV_W - 1) * bs, PAST_LEN
                st_shift = jnp.pad(state_shift[l], ((0, 0), (0, ca - pw))).reshape(1, halo1, ca)
                hist = jnp.pad(state_pool[l].transpose(1, 0, 2), ((POOL_HIST - nbuf, 0), (0, 0), (0, 0)))
                hist = hist.reshape(1, POOL_HIST * bs, cp)
                st_conv = state_conv[l].transpose(1, 0, 2).reshape(1, halo, fp)

            xn = _rmsnorm(x, g_mix[l], BF16)
            proj = _matmul(xn, w_all, out_dtype=F32, name="in_proj")
            tile = min(LANE, tp) if prompt else bs
            *rkvda, g = _rwkv_prep(proj, st_shift, mu, w0l, a0l, bw, ba, bg,
                                   nseq=nseq, shift=shift, halo=halo1, tt=tile, c=c, ca=ca)
            if prompt:
                chains = [_to_chain(x, nb=bp, nh=nh).reshape(1, tp, HEAD_SIZE, bp * nh) for x in rkvda]
                s0 = jnp.zeros((1, nvb, HEAD_SIZE, SUBLANE, bp * nh), F32)
                y, s_fin, cast_out = _wkv(*chains, s0, *chain_p, tt=16, group_major=True,
                                          cast=((w_out[l], None), (w_mq[l], g_attn[l]), (w_mo[l], None),
                                                (w_gate[l], g_ffn[l]), (w_val[l], g_ffn[l]),
                                                (w_down[l], None)))
                late.update(zip(("wo", "wmq", "wmo", "wg", "wv", "wd"), cast_out))
                r_out = _from_chain_gate(y.reshape(tp, HEAD_SIZE, bp * nh), g, nb=bp, nh=nh, tt=tile)
                new_wkv = s_fin.reshape(nvb, HEAD_SIZE, SUBLANE, bp, nh).transpose(3, 4, 0, 2, 1)
                new_wkv = new_wkv.reshape(bp, nh, HEAD_SIZE, HEAD_SIZE)
            else:
                chains = [x.reshape(ts, nh, HEAD_SIZE, bs) for x in rkvda]
                s0 = state_wkv[l].reshape(bs, nh, nvb, SUBLANE, HEAD_SIZE).transpose(1, 2, 4, 3, 0)
                y, s_fin, _ = _wkv(*chains, s0, *chain_s, tt=ts, group_major=False)
                r_out = _transpose_gate(y.reshape(ts, c, bs), g)
                new_wkv = s_fin.transpose(4, 0, 1, 3, 2).reshape(bs, nh, HEAD_SIZE, HEAD_SIZE)
            p_out = _pool(proj, ca, hist, wpool, pscale, nseq=nseq, shift=shift,
                          tt=256 if prompt else m, pos0=pos0)
            wmq, wmo, wg, wv, wd = (late[n] for n in ("wmq", "wmo", "wg", "wv", "wd"))
            x1, x1b = _matmul2_res(r_out, p_out, late["wo"], x)

            if prompt:
                q = _matmul(x1b, wmq, out_dtype=BF16, row_norm=True, name="attn_q")
                o = _attn_prompt(q, mk.reshape(bp, nmem, d), mv.reshape(bp, nmem, d),
                                 batch=bp, heads=MEM_HEADS)
            else:
                q = _matmul(x1b, wmq, out_dtype=F32, row_norm=True, name="attn_q")
                q = q.reshape(ts, bs, d).transpose(1, 0, 2)
                o = _attn_sample(q, cache_mem_k, cache_mem_v, l)
                o = o.transpose(1, 0, 2).reshape(m, d)
            x2, x2b = _matmul(o, wmo, out_dtype=F32, res=x1, emit_bf16=True, name="attn_o")

            act, ztail = _ffn_up(x2b, wg, wv, cw, cb, st_conv, nseq=nseq, shift=shift, halo=halo,
                                 tm=1024, tn=512)
            x3 = _matmul(act, wd, out_dtype=F32, res=x2, tm=512, tn=512, name="ffn_down")

            if prompt:
                new_shift = proj.reshape(bp, tp, ca + cp)[:, -1, :pw]
                new_pool = proj.reshape(bp, tp, ca + cp)[:, tp - nbuf:, ca:]
                new_conv = ztail.reshape(bp, -1, halo, fp)[:, -1, halo - (CONV_W - 1):, :f]
            else:
                new_shift = proj[(ts - 1) * bs:, :pw]
                u_b = proj[:, ca:].reshape(ts, bs, cp).transpose(1, 0, 2)
                new_pool = jnp.concatenate([state_pool[l], u_b], axis=1)[:, -nbuf:]
                new_conv = ztail.reshape(CONV_W - 1, bs, fp)[:, :, :f].transpose(1, 0, 2)
            return x3, new_shift, new_wkv, new_pool, new_conv

        xp, sh, wk, po, co = layer(xp, prompt=True)
        outs["p_sh"].append(sh)
        outs["p_wkv"].append(wk)
        outs["p_pool"].append(po)
        outs["p_conv"].append(co)
        xs, sh, wk, po, co = layer(xs, prompt=False)
        outs["s_sh"].append(sh)
        outs["s_wkv"].append(wk)
        outs["s_pool"].append(po)
        outs["s_conv"].append(co)

    y_prompt = _rmsnorm(xp, g_final, F32).reshape(bp, tp, d)
    y_sample = _rmsnorm(xs, g_final, F32).reshape(ts, bs, d).transpose(1, 0, 2)
    stk = lambda n: jnp.stack(outs[n])
    return (y_prompt, y_sample, stk("pm_k"), stk("pm_v"), stk("p_sh"), stk("p_wkv"), stk("p_pool"),
            stk("p_conv"), stk("s_sh"), stk("s_wkv"), stk("s_pool"), stk("s_conv"))
```

```python
import functools

import jax
import jax.numpy as jnp
from jax import lax
from jax.experimental import pallas as pl
from jax.experimental.pallas import tpu as pltpu

F32 = jnp.float32
BF16 = jnp.bfloat16

HEAD_SIZE = 64
POOL_WINDOWS = (2, 4, 8, 16)
POOL_HIST = 16
MEM_HEADS = 4
CONV_W = 3
PAST_LEN = 16384
NORM_EPS = 1e-6
GN_EPS = 64e-5
EXP_M_HALF = 0.6065306597126334
LANE = 128
SUBLANE = 8
VMEM_LIMIT = 56 * 1024 * 1024


def _pick(dim, pref, mult=SUBLANE):
    if dim <= pref:
        return dim
    t = (pref // mult) * mult
    while t >= mult:
        if dim % t == 0:
            return t
        t -= mult
    return dim


def _round_up(x, m):
    return (x + m - 1) // m * m


def _params(*sem):
    return pltpu.CompilerParams(dimension_semantics=sem, vmem_limit_bytes=VMEM_LIMIT)


def _sigmoid(x):
    return 1.0 / (1.0 + jnp.exp(-x))


def _rmsnorm_kernel(x_ref, g_ref, o_ref):
    x = x_ref[...].astype(F32)
    ms = jnp.mean(x * x, axis=-1, keepdims=True)
    o_ref[...] = (x * lax.rsqrt(ms + NORM_EPS) * g_ref[...]).astype(o_ref.dtype)


def _rmsnorm(x, g, out_dtype):
    m, d = x.shape
    tr = _pick(m, 256)
    return pl.pallas_call(
        _rmsnorm_kernel,
        grid=(m // tr,),
        in_specs=[pl.BlockSpec((tr, d), lambda i: (i, 0)),
                  pl.BlockSpec((1, d), lambda i: (0, 0))],
        out_specs=pl.BlockSpec((tr, d), lambda i: (i, 0)),
        out_shape=jax.ShapeDtypeStruct((m, d), out_dtype),
        compiler_params=_params("arbitrary"),
        name="rmsnorm",
    )(x, g.reshape(1, d).astype(F32))


def _row_rstd(x):
    xf = x.astype(F32)
    return lax.rsqrt(jnp.mean(xf * xf, axis=-1, keepdims=True) + NORM_EPS)


def _mm_kernel(*refs, has_res, emit_bf16, row_norm):
    refs = list(refs)
    x_ref, w_ref = refs[:2]
    del refs[:2]
    res_ref = refs.pop(0) if has_res else None
    o_ref = refs.pop(0)
    ob_ref = refs.pop(0) if emit_bf16 else None
    rstd_ref = refs.pop(0) if row_norm else None
    if row_norm:
        @pl.when(pl.program_id(1) == 0)
        def _():
            rstd_ref[...] = _row_rstd(x_ref[...])
    acc = jnp.dot(x_ref[...].astype(BF16), w_ref[...], preferred_element_type=F32)
    if row_norm:
        acc = acc * rstd_ref[...]
    if has_res:
        acc = acc + res_ref[...]
    o_ref[...] = acc.astype(o_ref.dtype)
    if emit_bf16:
        ob_ref[...] = acc.astype(BF16)


def _matmul(x, w, *, out_dtype, res=None, emit_bf16=False, row_norm=False, tm=1024, tn=512, name="matmul"):
    m, kd = x.shape
    _, n = w.shape
    tm = _pick(m, tm)
    tn = _pick(n, tn, LANE)
    tile = pl.BlockSpec((tm, tn), lambda i, j: (i, j))
    in_specs = [pl.BlockSpec((tm, kd), lambda i, j: (i, 0)),
                pl.BlockSpec((kd, tn), lambda i, j: (0, j))]
    args = [x, w]
    if res is not None:
        in_specs.append(tile)
        args.append(res)
    out_specs, out_shape = [tile], [jax.ShapeDtypeStruct((m, n), out_dtype)]
    if emit_bf16:
        out_specs.append(tile)
        out_shape.append(jax.ShapeDtypeStruct((m, n), BF16))
    out = pl.pallas_call(
        functools.partial(_mm_kernel, has_res=res is not None, emit_bf16=emit_bf16, row_norm=row_norm),
        grid=(m // tm, n // tn),
        in_specs=in_specs,
        out_specs=out_specs,
        out_shape=out_shape,
        scratch_shapes=[pltpu.VMEM((tm, 1), F32)] if row_norm else [],
        compiler_params=_params("arbitrary", "arbitrary"),
        name=name,
    )(*args)
    return out if emit_bf16 else out[0]


def _split_cast_kernel(w_ref, wa_ref, wb_ref, *, b0):
    na, nb = wa_ref.shape[1], wb_ref.shape[1]
    wa_ref[...] = w_ref[:, 0:na].astype(BF16)
    wb_ref[...] = w_ref[:, b0:b0 + nb].astype(BF16)


def _split_cast(w, na, b0, *, tr=256):
    rows, cols = w.shape
    nb = cols - b0
    tr = _pick(rows, tr, 2 * SUBLANE)
    return pl.pallas_call(
        functools.partial(_split_cast_kernel, b0=b0),
        grid=(rows // tr,),
        in_specs=[pl.BlockSpec((tr, cols), lambda i: (i, 0))],
        out_specs=[pl.BlockSpec((tr, na), lambda i: (i, 0)), pl.BlockSpec((tr, nb), lambda i: (i, 0))],
        out_shape=[jax.ShapeDtypeStruct((rows, na), BF16), jax.ShapeDtypeStruct((rows, nb), BF16)],
        compiler_params=_params("arbitrary"),
        name="split_cast",
    )(w)


def _mm_cols2_kernel(x_ref, wa_ref, wb_ref, o_ref, *, na):
    j = pl.program_id(1)

    @pl.when(j < na)
    def _():
        o_ref[...] = jnp.dot(x_ref[...], wa_ref[...], preferred_element_type=F32)

    @pl.when(j >= na)
    def _():
        o_ref[...] = jnp.dot(x_ref[...], wb_ref[...], preferred_element_type=F32)


def _matmul_cols2(x, wa, wb, *, tm=1024, tn=512, name="matmul"):
    m, kd = x.shape
    tm = _pick(m, tm)
    tn = _pick(wa.shape[1], tn, LANE)
    assert wa.shape[1] % tn == 0 and wb.shape[1] % tn == 0
    na, nb = wa.shape[1] // tn, wb.shape[1] // tn
    return pl.pallas_call(
        functools.partial(_mm_cols2_kernel, na=na),
        grid=(m // tm, na + nb),
        in_specs=[pl.BlockSpec((tm, kd), lambda i, j: (i, 0)),
                  pl.BlockSpec((kd, tn), lambda i, j: (0, jnp.minimum(j, na - 1))),
                  pl.BlockSpec((kd, tn), lambda i, j: (0, jnp.maximum(j - na, 0)))],
        out_specs=pl.BlockSpec((tm, tn), lambda i, j: (i, j)),
        out_shape=jax.ShapeDtypeStruct((m, (na + nb) * tn), F32),
        compiler_params=_params("arbitrary", "arbitrary"),
        name=name,
    )(x, wa, wb)


def _mm2_kernel(x1_ref, x2_ref, w1_ref, w2_ref, res_ref, o_ref, ob_ref):
    acc = jnp.dot(x1_ref[...], w1_ref[...], preferred_element_type=F32)
    acc = acc + jnp.dot(x2_ref[...], w2_ref[...], preferred_element_type=F32)
    acc = acc + res_ref[...]
    o_ref[...] = acc
    ob_ref[...] = acc.astype(BF16)


def _matmul2_res(x1, x2, w, res, *, tm=1024, tn=512):
    m, k1 = x1.shape
    _, k2 = x2.shape
    kw, n = w.shape
    assert k1 == k2 and kw == k1 + k2
    tm = _pick(m, tm)
    tn = _pick(n, tn, LANE)
    return pl.pallas_call(
        _mm2_kernel,
        grid=(m // tm, n // tn),
        in_specs=[pl.BlockSpec((tm, k1), lambda i, j: (i, 0)),
                  pl.BlockSpec((tm, k2), lambda i, j: (i, 0)),
                  pl.BlockSpec((k1, tn), lambda i, j: (0, j)),
                  pl.BlockSpec((k2, tn), lambda i, j: (1, j)),
                  pl.BlockSpec((tm, tn), lambda i, j: (i, j))],
        out_specs=[pl.BlockSpec((tm, tn), lambda i, j: (i, j))] * 2,
        out_shape=[jax.ShapeDtypeStruct((m, n), F32), jax.ShapeDtypeStruct((m, n), BF16)],
        compiler_params=_params("arbitrary", "arbitrary"),
        name="out_proj",
    )(x1, x2, w, w, res)


def _prep_kernel(p_ref, st_ref, mu_ref, w0_ref, a0_ref, bw_ref, ba_ref, bg_ref,
                 r_ref, k_ref, v_ref, d_ref, a_ref, g_ref, ext_ref, *, halo, tt, shift, c):
    @pl.when(pl.program_id(1) == 0)
    def _():
        ext_ref[0:halo, :] = st_ref[...]

    p = p_ref[...]
    ext_ref[halo:halo + tt, :] = p
    prev = ext_ref[halo - shift:halo - shift + tt, :]
    xs = p + (prev - p) * mu_ref[...]
    ext_ref[0:halo, :] = ext_ref[tt:tt + halo, :]

    r_ref[...] = xs[:, 0:c].T
    k_ref[...] = xs[:, c:2 * c].T
    v_ref[...] = xs[:, 2 * c:3 * c].T
    lo = xs[:, 3 * c:]
    wpre = jnp.dot(jnp.tanh(lo).astype(BF16), bw_ref[...], preferred_element_type=F32)
    apre = jnp.dot(lo.astype(BF16), ba_ref[...], preferred_element_type=F32)
    g_ref[...] = jnp.dot(_sigmoid(lo).astype(BF16), bg_ref[...], preferred_element_type=F32)
    ez = jnp.exp(-(w0_ref[...] + wpre))
    d_ref[...] = jnp.exp(-EXP_M_HALF / (1.0 + ez)).T
    a_ref[...] = _sigmoid(a0_ref[...] + apre).T


def _rwkv_prep(proj, state, mu, w0, a0, bw, ba, bg, *, nseq, shift, halo, tt, c, ca):
    m = proj.shape[0]
    rows = m // nseq
    assert rows % tt == 0
    nt = rows // tt
    row_spec = lambda width: pl.BlockSpec((tt, width), lambda s, j: (s * nt + j, 0))
    tr_spec = pl.BlockSpec((None, c, tt), lambda s, j: (s * nt + j, 0, 0))
    full = lambda arr: pl.BlockSpec(arr.shape, lambda s, j: (0,) * arr.ndim)
    out_t = jax.ShapeDtypeStruct((nseq * nt, c, tt), F32)
    return pl.pallas_call(
        functools.partial(_prep_kernel, halo=halo, tt=tt, shift=shift, c=c),
        grid=(nseq, nt),
        in_specs=[row_spec(ca),
                  pl.BlockSpec((None, halo, ca), lambda s, j: (s, 0, 0)),
                  full(mu), full(w0), full(a0), full(bw), full(ba), full(bg)],
        out_specs=[tr_spec] * 5 + [row_spec(c)],
        out_shape=[out_t] * 5 + [jax.ShapeDtypeStruct((m, c), F32)],
        scratch_shapes=[pltpu.VMEM((halo + tt, ca), F32)],
        compiler_params=_params("arbitrary", "arbitrary"),
        name="rwkv_prep",
    )(proj, state, mu, w0, a0, bw, ba, bg)


V_GROUP = 4
K_BLOCK = 32


def _wkv_kernel(*refs, tt, n, cast_steps, cast_gain):
    refs = list(refs)
    (r_ref, k_ref, v_ref, d_ref, a_ref, s0_ref, kk_ref, ka_ref, rk_ref, gw_ref, gb_ref) = refs[:11]
    del refs[:11]
    nc = len(cast_steps)
    w_refs = [refs.pop(0) for _ in range(nc)]
    wg_refs = [refs.pop(0) if has_gain else None for has_gain in cast_gain]
    o_ref, sf_ref = refs.pop(0), refs.pop(0)
    wb_refs = [refs.pop(0) for _ in range(nc)]
    s_ref, y_ref, row_ref, dot_ref, plast_ref = refs

    @pl.when(pl.program_id(1) == 0)
    def _():
        s_ref[...] = s0_ref[...]

    for w_ref, wg_ref, wb_ref, steps in zip(w_refs, wg_refs, wb_refs, cast_steps):
        @pl.when(pl.program_id(1) < steps)
        def _(w_ref=w_ref, wg_ref=wg_ref, wb_ref=wb_ref):
            w = w_ref[...] if wg_ref is None else w_ref[...] * wg_ref[...]
            wb_ref[...] = w.astype(wb_ref.dtype)

    k_k = kk_ref[...]
    k_a = ka_ref[...]
    r_k = rk_ref[...]
    gn_w = gw_ref[...]
    gn_b = gb_ref[...]
    inv_n = 1.0 / n
    lanes = s_ref.shape[-1]

    r = r_ref[...]
    k = k_ref[...]
    w = d_ref[...]
    a = a_ref[...]
    kk = k * k_k
    norm = jnp.sqrt(jnp.sum(kk * kk, axis=1, keepdims=True))
    kk = kk / jnp.maximum(norm, 1e-12)
    k2 = k * (1.0 + (a - 1.0) * k_a)
    bv = kk * a
    p = None
    for t in range(tt):
        row_ref[0, t] = -kk[t] if p is None else -kk[t] * p
        p = w[t] if p is None else p * w[t]
        inv_p = 1.0 / p
        row_ref[1, t] = r[t] * p
        row_ref[2, t] = bv[t] * inv_p
        row_ref[3, t] = k2[t] * inv_p
    plast_ref[...] = p
    dot_ref[0] = jnp.sum(bv * r, axis=1, keepdims=True)
    dot_ref[1] = jnp.sum(k2 * r, axis=1, keepdims=True)
    dot_ref[2] = jnp.sum(r * k2 * r_k, axis=1, keepdims=True)

    def token(t, carry):
        b_r = dot_ref[0, t]
        k_r = dot_ref[1, t]

        def vgroup(gidx, c2):
            vb0 = gidx * V_GROUP
            zero = jnp.zeros((SUBLANE, lanes), F32)

            def reduce_keys(kb, acc):
                sa_acc, q_acc = list(acc[0]), list(acc[1])
                for j in range(K_BLOCK):
                    kx = kb * K_BLOCK + j
                    a_row = row_ref[0, t, pl.ds(kx, 1), :]
                    q_row = row_ref[1, t, pl.ds(kx, 1), :]
                    for i in range(V_GROUP):
                        s_k = s_ref[vb0 + i, kx]
                        sa_acc[i] = sa_acc[i] + s_k * a_row
                        q_acc[i] = q_acc[i] + s_k * q_row
                return tuple(sa_acc), tuple(q_acc)

            sa8, q8 = lax.fori_loop(0, n // K_BLOCK, reduce_keys, ((zero,) * V_GROUP, (zero,) * V_GROUP))
            base = pl.multiple_of(vb0 * SUBLANE, SUBLANE)
            v8 = [v_ref[t, pl.ds(base + i * SUBLANE, SUBLANE), :] for i in range(V_GROUP)]

            def update_keys(kb, c3):
                for j in range(K_BLOCK):
                    kx = kb * K_BLOCK + j
                    b_row = row_ref[2, t, pl.ds(kx, 1), :]
                    k_row = row_ref[3, t, pl.ds(kx, 1), :]
                    for i in range(V_GROUP):
                        s_ref[vb0 + i, kx] = s_ref[vb0 + i, kx] + sa8[i] * b_row + v8[i] * k_row
                return c3

            lax.fori_loop(0, n // K_BLOCK, update_keys, 0)
            for i in range(V_GROUP):
                y_ref[t, pl.ds(base + i * SUBLANE, SUBLANE), :] = q8[i] + sa8[i] * b_r + v8[i] * k_r
            return c2

        lax.fori_loop(0, n // (SUBLANE * V_GROUP), vgroup, 0)
        return carry

    lax.fori_loop(0, tt, token, 0)

    def rescale(vb, carry):
        for kx in range(n):
            s_ref[vb, kx] = s_ref[vb, kx] * plast_ref[kx:kx + 1, :]
        return carry

    lax.fori_loop(0, n // SUBLANE, rescale, 0)

    y = y_ref[...]
    mean = jnp.sum(y, axis=1, keepdims=True) * inv_n
    yc = y - mean
    var = jnp.sum(yc * yc, axis=1, keepdims=True) * inv_n
    o_ref[...] = yc * lax.rsqrt(var + GN_EPS) * gn_w + gn_b + dot_ref[2] * v_ref[...]

    @pl.when(pl.program_id(1) == pl.num_programs(1) - 1)
    def _():
        sf_ref[...] = s_ref[...]


def _cast_rows(rows, nsteps):
    blk = _round_up(-(-rows // nsteps), 2 * SUBLANE)
    while rows % blk:
        blk += 2 * SUBLANE
    return blk, rows // blk


def _wkv(r, k, v, d, a, s0, k_k, k_a, r_k, gn_w, gn_b, *, tt, group_major, cast=()):
    if group_major:
        g, t, n, lanes = r.shape
    else:
        t, g, n, lanes = r.shape
    tt = _pick(t, tt, 1)
    nb = n // SUBLANE
    assert not cast or g == 1
    cast_specs, gain_specs, cast_steps = [], [], []
    for w, gain in cast:
        blk, steps = _cast_rows(w.shape[0], t // tt)
        cast_steps.append(steps)
        index = lambda i, j, last=steps - 1: (jnp.minimum(j, last), 0)
        cast_specs.append(pl.BlockSpec((blk, w.shape[1]), index))
        if gain is not None:
            gain_specs.append(pl.BlockSpec((blk, 1), index))
    weights = [w for w, _ in cast]
    gains = [gain.reshape(-1, 1).astype(F32) for _, gain in cast if gain is not None]
    cast_gain = tuple(gain is not None for _, gain in cast)
    if group_major:
        tok = pl.BlockSpec((None, tt, n, lanes), lambda i, j: (i, j, 0, 0))
    else:
        tok = pl.BlockSpec((tt, None, n, lanes), lambda i, j: (j, i, 0, 0))
    st = pl.BlockSpec((None, nb, n, SUBLANE, lanes), lambda i, j: (i, 0, 0, 0, 0))
    gp = k_k.shape[0]
    par = pl.BlockSpec((None, n, lanes), (lambda i, j: (i, 0, 0)) if gp > 1 else (lambda i, j: (0, 0, 0)))
    y, s_fin, *cast_out = pl.pallas_call(
        functools.partial(_wkv_kernel, tt=tt, n=n, cast_steps=tuple(cast_steps), cast_gain=cast_gain),
        grid=(g, t // tt),
        in_specs=[tok] * 5 + [st] + [par] * 5 + cast_specs + gain_specs,
        out_specs=[tok, st] + cast_specs,
        out_shape=[jax.ShapeDtypeStruct(r.shape, F32),
                   jax.ShapeDtypeStruct((g, nb, n, SUBLANE, lanes), F32)]
                  + [jax.ShapeDtypeStruct(w.shape, BF16) for w in weights],
        scratch_shapes=[pltpu.VMEM((nb, n, SUBLANE, lanes), F32), pltpu.VMEM((tt, n, lanes), F32),
                        pltpu.VMEM((4, tt, n, lanes), F32), pltpu.VMEM((3, tt, 1, lanes), F32),
                        pltpu.VMEM((n, lanes), F32)],
        compiler_params=_params("arbitrary", "arbitrary"),
        name="wkv_recurrence",
    )(r, k, v, d, a, s0, k_k, k_a, r_k, gn_w, gn_b, *weights, *gains)
    return y, s_fin, cast_out


def _to_chain_kernel(x_ref, o_ref, *, nb, nh, n, tt):
    for kb in range(n // SUBLANE):
        ks = slice(kb * SUBLANE, (kb + 1) * SUBLANE)
        per_key = jnp.concatenate([jnp.swapaxes(x_ref[b, :, ks, :], 0, 1) for b in range(nb)], axis=1)
        chains_last = jnp.swapaxes(per_key, 1, 2)
        o_ref[:, ks, :] = jnp.swapaxes(chains_last, 0, 1)


def _to_chain(xt, *, nb, nh):
    tiles, c, tt = xt.shape
    nt = tiles // nb
    n = c // nh
    return pl.pallas_call(
        functools.partial(_to_chain_kernel, nb=nb, nh=nh, n=n, tt=tt),
        grid=(nt,),
        in_specs=[pl.BlockSpec((nb, None, nh, n, tt), lambda j: (0, j, 0, 0, 0))],
        out_specs=pl.BlockSpec((tt, n, nb * nh), lambda j: (j, 0, 0)),
        out_shape=jax.ShapeDtypeStruct((nt * tt, n, nb * nh), F32),
        compiler_params=_params("arbitrary"),
        name="to_chain",
    )(xt.reshape(nb, nt, nh, n, tt))


def _from_chain_gate_kernel(y_ref, g_ref, o_ref, xt_ref, *, nb, nh, n, tt):
    for vb in range(n // SUBLANE):
        vs = slice(vb * SUBLANE, (vb + 1) * SUBLANE)
        chains_mid = jnp.swapaxes(jnp.swapaxes(y_ref[:, vs, :], 0, 1), 1, 2)
        for b in range(nb):
            xt_ref[b, :, vs, :] = jnp.swapaxes(chains_mid[:, b * nh:(b + 1) * nh, :], 0, 1)
    for b in range(nb):
        o_ref[b] = (xt_ref[b].reshape(nh * n, tt).T * g_ref[b]).astype(o_ref.dtype)


def _from_chain_gate(y, g, *, nb, nh, tt):
    m, c = g.shape
    t = m // nb
    n = c // nh
    nt = t // tt
    row = pl.BlockSpec((nb, tt, c), lambda j: (0, j, 0))
    out = pl.pallas_call(
        functools.partial(_from_chain_gate_kernel, nb=nb, nh=nh, n=n, tt=tt),
        grid=(nt,),
        in_specs=[pl.BlockSpec((tt, n, nb * nh), lambda j: (j, 0, 0)), row],
        out_specs=row,
        out_shape=jax.ShapeDtypeStruct((nb, t, c), BF16),
        scratch_shapes=[pltpu.VMEM((nb, nh, n, tt), F32)],
        compiler_params=_params("arbitrary"),
        name="from_chain_gate",
    )(y, g.reshape(nb, t, c))
    return out.reshape(m, c)


def _transpose_gate_kernel(y_ref, g_ref, o_ref):
    o_ref[...] = (y_ref[...].T * g_ref[...]).astype(o_ref.dtype)


def _transpose_gate(yt, g):
    tiles, c, tt = yt.shape
    return pl.pallas_call(
        _transpose_gate_kernel,
        grid=(tiles,),
        in_specs=[pl.BlockSpec((None, c, tt), lambda i: (i, 0, 0)),
                  pl.BlockSpec((tt, c), lambda i: (i, 0))],
        out_specs=pl.BlockSpec((tt, c), lambda i: (i, 0)),
        out_shape=jax.ShapeDtypeStruct((tiles * tt, c), BF16),
        compiler_params=_params("arbitrary"),
        name="transpose_gate",
    )(yt, g)


def _pool_kernel(u_ref, h_ref, w_ref, sc_ref, o_ref, ext_ref, *, halo, tt, shift, pos0):
    gi = pl.program_id(1)
    j = pl.program_id(2)

    @pl.when(j == 0)
    def _():
        ext_ref[0:halo, :] = h_ref[...]

    x = u_ref[...]
    ext_ref[halo:halo + tt, :] = x
    row = lax.broadcasted_iota(jnp.int32, (tt, 1), 0) + j * tt
    time = row if shift == 1 else row // shift

    for g, win in enumerate(POOL_WINDOWS):
        @pl.when(gi == g)
        def _(win=win):
            acc = x
            for back in range(1, win):
                lo = halo - back * shift
                acc = acc + ext_ref[lo:lo + tt, :]
            cnt = jnp.minimum(pos0 + time + 1, win).astype(F32)
            d = acc / cnt - x
            y = jnp.dot(d.astype(BF16), w_ref[...], preferred_element_type=F32)
            o_ref[...] = (y * sc_ref[...]).astype(o_ref.dtype)

    if tt >= halo:
        ext_ref[0:halo, :] = ext_ref[tt:tt + halo, :]


def _pool(proj, col0, hist, w_pool, scale, *, nseq, shift, tt, pos0):
    m = proj.shape[0]
    cp = scale.shape[1]
    ng = len(POOL_WINDOWS)
    gw = cp // ng
    assert col0 % gw == 0
    g0 = col0 // gw
    rows = m // nseq
    tt = _pick(rows, tt)
    nt = rows // tt
    halo = POOL_HIST * shift
    assert nt == 1 or tt >= halo
    return pl.pallas_call(
        functools.partial(_pool_kernel, halo=halo, tt=tt, shift=shift, pos0=pos0),
        grid=(nseq, ng, nt),
        in_specs=[pl.BlockSpec((tt, gw), lambda s, g, j: (s * nt + j, g0 + g)),
                  pl.BlockSpec((None, halo, gw), lambda s, g, j: (s, 0, g)),
                  pl.BlockSpec((None, gw, gw), lambda s, g, j: (g, 0, 0)),
                  pl.BlockSpec((1, gw), lambda s, g, j: (0, g))],
        out_specs=pl.BlockSpec((tt, gw), lambda s, g, j: (s * nt + j, g)),
        out_shape=jax.ShapeDtypeStruct((m, cp), BF16),
        scratch_shapes=[pltpu.VMEM((halo + tt, gw), F32)],
        compiler_params=_params("arbitrary", "arbitrary", "arbitrary"),
        name="pool_mix",
    )(proj, hist, w_pool, scale)


def _attend(q, k, v, scale):
    s = lax.dot_general(q, k, (((1,), (1,)), ((), ())), preferred_element_type=F32) * scale
    s = s - jnp.max(s, axis=-1, keepdims=True)
    e = jnp.exp(s)
    pr = e / jnp.sum(e, axis=-1, keepdims=True)
    return jnp.dot(pr.astype(BF16), v, preferred_element_type=F32)


def _attn_prompt_kernel(q_ref, k_ref, v_ref, o_ref, *, scale, heads, hd):
    for h in range(heads):
        cols = slice(h * hd, (h + 1) * hd)
        o = _attend(q_ref[:, cols], k_ref[:, cols].astype(BF16), v_ref[:, cols].astype(BF16), scale)
        o_ref[:, cols] = o.astype(o_ref.dtype)


def _attn_prompt(q, mk, mv, *, batch, heads):
    m, d = q.shape
    t = m // batch
    nm = mk.shape[1]
    hd = d // heads
    tq = _pick(t, 512)
    nq = t // tq
    return pl.pallas_call(
        functools.partial(_attn_prompt_kernel, scale=hd ** -0.5, heads=heads, hd=hd),
        grid=(batch, nq),
        in_specs=[pl.BlockSpec((tq, d), lambda b, i: (b * nq + i, 0)),
                  pl.BlockSpec((None, nm, d), lambda b, i: (b, 0, 0)),
                  pl.BlockSpec((None, nm, d), lambda b, i: (b, 0, 0))],
        out_specs=pl.BlockSpec((tq, d), lambda b, i: (b * nq + i, 0)),
        out_shape=jax.ShapeDtypeStruct((m, d), BF16),
        compiler_params=_params("arbitrary", "arbitrary"),
        name="attn_prompt",
    )(q, mk, mv)


def _attn_sample_kernel(q_ref, k_ref, v_ref, o_ref, *, scale, heads, hd):
    t = q_ref.shape[0]
    nm = k_ref.shape[0]
    q = jnp.concatenate([q_ref[:, h * hd:(h + 1) * hd] for h in range(heads)], axis=0)
    k = k_ref[...].reshape(nm * heads, hd)
    v = v_ref[...].reshape(nm * heads, hd)
    s = lax.dot_general(q, k, (((1,), (1,)), ((), ())), preferred_element_type=F32) * scale
    row_head = lax.broadcasted_iota(jnp.int32, s.shape, 0) // t
    col_head = lax.broadcasted_iota(jnp.int32, s.shape, 1) % heads
    s = jnp.where(row_head == col_head, s, -jnp.inf)
    s = s - jnp.max(s, axis=-1, keepdims=True)
    e = jnp.exp(s)
    pr = e / jnp.sum(e, axis=-1, keepdims=True)
    o = jnp.dot(pr, v, preferred_element_type=F32)
    for h in range(heads):
        o_ref[:, h * hd:(h + 1) * hd] = o[h * t:(h + 1) * t, :]


def _attn_sample(q, mk, mv, layer):
    b, t, d = q.shape
    _, _, nm, heads, hd = mk.shape
    kv_spec = pl.BlockSpec((None, None, nm, heads, hd), lambda i: (layer, i, 0, 0, 0))
    return pl.pallas_call(
        functools.partial(_attn_sample_kernel, scale=hd ** -0.5, heads=heads, hd=hd),
        grid=(b,),
        in_specs=[pl.BlockSpec((None, t, d), lambda i: (i, 0, 0)), kv_spec, kv_spec],
        out_specs=pl.BlockSpec((None, t, d), lambda i: (i, 0, 0)),
        out_shape=jax.ShapeDtypeStruct((b, t, d), F32),
        compiler_params=_params("arbitrary"),
        name="attn_sample",
    )(q, mk, mv)


def _ffn_up_kernel(h_ref, hp_ref, wg_ref, wv_ref, cw_ref, cb_ref, st_ref, act_ref, zt_ref, zext_ref, rstd_ref,
                   *, halo, tm, shift, recompute_halo):
    @pl.when(pl.program_id(2) == 0)
    def _():
        rstd_ref[...] = _row_rstd(h_ref[...])

    h = h_ref[...]
    rstd = rstd_ref[...]
    z = jnp.dot(h, wg_ref[...], preferred_element_type=F32) * rstd
    val = jnp.dot(h, wv_ref[...], preferred_element_type=F32) * rstd
    zprev = st_ref[...]
    if recompute_halo:
        hp = hp_ref[...]
        zhalo = jnp.dot(hp, wg_ref[...], preferred_element_type=F32) * _row_rstd(hp)
        zprev = jnp.where(pl.program_id(1) == 0, zprev, zhalo)
    zext_ref[0:halo, :] = zprev
    zext_ref[halo:halo + tm, :] = z
    zm1 = zext_ref[halo - shift:halo - shift + tm, :]
    zm2 = zext_ref[halo - 2 * shift:halo - 2 * shift + tm, :]
    zc = cb_ref[...] + cw_ref[0:1, :] * zm2 + cw_ref[1:2, :] * zm1 + cw_ref[2:3, :] * z
    act_ref[...] = (zc * _sigmoid(zc) * val).astype(act_ref.dtype)
    zt_ref[...] = zext_ref[tm:tm + halo, :]


def _ffn_up(h, wg, wv, cw, cb, state, *, nseq, shift, halo, tm, tn):
    m, d = h.shape
    _, fp = wg.shape
    rows = m // nseq
    tm = _pick(rows, tm)
    nt = rows // tm
    tn = min(tn, fp)
    assert tm >= halo and (nt == 1 or halo == SUBLANE)
    hb = tm // SUBLANE
    return pl.pallas_call(
        functools.partial(_ffn_up_kernel, halo=halo, tm=tm, shift=shift, recompute_halo=nt > 1),
        grid=(nseq, nt, pl.cdiv(fp, tn)),
        in_specs=[pl.BlockSpec((tm, d), lambda s, i, j: (s * nt + i, 0)),
                  pl.BlockSpec((SUBLANE, d), lambda s, i, j: (jnp.maximum((s * nt + i) * hb - 1, 0), 0)),
                  pl.BlockSpec((d, tn), lambda s, i, j: (0, j)),
                  pl.BlockSpec((d, tn), lambda s, i, j: (0, j)),
                  pl.BlockSpec((CONV_W, tn), lambda s, i, j: (0, j)),
                  pl.BlockSpec((1, tn), lambda s, i, j: (0, j)),
                  pl.BlockSpec((None, halo, tn), lambda s, i, j: (s, 0, j))],
        out_specs=[pl.BlockSpec((tm, tn), lambda s, i, j: (s * nt + i, j)),
                   pl.BlockSpec((halo, tn), lambda s, i, j: (s * nt + i, j))],
        out_shape=[jax.ShapeDtypeStruct((m, fp), BF16),
                   jax.ShapeDtypeStruct((nseq * nt * halo, fp), F32)],
        scratch_shapes=[pltpu.VMEM((halo + tm, tn), F32), pltpu.VMEM((tm, 1), F32)],
        compiler_params=_params("arbitrary", "arbitrary", "arbitrary"),
        name="ffn_up",
    )(h, h, wg, wv, cw, cb, state)


def kernel(x_prompt, x_sample, mem_prompt, cache_mem_k, cache_mem_v, state_shift, state_wkv, state_pool,
           state_conv, g_mix, w_in, mu_shift, w0, b_w, a0, b_a, b_g, k_k, k_a, r_k, gn_w, gn_b, w_pool,
           pool_scale, w_out, g_mem, w_mk, w_mv, g_attn, w_mq, w_mo, g_ffn, w_gate, w_val, conv_w, conv_b,
           w_down, g_final):
    bp, tp, d = x_prompt.shape
    bs, ts, _ = x_sample.shape
    depth = w_in.shape[0]
    c = w0.shape[1]
    nh = c // HEAD_SIZE
    nvb = HEAD_SIZE // SUBLANE
    cp = pool_scale.shape[1]
    pw = mu_shift.shape[1]
    lora = pw - 3 * c
    lp = _round_up(lora, LANE)
    ca = 3 * c + lp
    dl, al, gl = b_w.shape[1], b_a.shape[1], b_g.shape[1]
    f = w_gate.shape[2]
    fp = f
    nmem = mem_prompt.shape[1]
    nbuf = state_pool.shape[2]
    mp, ms = bp * tp, bs * ts

    xp = x_prompt.reshape(mp, d)
    xs = x_sample.transpose(1, 0, 2).reshape(ms, d)

    outs = {n: [] for n in ("pm_k", "pm_v", "p_sh", "p_wkv", "p_pool", "p_conv",
                            "s_sh", "s_wkv", "s_pool", "s_conv")}

    for l in range(depth):
        wa, wu = _split_cast(w_in[l], ca, pw)
        mu = jnp.pad(mu_shift[l], (0, ca - pw)).reshape(1, ca)
        bw = jnp.zeros((lp, c), F32).at[0:dl].set(b_w[l]).astype(BF16)
        ba = jnp.zeros((lp, c), F32).at[dl:dl + al].set(b_a[l]).astype(BF16)
        bg = jnp.zeros((lp, c), F32).at[dl + al:dl + al + gl].set(b_g[l]).astype(BF16)
        w0l = w0[l].reshape(1, c)
        a0l = a0[l].reshape(1, c)
        wpool = w_pool[l].astype(BF16)
        pscale = pool_scale[l].reshape(1, cp)
        wmk = w_mk[l].astype(BF16)
        wmv = w_mv[l].astype(BF16)
        late = {}
        cw = conv_w[l]
        cb = conv_b[l].reshape(1, fp)

        def head_tile(p):
            return p.reshape(nh, HEAD_SIZE).T

        chain_p = [jnp.tile(head_tile(p), (1, bp)).reshape(1, HEAD_SIZE, bp * nh)
                   for p in (k_k[l], k_a[l], r_k[l].reshape(-1), gn_w[l], gn_b[l])]
        chain_s = [jnp.broadcast_to(p.reshape(nh, HEAD_SIZE, 1), (nh, HEAD_SIZE, bs))
                   for p in (k_k[l], k_a[l], r_k[l].reshape(-1), gn_w[l], gn_b[l])]

        mn = _rmsnorm(mem_prompt.reshape(bp * nmem, d), g_mem[l], BF16)
        mk = _matmul(mn, wmk, out_dtype=F32, name="mem_k")
        mv = _matmul(mn, wmv, out_dtype=F32, name="mem_v")
        outs["pm_k"].append(mk.reshape(bp, nmem, MEM_HEADS, d // MEM_HEADS))
        outs["pm_v"].append(mv.reshape(bp, nmem, MEM_HEADS, d // MEM_HEADS))

        def layer(x, *, prompt):
            m = x.shape[0]
            if prompt:
                nseq, shift, halo1, halo, pos0 = bp, 1, SUBLANE, SUBLANE, 0
                st_shift = jnp.zeros((bp, halo1, ca), F32)
                hist = jnp.zeros((bp, POOL_HIST, cp), F32)
                st_conv = jnp.zeros((bp, halo, fp), F32)
            else:
                nseq, shift, halo1, halo, pos0 = 1, bs, bs, (CONV_W - 1) * bs, PAST_LEN
                st_shift = jnp.pad(state_shift[l], ((0, 0), (0, ca - pw))).reshape(1, halo1, ca)
                hist = jnp.pad(state_pool[l].transpose(1, 0, 2), ((POOL_HIST - nbuf, 0), (0, 0), (0, 0)))
                hist = hist.reshape(1, POOL_HIST * bs, cp)
                st_conv = state_conv[l].transpose(1, 0, 2).reshape(1, halo, fp)

            xn = _rmsnorm(x, g_mix[l], BF16)
            proj = _matmul_cols2(xn, wa, wu, name="in_proj")
            tile = min(LANE, tp) if prompt else bs
            *rkvda, g = _rwkv_prep(proj, st_shift, mu, w0l, a0l, bw, ba, bg,
                                   nseq=nseq, shift=shift, halo=halo1, tt=tile, c=c, ca=ca)
            if prompt:
                chains = [_to_chain(x, nb=bp, nh=nh).reshape(1, tp, HEAD_SIZE, bp * nh) for x in rkvda]
                s0 = jnp.zeros((1, nvb, HEAD_SIZE, SUBLANE, bp * nh), F32)
                y, s_fin, cast_out = _wkv(*chains, s0, *chain_p, tt=16, group_major=True,
                                          cast=((w_out[l], None), (w_mq[l], g_attn[l]), (w_mo[l], None),
                                                (w_gate[l], g_ffn[l]), (w_val[l], g_ffn[l]),
                                                (w_down[l], None)))
                late.update(zip(("wo", "wmq", "wmo", "wg", "wv", "wd"), cast_out))
                r_out = _from_chain_gate(y.reshape(tp, HEAD_SIZE, bp * nh), g, nb=bp, nh=nh, tt=tile)
                new_wkv = s_fin.reshape(nvb, HEAD_SIZE, SUBLANE, bp, nh).transpose(3, 4, 0, 2, 1)
                new_wkv = new_wkv.reshape(bp, nh, HEAD_SIZE, HEAD_SIZE)
            else:
                chains = [x.reshape(ts, nh, HEAD_SIZE, bs) for x in rkvda]
                s0 = state_wkv[l].reshape(bs, nh, nvb, SUBLANE, HEAD_SIZE).transpose(1, 2, 4, 3, 0)
                y, s_fin, _ = _wkv(*chains, s0, *chain_s, tt=ts, group_major=False)
                r_out = _transpose_gate(y.reshape(ts, c, bs), g)
                new_wkv = s_fin.transpose(4, 0, 1, 3, 2).reshape(bs, nh, HEAD_SIZE, HEAD_SIZE)
            p_out = _pool(proj, ca, hist, wpool, pscale, nseq=nseq, shift=shift,
                          tt=256 if prompt else m, pos0=pos0)
            wmq, wmo, wg, wv, wd = (late[n] for n in ("wmq", "wmo", "wg", "wv", "wd"))
            x1, x1b = _matmul2_res(r_out, p_out, late["wo"], x)

            if prompt:
                q = _matmul(x1b, wmq, out_dtype=BF16, row_norm=True, name="attn_q")
                o = _attn_prompt(q, mk.reshape(bp, nmem, d), mv.reshape(bp, nmem, d),
                                 batch=bp, heads=MEM_HEADS)
            else:
                q = _matmul(x1b, wmq, out_dtype=F32, row_norm=True, name="attn_q")
                q = q.reshape(ts, bs, d).transpose(1, 0, 2)
                o = _attn_sample(q, cache_mem_k, cache_mem_v, l)
                o = o.transpose(1, 0, 2).reshape(m, d)
            x2, x2b = _matmul(o, wmo, out_dtype=F32, res=x1, emit_bf16=True, name="attn_o")

            act, ztail = _ffn_up(x2b, wg, wv, cw, cb, st_conv, nseq=nseq, shift=shift, halo=halo,
                                 tm=1024, tn=512)
            x3 = _matmul(act, wd, out_dtype=F32, res=x2, tm=512, tn=512, name="ffn_down")

            if prompt:
                new_shift = proj.reshape(bp, tp, ca + cp)[:, -1, :pw]
                new_pool = proj.reshape(bp, tp, ca + cp)[:, tp - nbuf:, ca:]
                new_conv = ztail.reshape(bp, -1, halo, fp)[:, -1, halo - (CONV_W - 1):, :f]
            else:
                new_shift = proj[(ts - 1) * bs:, :pw]
                u_b = proj[:, ca:].reshape(ts, bs, cp).transpose(1, 0, 2)
                new_pool = jnp.concatenate([state_pool[l], u_b], axis=1)[:, -nbuf:]
                new_conv = ztail.reshape(CONV_W - 1, bs, fp)[:, :, :f].transpose(1, 0, 2)
            return x3, new_shift, new_wkv, new_pool, new_conv

        xp, sh, wk, po, co = layer(xp, prompt=True)
        outs["p_sh"].append(sh)
        outs["p_wkv"].append(wk)
        outs["p_pool"].append(po)
        outs["p_conv"].append(co)
        xs, sh, wk, po, co = layer(xs, prompt=False)
        outs["s_sh"].append(sh)
        outs["s_wkv"].append(wk)
        outs["s_pool"].append(po)
        outs["s_conv"].append(co)

    y_prompt = _rmsnorm(xp, g_final, F32).reshape(bp, tp, d)
    y_sample = _rmsnorm(xs, g_final, F32).reshape(ts, bs, d).transpose(1, 0, 2)
    stk = lambda n: jnp.stack(outs[n])
    return (y_prompt, y_sample, stk("pm_k"), stk("pm_v"), stk("p_sh"), stk("p_wkv"), stk("p_pool"),
            stk("p_conv"), stk("s_sh"), stk("s_wkv"), stk("s_pool"), stk("s_conv"))
```

```python
import functools

import jax
import jax.numpy as jnp
from jax import lax
from jax.experimental import pallas as pl
from jax.experimental.pallas import tpu as pltpu

F32 = jnp.float32
BF16 = jnp.bfloat16

HEAD_SIZE = 64
POOL_WINDOWS = (2, 4, 8, 16)
POOL_HIST = 16
MEM_HEADS = 4
CONV_W = 3
PAST_LEN = 16384
NORM_EPS = 1e-6
GN_EPS = 64e-5
EXP_M_HALF = 0.6065306597126334
LANE = 128
SUBLANE = 8
VMEM_LIMIT = 56 * 1024 * 1024


def _pick(dim, pref, mult=SUBLANE):
    if dim <= pref:
        return dim
    t = (pref // mult) * mult
    while t >= mult:
        if dim % t == 0:
            return t
        t -= mult
    return dim


def _round_up(x, m):
    return (x + m - 1) // m * m


def _params(*sem):
    return pltpu.CompilerParams(dimension_semantics=sem, vmem_limit_bytes=VMEM_LIMIT)


def _sigmoid(x):
    return 1.0 / (1.0 + jnp.exp(-x))


def _rmsnorm_kernel(x_ref, g_ref, o_ref):
    x = x_ref[...].astype(F32)
    ms = jnp.mean(x * x, axis=-1, keepdims=True)
    o_ref[...] = (x * lax.rsqrt(ms + NORM_EPS) * g_ref[...]).astype(o_ref.dtype)


def _rmsnorm(x, g, out_dtype):
    m, d = x.shape
    tr = _pick(m, 256)
    return pl.pallas_call(
        _rmsnorm_kernel,
        grid=(m // tr,),
        in_specs=[pl.BlockSpec((tr, d), lambda i: (i, 0)),
                  pl.BlockSpec((1, d), lambda i: (0, 0))],
        out_specs=pl.BlockSpec((tr, d), lambda i: (i, 0)),
        out_shape=jax.ShapeDtypeStruct((m, d), out_dtype),
        compiler_params=_params("arbitrary"),
        name="rmsnorm",
    )(x, g.reshape(1, d).astype(F32))


def _row_rstd(x):
    xf = x.astype(F32)
    return lax.rsqrt(jnp.mean(xf * xf, axis=-1, keepdims=True) + NORM_EPS)


def _mm_kernel(*refs, has_res, emit_bf16, row_norm):
    refs = list(refs)
    x_ref, w_ref = refs[:2]
    del refs[:2]
    res_ref = refs.pop(0) if has_res else None
    o_ref = refs.pop(0)
    ob_ref = refs.pop(0) if emit_bf16 else None
    rstd_ref = refs.pop(0) if row_norm else None
    if row_norm:
        @pl.when(pl.program_id(1) == 0)
        def _():
            rstd_ref[...] = _row_rstd(x_ref[...])
    acc = jnp.dot(x_ref[...].astype(BF16), w_ref[...], preferred_element_type=F32)
    if row_norm:
        acc = acc * rstd_ref[...]
    if has_res:
        acc = acc + res_ref[...]
    o_ref[...] = acc.astype(o_ref.dtype)
    if emit_bf16:
        ob_ref[...] = acc.astype(BF16)


def _matmul(x, w, *, out_dtype, res=None, emit_bf16=False, row_norm=False, tm=1024, tn=512, name="matmul"):
    m, kd = x.shape
    _, n = w.shape
    tm = _pick(m, tm)
    tn = _pick(n, tn, LANE)
    tile = pl.BlockSpec((tm, tn), lambda i, j: (i, j))
    in_specs = [pl.BlockSpec((tm, kd), lambda i, j: (i, 0)),
                pl.BlockSpec((kd, tn), lambda i, j: (0, j))]
    args = [x, w]
    if res is not None:
        in_specs.append(tile)
        args.append(res)
    out_specs, out_shape = [tile], [jax.ShapeDtypeStruct((m, n), out_dtype)]
    if emit_bf16:
        out_specs.append(tile)
        out_shape.append(jax.ShapeDtypeStruct((m, n), BF16))
    out = pl.pallas_call(
        functools.partial(_mm_kernel, has_res=res is not None, emit_bf16=emit_bf16, row_norm=row_norm),
        grid=(m // tm, n // tn),
        in_specs=in_specs,
        out_specs=out_specs,
        out_shape=out_shape,
        scratch_shapes=[pltpu.VMEM((tm, 1), F32)] if row_norm else [],
        compiler_params=_params("arbitrary", "arbitrary"),
        name=name,
    )(*args)
    return out if emit_bf16 else out[0]


def _mm_cols2_kernel(x_ref, wa_ref, wb_ref, o_ref, *, na):
    j = pl.program_id(1)
    contract_last = (((1,), (1,)), ((), ()))

    @pl.when(j < na)
    def _():
        o_ref[...] = lax.dot_general(x_ref[...], wa_ref[...], contract_last, preferred_element_type=F32)

    @pl.when(j >= na)
    def _():
        o_ref[...] = lax.dot_general(x_ref[...], wb_ref[...], contract_last, preferred_element_type=F32)


def _matmul_cols2(x, wa_t, na_cols, wb_t, *, tm=1024, tn=512, name="matmul"):
    m, kd = x.shape
    tm = _pick(m, tm)
    tn = _pick(na_cols, tn, LANE)
    assert na_cols % tn == 0 and wb_t.shape[0] % tn == 0
    na, nb = na_cols // tn, wb_t.shape[0] // tn
    return pl.pallas_call(
        functools.partial(_mm_cols2_kernel, na=na),
        grid=(m // tm, na + nb),
        in_specs=[pl.BlockSpec((tm, kd), lambda i, j: (i, 0)),
                  pl.BlockSpec((tn, kd), lambda i, j: (jnp.minimum(j, na - 1), 0)),
                  pl.BlockSpec((tn, kd), lambda i, j: (jnp.maximum(j - na, 0), 0))],
        out_specs=pl.BlockSpec((tm, tn), lambda i, j: (i, j)),
        out_shape=jax.ShapeDtypeStruct((m, (na + nb) * tn), F32),
        compiler_params=_params("arbitrary", "arbitrary"),
        name=name,
    )(x, wa_t, wb_t)


def _mm2_kernel(x1_ref, x2_ref, w1_ref, w2_ref, res_ref, o_ref, ob_ref):
    acc = jnp.dot(x1_ref[...], w1_ref[...], preferred_element_type=F32)
    acc = acc + jnp.dot(x2_ref[...], w2_ref[...], preferred_element_type=F32)
    acc = acc + res_ref[...]
    o_ref[...] = acc
    ob_ref[...] = acc.astype(BF16)


def _matmul2_res(x1, x2, w, res, *, tm=1024, tn=512):
    m, k1 = x1.shape
    _, k2 = x2.shape
    kw, n = w.shape
    assert k1 == k2 and kw == k1 + k2
    tm = _pick(m, tm)
    tn = _pick(n, tn, LANE)
    return pl.pallas_call(
        _mm2_kernel,
        grid=(m // tm, n // tn),
        in_specs=[pl.BlockSpec((tm, k1), lambda i, j: (i, 0)),
                  pl.BlockSpec((tm, k2), lambda i, j: (i, 0)),
                  pl.BlockSpec((k1, tn), lambda i, j: (0, j)),
                  pl.BlockSpec((k2, tn), lambda i, j: (1, j)),
                  pl.BlockSpec((tm, tn), lambda i, j: (i, j))],
        out_specs=[pl.BlockSpec((tm, tn), lambda i, j: (i, j))] * 2,
        out_shape=[jax.ShapeDtypeStruct((m, n), F32), jax.ShapeDtypeStruct((m, n), BF16)],
        compiler_params=_params("arbitrary", "arbitrary"),
        name="out_proj",
    )(x1, x2, w, w, res)


def _prep_kernel(p_ref, st_ref, mu_ref, w0_ref, a0_ref, bw_ref, ba_ref, bg_ref,
                 r_ref, k_ref, v_ref, d_ref, a_ref, g_ref, ext_ref, *, halo, tt, shift, c):
    @pl.when(pl.program_id(1) == 0)
    def _():
        ext_ref[0:halo, :] = st_ref[...]

    p = p_ref[...]
    ext_ref[halo:halo + tt, :] = p
    prev = ext_ref[halo - shift:halo - shift + tt, :]
    xs = p + (prev - p) * mu_ref[...]
    ext_ref[0:halo, :] = ext_ref[tt:tt + halo, :]

    r_ref[...] = xs[:, 0:c].T
    k_ref[...] = xs[:, c:2 * c].T
    v_ref[...] = xs[:, 2 * c:3 * c].T
    lo = xs[:, 3 * c:]
    wpre = jnp.dot(jnp.tanh(lo).astype(BF16), bw_ref[...], preferred_element_type=F32)
    apre = jnp.dot(lo.astype(BF16), ba_ref[...], preferred_element_type=F32)
    g_ref[...] = jnp.dot(_sigmoid(lo).astype(BF16), bg_ref[...], preferred_element_type=F32)
    ez = jnp.exp(-(w0_ref[...] + wpre))
    d_ref[...] = jnp.exp(-EXP_M_HALF / (1.0 + ez)).T
    a_ref[...] = _sigmoid(a0_ref[...] + apre).T


def _rwkv_prep(proj, state, mu, w0, a0, bw, ba, bg, *, nseq, shift, halo, tt, c, ca):
    m = proj.shape[0]
    rows = m // nseq
    assert rows % tt == 0
    nt = rows // tt
    row_spec = lambda width: pl.BlockSpec((tt, width), lambda s, j: (s * nt + j, 0))
    tr_spec = pl.BlockSpec((None, c, tt), lambda s, j: (s * nt + j, 0, 0))
    full = lambda arr: pl.BlockSpec(arr.shape, lambda s, j: (0,) * arr.ndim)
    out_t = jax.ShapeDtypeStruct((nseq * nt, c, tt), F32)
    return pl.pallas_call(
        functools.partial(_prep_kernel, halo=halo, tt=tt, shift=shift, c=c),
        grid=(nseq, nt),
        in_specs=[row_spec(ca),
                  pl.BlockSpec((None, halo, ca), lambda s, j: (s, 0, 0)),
                  full(mu), full(w0), full(a0), full(bw), full(ba), full(bg)],
        out_specs=[tr_spec] * 5 + [row_spec(c)],
        out_shape=[out_t] * 5 + [jax.ShapeDtypeStruct((m, c), F32)],
        scratch_shapes=[pltpu.VMEM((halo + tt, ca), F32)],
        compiler_params=_params("arbitrary", "arbitrary"),
        name="rwkv_prep",
    )(proj, state, mu, w0, a0, bw, ba, bg)


V_GROUP = 4
K_BLOCK = 32


def _wkv_kernel(*refs, tt, n, cast_steps, cast_gain):
    refs = list(refs)
    (r_ref, k_ref, v_ref, d_ref, a_ref, s0_ref, kk_ref, ka_ref, rk_ref, gw_ref, gb_ref) = refs[:11]
    del refs[:11]
    nc = len(cast_steps)
    w_refs = [refs.pop(0) for _ in range(nc)]
    wg_refs = [refs.pop(0) if has_gain else None for has_gain in cast_gain]
    o_ref, sf_ref = refs.pop(0), refs.pop(0)
    wb_refs = [refs.pop(0) for _ in range(nc)]
    s_ref, y_ref, row_ref, dot_ref, plast_ref = refs

    @pl.when(pl.program_id(1) == 0)
    def _():
        s_ref[...] = s0_ref[...]

    for w_ref, wg_ref, wb_ref, steps in zip(w_refs, wg_refs, wb_refs, cast_steps):
        @pl.when(pl.program_id(1) < steps)
        def _(w_ref=w_ref, wg_ref=wg_ref, wb_ref=wb_ref):
            w = w_ref[...] if wg_ref is None else w_ref[...] * wg_ref[...]
            wb_ref[...] = w.astype(wb_ref.dtype)

    k_k = kk_ref[...]
    k_a = ka_ref[...]
    r_k = rk_ref[...]
    gn_w = gw_ref[...]
    gn_b = gb_ref[...]
    inv_n = 1.0 / n
    lanes = s_ref.shape[-1]

    r = r_ref[...]
    k = k_ref[...]
    w = d_ref[...]
    a = a_ref[...]
    kk = k * k_k
    norm = jnp.sqrt(jnp.sum(kk * kk, axis=1, keepdims=True))
    kk = kk / jnp.maximum(norm, 1e-12)
    k2 = k * (1.0 + (a - 1.0) * k_a)
    bv = kk * a
    p = None
    for t in range(tt):
        row_ref[0, t] = -kk[t] if p is None else -kk[t] * p
        p = w[t] if p is None else p * w[t]
        inv_p = 1.0 / p
        row_ref[1, t] = r[t] * p
        row_ref[2, t] = bv[t] * inv_p
        row_ref[3, t] = k2[t] * inv_p
    plast_ref[...] = p
    dot_ref[0] = jnp.sum(bv * r, axis=1, keepdims=True)
    dot_ref[1] = jnp.sum(k2 * r, axis=1, keepdims=True)
    dot_ref[2] = jnp.sum(r * k2 * r_k, axis=1, keepdims=True)

    def token(t, carry):
        b_r = dot_ref[0, t]
        k_r = dot_ref[1, t]

        def vgroup(gidx, c2):
            vb0 = gidx * V_GROUP
            zero = jnp.zeros((SUBLANE, lanes), F32)

            def reduce_keys(kb, acc):
                sa_acc, q_acc = list(acc[0]), list(acc[1])
                for j in range(K_BLOCK):
                    kx = kb * K_BLOCK + j
                    a_row = row_ref[0, t, pl.ds(kx, 1), :]
                    q_row = row_ref[1, t, pl.ds(kx, 1), :]
                    for i in range(V_GROUP):
                        s_k = s_ref[vb0 + i, kx]
                        sa_acc[i] = sa_acc[i] + s_k * a_row
                        q_acc[i] = q_acc[i] + s_k * q_row
                return tuple(sa_acc), tuple(q_acc)

            sa8, q8 = lax.fori_loop(0, n // K_BLOCK, reduce_keys, ((zero,) * V_GROUP, (zero,) * V_GROUP))
            base = pl.multiple_of(vb0 * SUBLANE, SUBLANE)
            v8 = [v_ref[t, pl.ds(base + i * SUBLANE, SUBLANE), :] for i in range(V_GROUP)]

            def update_keys(kb, c3):
                for j in range(K_BLOCK):
                    kx = kb * K_BLOCK + j
                    b_row = row_ref[2, t, pl.ds(kx, 1), :]
                    k_row = row_ref[3, t, pl.ds(kx, 1), :]
                    for i in range(V_GROUP):
                        s_ref[vb0 + i, kx] = s_ref[vb0 + i, kx] + sa8[i] * b_row + v8[i] * k_row
                return c3

            lax.fori_loop(0, n // K_BLOCK, update_keys, 0)
            for i in range(V_GROUP):
                y_ref[t, pl.ds(base + i * SUBLANE, SUBLANE), :] = q8[i] + sa8[i] * b_r + v8[i] * k_r
            return c2

        lax.fori_loop(0, n // (SUBLANE * V_GROUP), vgroup, 0)
        return carry

    lax.fori_loop(0, tt, token, 0)

    def rescale(vb, carry):
        for kx in range(n):
            s_ref[vb, kx] = s_ref[vb, kx] * plast_ref[kx:kx + 1, :]
        return carry

    lax.fori_loop(0, n // SUBLANE, rescale, 0)

    y = y_ref[...]
    mean = jnp.sum(y, axis=1, keepdims=True) * inv_n
    yc = y - mean
    var = jnp.sum(yc * yc, axis=1, keepdims=True) * inv_n
    o_ref[...] = yc * lax.rsqrt(var + GN_EPS) * gn_w + gn_b + dot_ref[2] * v_ref[...]

    @pl.when(pl.program_id(1) == pl.num_programs(1) - 1)
    def _():
        sf_ref[...] = s_ref[...]


def _cast_rows(rows, nsteps):
    blk = _round_up(-(-rows // nsteps), 2 * SUBLANE)
    while rows % blk:
        blk += 2 * SUBLANE
    return blk, rows // blk


def _wkv(r, k, v, d, a, s0, k_k, k_a, r_k, gn_w, gn_b, *, tt, group_major, cast=()):
    if group_major:
        g, t, n, lanes = r.shape
    else:
        t, g, n, lanes = r.shape
    tt = _pick(t, tt, 1)
    nb = n // SUBLANE
    assert not cast or g == 1
    cast_specs, gain_specs, cast_steps = [], [], []
    for w, gain in cast:
        blk, steps = _cast_rows(w.shape[0], t // tt)
        cast_steps.append(steps)
        index = lambda i, j, last=steps - 1: (jnp.minimum(j, last), 0)
        cast_specs.append(pl.BlockSpec((blk, w.shape[1]), index))
        if gain is not None:
            gain_specs.append(pl.BlockSpec((blk, 1), index))
    weights = [w for w, _ in cast]
    gains = [gain.reshape(-1, 1).astype(F32) for _, gain in cast if gain is not None]
    cast_gain = tuple(gain is not None for _, gain in cast)
    if group_major:
        tok = pl.BlockSpec((None, tt, n, lanes), lambda i, j: (i, j, 0, 0))
    else:
        tok = pl.BlockSpec((tt, None, n, lanes), lambda i, j: (j, i, 0, 0))
    st = pl.BlockSpec((None, nb, n, SUBLANE, lanes), lambda i, j: (i, 0, 0, 0, 0))
    gp = k_k.shape[0]
    par = pl.BlockSpec((None, n, lanes), (lambda i, j: (i, 0, 0)) if gp > 1 else (lambda i, j: (0, 0, 0)))
    y, s_fin, *cast_out = pl.pallas_call(
        functools.partial(_wkv_kernel, tt=tt, n=n, cast_steps=tuple(cast_steps), cast_gain=cast_gain),
        grid=(g, t // tt),
        in_specs=[tok] * 5 + [st] + [par] * 5 + cast_specs + gain_specs,
        out_specs=[tok, st] + cast_specs,
        out_shape=[jax.ShapeDtypeStruct(r.shape, F32),
                   jax.ShapeDtypeStruct((g, nb, n, SUBLANE, lanes), F32)]
                  + [jax.ShapeDtypeStruct(w.shape, BF16) for w in weights],
        scratch_shapes=[pltpu.VMEM((nb, n, SUBLANE, lanes), F32), pltpu.VMEM((tt, n, lanes), F32),
                        pltpu.VMEM((4, tt, n, lanes), F32), pltpu.VMEM((3, tt, 1, lanes), F32),
                        pltpu.VMEM((n, lanes), F32)],
        compiler_params=_params("arbitrary", "arbitrary"),
        name="wkv_recurrence",
    )(r, k, v, d, a, s0, k_k, k_a, r_k, gn_w, gn_b, *weights, *gains)
    return y, s_fin, cast_out


def _to_chain_kernel(x_ref, o_ref, *, nb, nh, n, tt):
    for kb in range(n // SUBLANE):
        ks = slice(kb * SUBLANE, (kb + 1) * SUBLANE)
        per_key = jnp.concatenate([jnp.swapaxes(x_ref[b, :, ks, :], 0, 1) for b in range(nb)], axis=1)
        chains_last = jnp.swapaxes(per_key, 1, 2)
        o_ref[:, ks, :] = jnp.swapaxes(chains_last, 0, 1)


def _to_chain(xt, *, nb, nh):
    tiles, c, tt = xt.shape
    nt = tiles // nb
    n = c // nh
    return pl.pallas_call(
        functools.partial(_to_chain_kernel, nb=nb, nh=nh, n=n, tt=tt),
        grid=(nt,),
        in_specs=[pl.BlockSpec((nb, None, nh, n, tt), lambda j: (0, j, 0, 0, 0))],
        out_specs=pl.BlockSpec((tt, n, nb * nh), lambda j: (j, 0, 0)),
        out_shape=jax.ShapeDtypeStruct((nt * tt, n, nb * nh), F32),
        compiler_params=_params("arbitrary"),
        name="to_chain",
    )(xt.reshape(nb, nt, nh, n, tt))


def _from_chain_gate_kernel(y_ref, g_ref, o_ref, xt_ref, *, nb, nh, n, tt):
    for vb in range(n // SUBLANE):
        vs = slice(vb * SUBLANE, (vb + 1) * SUBLANE)
        chains_mid = jnp.swapaxes(jnp.swapaxes(y_ref[:, vs, :], 0, 1), 1, 2)
        for b in range(nb):
            xt_ref[b, :, vs, :] = jnp.swapaxes(chains_mid[:, b * nh:(b + 1) * nh, :], 0, 1)
    for b in range(nb):
        o_ref[b] = (xt_ref[b].reshape(nh * n, tt).T * g_ref[b]).astype(o_ref.dtype)


def _from_chain_gate(y, g, *, nb, nh, tt):
    m, c = g.shape
    t = m // nb
    n = c // nh
    nt = t // tt
    row = pl.BlockSpec((nb, tt, c), lambda j: (0, j, 0))
    out = pl.pallas_call(
        functools.partial(_from_chain_gate_kernel, nb=nb, nh=nh, n=n, tt=tt),
        grid=(nt,),
        in_specs=[pl.BlockSpec((tt, n, nb * nh), lambda j: (j, 0, 0)), row],
        out_specs=row,
        out_shape=jax.ShapeDtypeStruct((nb, t, c), BF16),
        scratch_shapes=[pltpu.VMEM((nb, nh, n, tt), F32)],
        compiler_params=_params("arbitrary"),
        name="from_chain_gate",
    )(y, g.reshape(nb, t, c))
    return out.reshape(m, c)


def _transpose_gate_kernel(y_ref, g_ref, o_ref):
    o_ref[...] = (y_ref[...].T * g_ref[...]).astype(o_ref.dtype)


def _transpose_gate(yt, g):
    tiles, c, tt = yt.shape
    return pl.pallas_call(
        _transpose_gate_kernel,
        grid=(tiles,),
        in_specs=[pl.BlockSpec((None, c, tt), lambda i: (i, 0, 0)),
                  pl.BlockSpec((tt, c), lambda i: (i, 0))],
        out_specs=pl.BlockSpec((tt, c), lambda i: (i, 0)),
        out_shape=jax.ShapeDtypeStruct((tiles * tt, c), BF16),
        compiler_params=_params("arbitrary"),
        name="transpose_gate",
    )(yt, g)


def _pool_kernel(u_ref, h_ref, w_ref, sc_ref, o_ref, ext_ref, *, halo, tt, shift, pos0):
    gi = pl.program_id(1)
    j = pl.program_id(2)

    @pl.when(j == 0)
    def _():
        ext_ref[0:halo, :] = h_ref[...]

    x = u_ref[...]
    ext_ref[halo:halo + tt, :] = x
    row = lax.broadcasted_iota(jnp.int32, (tt, 1), 0) + j * tt
    time = row if shift == 1 else row // shift

    for g, win in enumerate(POOL_WINDOWS):
        @pl.when(gi == g)
        def _(win=win):
            acc = x
            for back in range(1, win):
                lo = halo - back * shift
                acc = acc + ext_ref[lo:lo + tt, :]
            cnt = jnp.minimum(pos0 + time + 1, win).astype(F32)
            d = acc / cnt - x
            y = jnp.dot(d.astype(BF16), w_ref[...], preferred_element_type=F32)
            o_ref[...] = (y * sc_ref[...]).astype(o_ref.dtype)

    if tt >= halo:
        ext_ref[0:halo, :] = ext_ref[tt:tt + halo, :]


def _pool(proj, col0, hist, w_pool, scale, *, nseq, shift, tt, pos0):
    m = proj.shape[0]
    cp = scale.shape[1]
    ng = len(POOL_WINDOWS)
    gw = cp // ng
    assert col0 % gw == 0
    g0 = col0 // gw
    rows = m // nseq
    tt = _pick(rows, tt)
    nt = rows // tt
    halo = POOL_HIST * shift
    assert nt == 1 or tt >= halo
    return pl.pallas_call(
        functools.partial(_pool_kernel, halo=halo, tt=tt, shift=shift, pos0=pos0),
        grid=(nseq, ng, nt),
        in_specs=[pl.BlockSpec((tt, gw), lambda s, g, j: (s * nt + j, g0 + g)),
                  pl.BlockSpec((None, halo, gw), lambda s, g, j: (s, 0, g)),
                  pl.BlockSpec((None, gw, gw), lambda s, g, j: (g, 0, 0)),
                  pl.BlockSpec((1, gw), lambda s, g, j: (0, g))],
        out_specs=pl.BlockSpec((tt, gw), lambda s, g, j: (s * nt + j, g)),
        out_shape=jax.ShapeDtypeStruct((m, cp), BF16),
        scratch_shapes=[pltpu.VMEM((halo + tt, gw), F32)],
        compiler_params=_params("arbitrary", "arbitrary", "arbitrary"),
        name="pool_mix",
    )(proj, hist, w_pool, scale)


def _attend(q, k, v, scale):
    s = lax.dot_general(q, k, (((1,), (1,)), ((), ())), preferred_element_type=F32) * scale
    s = s - jnp.max(s, axis=-1, keepdims=True)
    e = jnp.exp(s)
    pr = e / jnp.sum(e, axis=-1, keepdims=True)
    return jnp.dot(pr.astype(BF16), v, preferred_element_type=F32)


def _attn_prompt_kernel(q_ref, k_ref, v_ref, o_ref, *, scale, heads, hd):
    for h in range(heads):
        cols = slice(h * hd, (h + 1) * hd)
        o = _attend(q_ref[:, cols], k_ref[:, cols].astype(BF16), v_ref[:, cols].astype(BF16), scale)
        o_ref[:, cols] = o.astype(o_ref.dtype)


def _attn_prompt(q, mk, mv, *, batch, heads):
    m, d = q.shape
    t = m // batch
    nm = mk.shape[1]
    hd = d // heads
    tq = _pick(t, 512)
    nq = t // tq
    return pl.pallas_call(
        functools.partial(_attn_prompt_kernel, scale=hd ** -0.5, heads=heads, hd=hd),
        grid=(batch, nq),
        in_specs=[pl.BlockSpec((tq, d), lambda b, i: (b * nq + i, 0)),
                  pl.BlockSpec((None, nm, d), lambda b, i: (b, 0, 0)),
                  pl.BlockSpec((None, nm, d), lambda b, i: (b, 0, 0))],
        out_specs=pl.BlockSpec((tq, d), lambda b, i: (b * nq + i, 0)),
        out_shape=jax.ShapeDtypeStruct((m, d), BF16),
        compiler_params=_params("arbitrary", "arbitrary"),
        name="attn_prompt",
    )(q, mk, mv)


def _attn_sample_kernel(q_ref, k_ref, v_ref, o_ref, *, scale, heads, hd):
    t = q_ref.shape[0]
    nm = k_ref.shape[0]
    q = jnp.concatenate([q_ref[:, h * hd:(h + 1) * hd] for h in range(heads)], axis=0)
    k = k_ref[...].reshape(nm * heads, hd)
    v = v_ref[...].reshape(nm * heads, hd)
    s = lax.dot_general(q, k, (((1,), (1,)), ((), ())), preferred_element_type=F32) * scale
    row_head = lax.broadcasted_iota(jnp.int32, s.shape, 0) // t
    col_head = lax.broadcasted_iota(jnp.int32, s.shape, 1) % heads
    s = jnp.where(row_head == col_head, s, -jnp.inf)
    s = s - jnp.max(s, axis=-1, keepdims=True)
    e = jnp.exp(s)
    pr = e / jnp.sum(e, axis=-1, keepdims=True)
    o = jnp.dot(pr, v, preferred_element_type=F32)
    for h in range(heads):
        o_ref[:, h * hd:(h + 1) * hd] = o[h * t:(h + 1) * t, :]


def _attn_sample(q, mk, mv, layer):
    b, t, d = q.shape
    _, _, nm, heads, hd = mk.shape
    kv_spec = pl.BlockSpec((None, None, nm, heads, hd), lambda i: (layer, i, 0, 0, 0))
    return pl.pallas_call(
        functools.partial(_attn_sample_kernel, scale=hd ** -0.5, heads=heads, hd=hd),
        grid=(b,),
        in_specs=[pl.BlockSpec((None, t, d), lambda i: (i, 0, 0)), kv_spec, kv_spec],
        out_specs=pl.BlockSpec((None, t, d), lambda i: (i, 0, 0)),
        out_shape=jax.ShapeDtypeStruct((b, t, d), F32),
        compiler_params=_params("arbitrary"),
        name="attn_sample",
    )(q, mk, mv)


def _ffn_up_kernel(h_ref, hp_ref, wg_ref, wv_ref, cw_ref, cb_ref, st_ref, act_ref, zt_ref, zext_ref, rstd_ref,
                   *, halo, tm, shift, recompute_halo):
    @pl.when(pl.program_id(2) == 0)
    def _():
        rstd_ref[...] = _row_rstd(h_ref[...])

    h = h_ref[...]
    rstd = rstd_ref[...]
    z = jnp.dot(h, wg_ref[...], preferred_element_type=F32) * rstd
    val = jnp.dot(h, wv_ref[...], preferred_element_type=F32) * rstd
    zprev = st_ref[...]
    if recompute_halo:
        hp = hp_ref[...]
        zhalo = jnp.dot(hp, wg_ref[...], preferred_element_type=F32) * _row_rstd(hp)
        zprev = jnp.where(pl.program_id(1) == 0, zprev, zhalo)
    zext_ref[0:halo, :] = zprev
    zext_ref[halo:halo + tm, :] = z
    zm1 = zext_ref[halo - shift:halo - shift + tm, :]
    zm2 = zext_ref[halo - 2 * shift:halo - 2 * shift + tm, :]
    zc = cb_ref[...] + cw_ref[0:1, :] * zm2 + cw_ref[1:2, :] * zm1 + cw_ref[2:3, :] * z
    act_ref[...] = (zc * _sigmoid(zc) * val).astype(act_ref.dtype)
    zt_ref[...] = zext_ref[tm:tm + halo, :]


def _ffn_up(h, wg, wv, cw, cb, state, *, nseq, shift, halo, tm, tn):
    m, d = h.shape
    _, fp = wg.shape
    rows = m // nseq
    tm = _pick(rows, tm)
    nt = rows // tm
    tn = min(tn, fp)
    assert tm >= halo and (nt == 1 or halo == SUBLANE)
    hb = tm // SUBLANE
    return pl.pallas_call(
        functools.partial(_ffn_up_kernel, halo=halo, tm=tm, shift=shift, recompute_halo=nt > 1),
        grid=(nseq, nt, pl.cdiv(fp, tn)),
        in_specs=[pl.BlockSpec((tm, d), lambda s, i, j: (s * nt + i, 0)),
                  pl.BlockSpec((SUBLANE, d), lambda s, i, j: (jnp.maximum((s * nt + i) * hb - 1, 0), 0)),
                  pl.BlockSpec((d, tn), lambda s, i, j: (0, j)),
                  pl.BlockSpec((d, tn), lambda s, i, j: (0, j)),
                  pl.BlockSpec((CONV_W, tn), lambda s, i, j: (0, j)),
                  pl.BlockSpec((1, tn), lambda s, i, j: (0, j)),
                  pl.BlockSpec((None, halo, tn), lambda s, i, j: (s, 0, j))],
        out_specs=[pl.BlockSpec((tm, tn), lambda s, i, j: (s * nt + i, j)),
                   pl.BlockSpec((halo, tn), lambda s, i, j: (s * nt + i, j))],
        out_shape=[jax.ShapeDtypeStruct((m, fp), BF16),
                   jax.ShapeDtypeStruct((nseq * nt * halo, fp), F32)],
        scratch_shapes=[pltpu.VMEM((halo + tm, tn), F32), pltpu.VMEM((tm, 1), F32)],
        compiler_params=_params("arbitrary", "arbitrary", "arbitrary"),
        name="ffn_up",
    )(h, h, wg, wv, cw, cb, state)


def kernel(x_prompt, x_sample, mem_prompt, cache_mem_k, cache_mem_v, state_shift, state_wkv, state_pool,
           state_conv, g_mix, w_in, mu_shift, w0, b_w, a0, b_a, b_g, k_k, k_a, r_k, gn_w, gn_b, w_pool,
           pool_scale, w_out, g_mem, w_mk, w_mv, g_attn, w_mq, w_mo, g_ffn, w_gate, w_val, conv_w, conv_b,
           w_down, g_final):
    bp, tp, d = x_prompt.shape
    bs, ts, _ = x_sample.shape
    depth = w_in.shape[0]
    c = w0.shape[1]
    nh = c // HEAD_SIZE
    nvb = HEAD_SIZE // SUBLANE
    cp = pool_scale.shape[1]
    pw = mu_shift.shape[1]
    lora = pw - 3 * c
    lp = _round_up(lora, LANE)
    ca = 3 * c + lp
    dl, al, gl = b_w.shape[1], b_a.shape[1], b_g.shape[1]
    f = w_gate.shape[2]
    fp = f
    nmem = mem_prompt.shape[1]
    nbuf = state_pool.shape[2]
    mp, ms = bp * tp, bs * ts

    xp = x_prompt.reshape(mp, d)
    xs = x_sample.transpose(1, 0, 2).reshape(ms, d)

    outs = {n: [] for n in ("pm_k", "pm_v", "p_sh", "p_wkv", "p_pool", "p_conv",
                            "s_sh", "s_wkv", "s_pool", "s_conv")}

    for l in range(depth):
        w_in_t = w_in[l].T.astype(BF16)
        wu_t = w_in_t[pw:]
        mu = jnp.pad(mu_shift[l], (0, ca - pw)).reshape(1, ca)
        bw = jnp.zeros((lp, c), F32).at[0:dl].set(b_w[l]).astype(BF16)
        ba = jnp.zeros((lp, c), F32).at[dl:dl + al].set(b_a[l]).astype(BF16)
        bg = jnp.zeros((lp, c), F32).at[dl + al:dl + al + gl].set(b_g[l]).astype(BF16)
        w0l = w0[l].reshape(1, c)
        a0l = a0[l].reshape(1, c)
        wpool = w_pool[l].astype(BF16)
        pscale = pool_scale[l].reshape(1, cp)
        wmk = w_mk[l].astype(BF16)
        wmv = w_mv[l].astype(BF16)
        late = {}
        cw = conv_w[l]
        cb = conv_b[l].reshape(1, fp)

        def head_tile(p):
            return p.reshape(nh, HEAD_SIZE).T

        chain_p = [jnp.tile(head_tile(p), (1, bp)).reshape(1, HEAD_SIZE, bp * nh)
                   for p in (k_k[l], k_a[l], r_k[l].reshape(-1), gn_w[l], gn_b[l])]
        chain_s = [jnp.broadcast_to(p.reshape(nh, HEAD_SIZE, 1), (nh, HEAD_SIZE, bs))
                   for p in (k_k[l], k_a[l], r_k[l].reshape(-1), gn_w[l], gn_b[l])]

        mn = _rmsnorm(mem_prompt.reshape(bp * nmem, d), g_mem[l], BF16)
        mk = _matmul(mn, wmk, out_dtype=F32, name="mem_k")
        mv = _matmul(mn, wmv, out_dtype=F32, name="mem_v")
        outs["pm_k"].append(mk.reshape(bp, nmem, MEM_HEADS, d // MEM_HEADS))
        outs["pm_v"].append(mv.reshape(bp, nmem, MEM_HEADS, d // MEM_HEADS))

        def layer(x, *, prompt):
            m = x.shape[0]
            if prompt:
                nseq, shift, halo1, halo, pos0 = bp, 1, SUBLANE, SUBLANE, 0
                st_shift = jnp.zeros((bp, halo1, ca), F32)
                hist = jnp.zeros((bp, POOL_HIST, cp), F32)
                st_conv = jnp.zeros((bp, halo, fp), F32)
            else:
                nseq, shift, halo1, halo, pos0 = 1, bs, bs, (CONV_W - 1) * bs, PAST_LEN
                st_shift = jnp.pad(state_shift[l], ((0, 0), (0, ca - pw))).reshape(1, halo1, ca)
                hist = jnp.pad(state_pool[l].transpose(1, 0, 2), ((POOL_HIST - nbuf, 0), (0, 0), (0, 0)))
                hist = hist.reshape(1, POOL_HIST * bs, cp)
                st_conv = state_conv[l].transpose(1, 0, 2).reshape(1, halo, fp)

            xn = _rmsnorm(x, g_mix[l], BF16)
            proj = _matmul_cols2(xn, w_in_t, ca, wu_t, name="in_proj")
            tile = min(LANE, tp) if prompt else bs
            *rkvda, g = _rwkv_prep(proj, st_shift, mu, w0l, a0l, bw, ba, bg,
                                   nseq=nseq, shift=shift, halo=halo1, tt=tile, c=c, ca=ca)
            if prompt:
                chains = [_to_chain(x, nb=bp, nh=nh).reshape(1, tp, HEAD_SIZE, bp * nh) for x in rkvda]
                s0 = jnp.zeros((1, nvb, HEAD_SIZE, SUBLANE, bp * nh), F32)
                y, s_fin, cast_out = _wkv(*chains, s0, *chain_p, tt=16, group_major=True,
                                          cast=((w_out[l], None), (w_mq[l], g_attn[l]), (w_mo[l], None),
                                                (w_gate[l], g_ffn[l]), (w_val[l], g_ffn[l]),
                                                (w_down[l], None)))
                late.update(zip(("wo", "wmq", "wmo", "wg", "wv", "wd"), cast_out))
                r_out = _from_chain_gate(y.reshape(tp, HEAD_SIZE, bp * nh), g, nb=bp, nh=nh, tt=tile)
                new_wkv = s_fin.reshape(nvb, HEAD_SIZE, SUBLANE, bp, nh).transpose(3, 4, 0, 2, 1)
                new_wkv = new_wkv.reshape(bp, nh, HEAD_SIZE, HEAD_SIZE)
            else:
                chains = [x.reshape(ts, nh, HEAD_SIZE, bs) for x in rkvda]
                s0 = state_wkv[l].reshape(bs, nh, nvb, SUBLANE, HEAD_SIZE).transpose(1, 2, 4, 3, 0)
                y, s_fin, _ = _wkv(*chains, s0, *chain_s, tt=ts, group_major=False)
                r_out = _transpose_gate(y.reshape(ts, c, bs), g)
                new_wkv = s_fin.transpose(4, 0, 1, 3, 2).reshape(bs, nh, HEAD_SIZE, HEAD_SIZE)
            p_out = _pool(proj, ca, hist, wpool, pscale, nseq=nseq, shift=shift,
                          tt=256 if prompt else m, pos0=pos0)
            wmq, wmo, wg, wv, wd = (late[n] for n in ("wmq", "wmo", "wg", "wv", "wd"))
            x1, x1b = _matmul2_res(r_out, p_out, late["wo"], x)

            if prompt:
                q = _matmul(x1b, wmq, out_dtype=BF16, row_norm=True, name="attn_q")
                o = _attn_prompt(q, mk.reshape(bp, nmem, d), mv.reshape(bp, nmem, d),
                                 batch=bp, heads=MEM_HEADS)
            else:
                q = _matmul(x1b, wmq, out_dtype=F32, row_norm=True, name="attn_q")
                q = q.reshape(ts, bs, d).transpose(1, 0, 2)
                o = _attn_sample(q, cache_mem_k, cache_mem_v, l)
                o = o.transpose(1, 0, 2).reshape(m, d)
            x2, x2b = _matmul(o, wmo, out_dtype=F32, res=x1, emit_bf16=True, name="attn_o")

            act, ztail = _ffn_up(x2b, wg, wv, cw, cb, st_conv, nseq=nseq, shift=shift, halo=halo,
                                 tm=1024, tn=512)
            x3 = _matmul(act, wd, out_dtype=F32, res=x2, tm=512, tn=512, name="ffn_down")

            if prompt:
                new_shift = proj.reshape(bp, tp, ca + cp)[:, -1, :pw]
                new_pool = proj.reshape(bp, tp, ca + cp)[:, tp - nbuf:, ca:]
                new_conv = ztail.reshape(bp, -1, halo, fp)[:, -1, halo - (CONV_W - 1):, :f]
            else:
                new_shift = proj[(ts - 1) * bs:, :pw]
                u_b = proj[:, ca:].reshape(ts, bs, cp).transpose(1, 0, 2)
                new_pool = jnp.concatenate([state_pool[l], u_b], axis=1)[:, -nbuf:]
                new_conv = ztail.reshape(CONV_W - 1, bs, fp)[:, :, :f].transpose(1, 0, 2)
            return x3, new_shift, new_wkv, new_pool, new_conv

        xp, sh, wk, po, co = layer(xp, prompt=True)
        outs["p_sh"].append(sh)
        outs["p_wkv"].append(wk)
        outs["p_pool"].append(po)
        outs["p_conv"].append(co)
        xs, sh, wk, po, co = layer(xs, prompt=False)
        outs["s_sh"].append(sh)
        outs["s_wkv"].append(wk)
        outs["s_pool"].append(po)
        outs["s_conv"].append(co)

    y_prompt = _rmsnorm(xp, g_final, F32).reshape(bp, tp, d)
    y_sample = _rmsnorm(xs, g_final, F32).reshape(ts, bs, d).transpose(1, 0, 2)
    stk = lambda n: jnp.stack(outs[n])
    return (y_prompt, y_sample, stk("pm_k"), stk("pm_v"), stk("p_sh"), stk("p_wkv"), stk("p_pool"),
            stk("p_conv"), stk("s_sh"), stk("s_wkv"), stk("s_pool"), stk("s_conv"))
```

```python
import functools

import jax
import jax.numpy as jnp
from jax import lax
from jax.experimental import pallas as pl
from jax.experimental.pallas import tpu as pltpu

F32 = jnp.float32
BF16 = jnp.bfloat16

HEAD_SIZE = 64
POOL_WINDOWS = (2, 4, 8, 16)
POOL_HIST = 16
MEM_HEADS = 4
CONV_W = 3
PAST_LEN = 16384
NORM_EPS = 1e-6
GN_EPS = 64e-5
EXP_M_HALF = 0.6065306597126334
LANE = 128
SUBLANE = 8
VMEM_LIMIT = 56 * 1024 * 1024


def _pick(dim, pref, mult=SUBLANE):
    if dim <= pref:
        return dim
    t = (pref // mult) * mult
    while t >= mult:
        if dim % t == 0:
            return t
        t -= mult
    return dim


def _round_up(x, m):
    return (x + m - 1) // m * m


def _params(*sem):
    return pltpu.CompilerParams(dimension_semantics=sem, vmem_limit_bytes=VMEM_LIMIT)


def _sigmoid(x):
    return 1.0 / (1.0 + jnp.exp(-x))


def _rmsnorm_kernel(x_ref, g_ref, o_ref):
    x = x_ref[...].astype(F32)
    ms = jnp.mean(x * x, axis=-1, keepdims=True)
    o_ref[...] = (x * lax.rsqrt(ms + NORM_EPS) * g_ref[...]).astype(o_ref.dtype)


def _rmsnorm(x, g, out_dtype):
    m, d = x.shape
    tr = _pick(m, 256)
    return pl.pallas_call(
        _rmsnorm_kernel,
        grid=(m // tr,),
        in_specs=[pl.BlockSpec((tr, d), lambda i: (i, 0)),
                  pl.BlockSpec((1, d), lambda i: (0, 0))],
        out_specs=pl.BlockSpec((tr, d), lambda i: (i, 0)),
        out_shape=jax.ShapeDtypeStruct((m, d), out_dtype),
        compiler_params=_params("arbitrary"),
        name="rmsnorm",
    )(x, g.reshape(1, d).astype(F32))


def _row_rstd(x):
    xf = x.astype(F32)
    return lax.rsqrt(jnp.mean(xf * xf, axis=-1, keepdims=True) + NORM_EPS)


def _mm_kernel(*refs, has_res, emit_bf16, row_norm):
    refs = list(refs)
    x_ref, w_ref = refs[:2]
    del refs[:2]
    res_ref = refs.pop(0) if has_res else None
    o_ref = refs.pop(0)
    ob_ref = refs.pop(0) if emit_bf16 else None
    rstd_ref = refs.pop(0) if row_norm else None
    if row_norm:
        @pl.when(pl.program_id(1) == 0)
        def _():
            rstd_ref[...] = _row_rstd(x_ref[...])
    acc = jnp.dot(x_ref[...].astype(BF16), w_ref[...].astype(BF16), preferred_element_type=F32)
    if row_norm:
        acc = acc * rstd_ref[...]
    if has_res:
        acc = acc + res_ref[...]
    o_ref[...] = acc.astype(o_ref.dtype)
    if emit_bf16:
        ob_ref[...] = acc.astype(BF16)


def _matmul(x, w, *, out_dtype, res=None, emit_bf16=False, row_norm=False, tm=1024, tn=512, name="matmul"):
    m, kd = x.shape
    _, n = w.shape
    tm = _pick(m, tm)
    tn = _pick(n, tn, LANE)
    tile = pl.BlockSpec((tm, tn), lambda i, j: (i, j))
    in_specs = [pl.BlockSpec((tm, kd), lambda i, j: (i, 0)),
                pl.BlockSpec((kd, tn), lambda i, j: (0, j))]
    args = [x, w]
    if res is not None:
        in_specs.append(tile)
        args.append(res)
    out_specs, out_shape = [tile], [jax.ShapeDtypeStruct((m, n), out_dtype)]
    if emit_bf16:
        out_specs.append(tile)
        out_shape.append(jax.ShapeDtypeStruct((m, n), BF16))
    out = pl.pallas_call(
        functools.partial(_mm_kernel, has_res=res is not None, emit_bf16=emit_bf16, row_norm=row_norm),
        grid=(m // tm, n // tn),
        in_specs=in_specs,
        out_specs=out_specs,
        out_shape=out_shape,
        scratch_shapes=[pltpu.VMEM((tm, 1), F32)] if row_norm else [],
        compiler_params=_params("arbitrary", "arbitrary"),
        name=name,
    )(*args)
    return out if emit_bf16 else out[0]


def _mm_cols2_kernel(x_ref, wa_ref, wb_ref, o_ref, *, na):
    j = pl.program_id(1)
    contract_last = (((1,), (1,)), ((), ()))

    @pl.when(j < na)
    def _():
        o_ref[...] = lax.dot_general(x_ref[...], wa_ref[...], contract_last, preferred_element_type=F32)

    @pl.when(j >= na)
    def _():
        o_ref[...] = lax.dot_general(x_ref[...], wb_ref[...], contract_last, preferred_element_type=F32)


def _matmul_cols2(x, wa_t, na_cols, wb_t, *, tm=1024, tn=512, name="matmul"):
    m, kd = x.shape
    tm = _pick(m, tm)
    tn = _pick(na_cols, tn, LANE)
    assert na_cols % tn == 0 and wb_t.shape[0] % tn == 0
    na, nb = na_cols // tn, wb_t.shape[0] // tn
    return pl.pallas_call(
        functools.partial(_mm_cols2_kernel, na=na),
        grid=(m // tm, na + nb),
        in_specs=[pl.BlockSpec((tm, kd), lambda i, j: (i, 0)),
                  pl.BlockSpec((tn, kd), lambda i, j: (jnp.minimum(j, na - 1), 0)),
                  pl.BlockSpec((tn, kd), lambda i, j: (jnp.maximum(j - na, 0), 0))],
        out_specs=pl.BlockSpec((tm, tn), lambda i, j: (i, j)),
        out_shape=jax.ShapeDtypeStruct((m, (na + nb) * tn), F32),
        compiler_params=_params("arbitrary", "arbitrary"),
        name=name,
    )(x, wa_t, wb_t)


def _mm2_kernel(x1_ref, x2_ref, w1_ref, w2_ref, res_ref, o_ref, ob_ref):
    acc = jnp.dot(x1_ref[...], w1_ref[...], preferred_element_type=F32)
    acc = acc + jnp.dot(x2_ref[...], w2_ref[...], preferred_element_type=F32)
    acc = acc + res_ref[...]
    o_ref[...] = acc
    ob_ref[...] = acc.astype(BF16)


def _matmul2_res(x1, x2, w, res, *, tm=1024, tn=512):
    m, k1 = x1.shape
    _, k2 = x2.shape
    kw, n = w.shape
    assert k1 == k2 and kw == k1 + k2
    tm = _pick(m, tm)
    tn = _pick(n, tn, LANE)
    return pl.pallas_call(
        _mm2_kernel,
        grid=(m // tm, n // tn),
        in_specs=[pl.BlockSpec((tm, k1), lambda i, j: (i, 0)),
                  pl.BlockSpec((tm, k2), lambda i, j: (i, 0)),
                  pl.BlockSpec((k1, tn), lambda i, j: (0, j)),
                  pl.BlockSpec((k2, tn), lambda i, j: (1, j)),
                  pl.BlockSpec((tm, tn), lambda i, j: (i, j))],
        out_specs=[pl.BlockSpec((tm, tn), lambda i, j: (i, j))] * 2,
        out_shape=[jax.ShapeDtypeStruct((m, n), F32), jax.ShapeDtypeStruct((m, n), BF16)],
        compiler_params=_params("arbitrary", "arbitrary"),
        name="out_proj",
    )(x1, x2, w, w, res)


def _prep_kernel(p_ref, st_ref, mu_ref, w0_ref, a0_ref, bw_ref, ba_ref, bg_ref,
                 r_ref, k_ref, v_ref, d_ref, a_ref, g_ref, ext_ref, *, halo, tt, shift, c):
    @pl.when(pl.program_id(1) == 0)
    def _():
        ext_ref[0:halo, :] = st_ref[...]

    p = p_ref[...]
    ext_ref[halo:halo + tt, :] = p
    prev = ext_ref[halo - shift:halo - shift + tt, :]
    xs = p + (prev - p) * mu_ref[...]
    ext_ref[0:halo, :] = ext_ref[tt:tt + halo, :]

    r_ref[...] = xs[:, 0:c].T
    k_ref[...] = xs[:, c:2 * c].T
    v_ref[...] = xs[:, 2 * c:3 * c].T
    lo = xs[:, 3 * c:]
    wpre = jnp.dot(jnp.tanh(lo).astype(BF16), bw_ref[...], preferred_element_type=F32)
    apre = jnp.dot(lo.astype(BF16), ba_ref[...], preferred_element_type=F32)
    g_ref[...] = jnp.dot(_sigmoid(lo).astype(BF16), bg_ref[...], preferred_element_type=F32)
    ez = jnp.exp(-(w0_ref[...] + wpre))
    d_ref[...] = jnp.exp(-EXP_M_HALF / (1.0 + ez)).T
    a_ref[...] = _sigmoid(a0_ref[...] + apre).T


def _rwkv_prep(proj, state, mu, w0, a0, bw, ba, bg, *, nseq, shift, halo, tt, c, ca):
    m = proj.shape[0]
    rows = m // nseq
    assert rows % tt == 0
    nt = rows // tt
    row_spec = lambda width: pl.BlockSpec((tt, width), lambda s, j: (s * nt + j, 0))
    tr_spec = pl.BlockSpec((None, c, tt), lambda s, j: (s * nt + j, 0, 0))
    full = lambda arr: pl.BlockSpec(arr.shape, lambda s, j: (0,) * arr.ndim)
    out_t = jax.ShapeDtypeStruct((nseq * nt, c, tt), F32)
    return pl.pallas_call(
        functools.partial(_prep_kernel, halo=halo, tt=tt, shift=shift, c=c),
        grid=(nseq, nt),
        in_specs=[row_spec(ca),
                  pl.BlockSpec((None, halo, ca), lambda s, j: (s, 0, 0)),
                  full(mu), full(w0), full(a0), full(bw), full(ba), full(bg)],
        out_specs=[tr_spec] * 5 + [row_spec(c)],
        out_shape=[out_t] * 5 + [jax.ShapeDtypeStruct((m, c), F32)],
        scratch_shapes=[pltpu.VMEM((halo + tt, ca), F32)],
        compiler_params=_params("arbitrary", "arbitrary"),
        name="rwkv_prep",
    )(proj, state, mu, w0, a0, bw, ba, bg)


V_GROUP = 4
K_BLOCK = 32


def _wkv_kernel(*refs, tt, n, cast_steps, cast_gain):
    refs = list(refs)
    (r_ref, k_ref, v_ref, d_ref, a_ref, s0_ref, kk_ref, ka_ref, rk_ref, gw_ref, gb_ref) = refs[:11]
    del refs[:11]
    nc = len(cast_steps)
    w_refs = [refs.pop(0) for _ in range(nc)]
    wg_refs = [refs.pop(0) if has_gain else None for has_gain in cast_gain]
    o_ref, sf_ref = refs.pop(0), refs.pop(0)
    wb_refs = [refs.pop(0) for _ in range(nc)]
    s_ref, y_ref, row_ref, dot_ref, plast_ref = refs

    @pl.when(pl.program_id(1) == 0)
    def _():
        s_ref[...] = s0_ref[...]

    for w_ref, wg_ref, wb_ref, steps in zip(w_refs, wg_refs, wb_refs, cast_steps):
        @pl.when(pl.program_id(1) < steps)
        def _(w_ref=w_ref, wg_ref=wg_ref, wb_ref=wb_ref):
            w = w_ref[...] if wg_ref is None else w_ref[...] * wg_ref[...]
            wb_ref[...] = w.astype(wb_ref.dtype)

    k_k = kk_ref[...]
    k_a = ka_ref[...]
    r_k = rk_ref[...]
    gn_w = gw_ref[...]
    gn_b = gb_ref[...]
    inv_n = 1.0 / n
    lanes = s_ref.shape[-1]

    r = r_ref[...]
    k = k_ref[...]
    w = d_ref[...]
    a = a_ref[...]
    kk = k * k_k
    norm = jnp.sqrt(jnp.sum(kk * kk, axis=1, keepdims=True))
    kk = kk / jnp.maximum(norm, 1e-12)
    k2 = k * (1.0 + (a - 1.0) * k_a)
    bv = kk * a
    p = None
    for t in range(tt):
        row_ref[0, t] = -kk[t] if p is None else -kk[t] * p
        p = w[t] if p is None else p * w[t]
        inv_p = 1.0 / p
        row_ref[1, t] = r[t] * p
        row_ref[2, t] = bv[t] * inv_p
        row_ref[3, t] = k2[t] * inv_p
    plast_ref[...] = p
    dot_ref[0] = jnp.sum(bv * r, axis=1, keepdims=True)
    dot_ref[1] = jnp.sum(k2 * r, axis=1, keepdims=True)
    dot_ref[2] = jnp.sum(r * k2 * r_k, axis=1, keepdims=True)

    def token(t, carry):
        b_r = dot_ref[0, t]
        k_r = dot_ref[1, t]

        def vgroup(gidx, c2):
            vb0 = gidx * V_GROUP
            zero = jnp.zeros((SUBLANE, lanes), F32)

            def reduce_keys(kb, acc):
                sa_acc, q_acc = list(acc[0]), list(acc[1])
                for j in range(K_BLOCK):
                    kx = kb * K_BLOCK + j
                    a_row = row_ref[0, t, pl.ds(kx, 1), :]
                    q_row = row_ref[1, t, pl.ds(kx, 1), :]
                    for i in range(V_GROUP):
                        s_k = s_ref[vb0 + i, kx]
                        sa_acc[i] = sa_acc[i] + s_k * a_row
                        q_acc[i] = q_acc[i] + s_k * q_row
                return tuple(sa_acc), tuple(q_acc)

            sa8, q8 = lax.fori_loop(0, n // K_BLOCK, reduce_keys, ((zero,) * V_GROUP, (zero,) * V_GROUP))
            base = pl.multiple_of(vb0 * SUBLANE, SUBLANE)
            v8 = [v_ref[t, pl.ds(base + i * SUBLANE, SUBLANE), :] for i in range(V_GROUP)]

            def update_keys(kb, c3):
                for j in range(K_BLOCK):
                    kx = kb * K_BLOCK + j
                    b_row = row_ref[2, t, pl.ds(kx, 1), :]
                    k_row = row_ref[3, t, pl.ds(kx, 1), :]
                    for i in range(V_GROUP):
                        s_ref[vb0 + i, kx] = s_ref[vb0 + i, kx] + sa8[i] * b_row + v8[i] * k_row
                return c3

            lax.fori_loop(0, n // K_BLOCK, update_keys, 0)
            for i in range(V_GROUP):
                y_ref[t, pl.ds(base + i * SUBLANE, SUBLANE), :] = q8[i] + sa8[i] * b_r + v8[i] * k_r
            return c2

        lax.fori_loop(0, n // (SUBLANE * V_GROUP), vgroup, 0)
        return carry

    lax.fori_loop(0, tt, token, 0)

    def rescale(vb, carry):
        for kx in range(n):
            s_ref[vb, kx] = s_ref[vb, kx] * plast_ref[kx:kx + 1, :]
        return carry

    lax.fori_loop(0, n // SUBLANE, rescale, 0)

    y = y_ref[...]
    mean = jnp.sum(y, axis=1, keepdims=True) * inv_n
    yc = y - mean
    var = jnp.sum(yc * yc, axis=1, keepdims=True) * inv_n
    o_ref[...] = yc * lax.rsqrt(var + GN_EPS) * gn_w + gn_b + dot_ref[2] * v_ref[...]

    @pl.when(pl.program_id(1) == pl.num_programs(1) - 1)
    def _():
        sf_ref[...] = s_ref[...]


def _cast_rows(rows, nsteps):
    blk = _round_up(-(-rows // nsteps), 2 * SUBLANE)
    while rows % blk:
        blk += 2 * SUBLANE
    return blk, rows // blk


def _wkv(r, k, v, d, a, s0, k_k, k_a, r_k, gn_w, gn_b, *, tt, group_major, cast=()):
    if group_major:
        g, t, n, lanes = r.shape
    else:
        t, g, n, lanes = r.shape
    tt = _pick(t, tt, 1)
    nb = n // SUBLANE
    assert not cast or g == 1
    cast_specs, gain_specs, cast_steps = [], [], []
    for w, gain in cast:
        blk, steps = _cast_rows(w.shape[0], t // tt)
        cast_steps.append(steps)
        index = lambda i, j, last=steps - 1: (jnp.minimum(j, last), 0)
        cast_specs.append(pl.BlockSpec((blk, w.shape[1]), index))
        if gain is not None:
            gain_specs.append(pl.BlockSpec((blk, 1), index))
    weights = [w for w, _ in cast]
    gains = [gain.reshape(-1, 1).astype(F32) for _, gain in cast if gain is not None]
    cast_gain = tuple(gain is not None for _, gain in cast)
    if group_major:
        tok = pl.BlockSpec((None, tt, n, lanes), lambda i, j: (i, j, 0, 0))
    else:
        tok = pl.BlockSpec((tt, None, n, lanes), lambda i, j: (j, i, 0, 0))
    st = pl.BlockSpec((None, nb, n, SUBLANE, lanes), lambda i, j: (i, 0, 0, 0, 0))
    gp = k_k.shape[0]
    par = pl.BlockSpec((None, n, lanes), (lambda i, j: (i, 0, 0)) if gp > 1 else (lambda i, j: (0, 0, 0)))
    y, s_fin, *cast_out = pl.pallas_call(
        functools.partial(_wkv_kernel, tt=tt, n=n, cast_steps=tuple(cast_steps), cast_gain=cast_gain),
        grid=(g, t // tt),
        in_specs=[tok] * 5 + [st] + [par] * 5 + cast_specs + gain_specs,
        out_specs=[tok, st] + cast_specs,
        out_shape=[jax.ShapeDtypeStruct(r.shape, F32),
                   jax.ShapeDtypeStruct((g, nb, n, SUBLANE, lanes), F32)]
                  + [jax.ShapeDtypeStruct(w.shape, BF16) for w in weights],
        scratch_shapes=[pltpu.VMEM((nb, n, SUBLANE, lanes), F32), pltpu.VMEM((tt, n, lanes), F32),
                        pltpu.VMEM((4, tt, n, lanes), F32), pltpu.VMEM((3, tt, 1, lanes), F32),
                        pltpu.VMEM((n, lanes), F32)],
        compiler_params=_params("arbitrary", "arbitrary"),
        name="wkv_recurrence",
    )(r, k, v, d, a, s0, k_k, k_a, r_k, gn_w, gn_b, *weights, *gains)
    return y, s_fin, cast_out


def _to_chain_kernel(x_ref, o_ref, *, nb, nh, n, tt):
    for kb in range(n // SUBLANE):
        ks = slice(kb * SUBLANE, (kb + 1) * SUBLANE)
        per_key = jnp.concatenate([jnp.swapaxes(x_ref[b, :, ks, :], 0, 1) for b in range(nb)], axis=1)
        chains_last = jnp.swapaxes(per_key, 1, 2)
        o_ref[:, ks, :] = jnp.swapaxes(chains_last, 0, 1)


def _to_chain(xt, *, nb, nh):
    tiles, c, tt = xt.shape
    nt = tiles // nb
    n = c // nh
    return pl.pallas_call(
        functools.partial(_to_chain_kernel, nb=nb, nh=nh, n=n, tt=tt),
        grid=(nt,),
        in_specs=[pl.BlockSpec((nb, None, nh, n, tt), lambda j: (0, j, 0, 0, 0))],
        out_specs=pl.BlockSpec((tt, n, nb * nh), lambda j: (j, 0, 0)),
        out_shape=jax.ShapeDtypeStruct((nt * tt, n, nb * nh), F32),
        compiler_params=_params("arbitrary"),
        name="to_chain",
    )(xt.reshape(nb, nt, nh, n, tt))


def _from_chain_gate_kernel(y_ref, g_ref, o_ref, xt_ref, *, nb, nh, n, tt):
    for vb in range(n // SUBLANE):
        vs = slice(vb * SUBLANE, (vb + 1) * SUBLANE)
        chains_mid = jnp.swapaxes(jnp.swapaxes(y_ref[:, vs, :], 0, 1), 1, 2)
        for b in range(nb):
            xt_ref[b, :, vs, :] = jnp.swapaxes(chains_mid[:, b * nh:(b + 1) * nh, :], 0, 1)
    for b in range(nb):
        o_ref[b] = (xt_ref[b].reshape(nh * n, tt).T * g_ref[b]).astype(o_ref.dtype)


def _from_chain_gate(y, g, *, nb, nh, tt):
    m, c = g.shape
    t = m // nb
    n = c // nh
    nt = t // tt
    row = pl.BlockSpec((nb, tt, c), lambda j: (0, j, 0))
    out = pl.pallas_call(
        functools.partial(_from_chain_gate_kernel, nb=nb, nh=nh, n=n, tt=tt),
        grid=(nt,),
        in_specs=[pl.BlockSpec((tt, n, nb * nh), lambda j: (j, 0, 0)), row],
        out_specs=row,
        out_shape=jax.ShapeDtypeStruct((nb, t, c), BF16),
        scratch_shapes=[pltpu.VMEM((nb, nh, n, tt), F32)],
        compiler_params=_params("arbitrary"),
        name="from_chain_gate",
    )(y, g.reshape(nb, t, c))
    return out.reshape(m, c)


def _transpose_gate_kernel(y_ref, g_ref, o_ref):
    o_ref[...] = (y_ref[...].T * g_ref[...]).astype(o_ref.dtype)


def _transpose_gate(yt, g):
    tiles, c, tt = yt.shape
    return pl.pallas_call(
        _transpose_gate_kernel,
        grid=(tiles,),
        in_specs=[pl.BlockSpec((None, c, tt), lambda i: (i, 0, 0)),
                  pl.BlockSpec((tt, c), lambda i: (i, 0))],
        out_specs=pl.BlockSpec((tt, c), lambda i: (i, 0)),
        out_shape=jax.ShapeDtypeStruct((tiles * tt, c), BF16),
        compiler_params=_params("arbitrary"),
        name="transpose_gate",
    )(yt, g)


def _pool_kernel(*refs, gps, halo, tt, shift, pos0):
    u_refs = refs[:gps]
    h_ref, w_ref, sc_ref, o_ref, ext_ref = refs[gps:]
    gw = u_refs[0].shape[1]
    gb = pl.program_id(1)
    j = pl.program_id(2)

    @pl.when(j == 0)
    def _():
        ext_ref[0:halo, :] = h_ref[...]

    row = lax.broadcasted_iota(jnp.int32, (tt, 1), 0) + j * tt
    time = row if shift == 1 else row // shift

    def group(g, win):
        cols = slice(g * gw, (g + 1) * gw)
        x = u_refs[g][...]
        acc = x
        for back in range(1, win):
            lo = halo - back * shift
            acc = acc + ext_ref[lo:lo + tt, cols]
        cnt = jnp.minimum(pos0 + time + 1, win).astype(F32)
        d = acc / cnt - x
        y = jnp.dot(d.astype(BF16), w_ref[g], preferred_element_type=F32)
        o_ref[:, cols] = (y * sc_ref[:, cols]).astype(o_ref.dtype)

    for g in range(gps):
        ext_ref[halo:halo + tt, g * gw:(g + 1) * gw] = u_refs[g][...]
    if gps == len(POOL_WINDOWS):
        for g, win in enumerate(POOL_WINDOWS):
            group(g, win)
    else:
        for g, win in enumerate(POOL_WINDOWS):
            pl.when(gb == g)(functools.partial(group, 0, win))

    if tt >= halo:
        ext_ref[0:halo, :] = ext_ref[tt:tt + halo, :]


def _pool(proj, col0, hist, w_pool, scale, *, nseq, shift, tt, pos0, gps):
    m = proj.shape[0]
    cp = scale.shape[1]
    ng = len(POOL_WINDOWS)
    assert gps in (1, ng)
    gw = cp // ng
    assert col0 % gw == 0
    g0 = col0 // gw
    rows = m // nseq
    tt = _pick(rows, tt)
    nt = rows // tt
    halo = POOL_HIST * shift
    assert nt == 1 or tt >= halo
    u_specs = [pl.BlockSpec((tt, gw), lambda s, gb, j, g=g: (s * nt + j, g0 + gb * gps + g)) for g in range(gps)]
    return pl.pallas_call(
        functools.partial(_pool_kernel, gps=gps, halo=halo, tt=tt, shift=shift, pos0=pos0),
        grid=(nseq, ng // gps, nt),
        in_specs=u_specs + [pl.BlockSpec((None, halo, gps * gw), lambda s, gb, j: (s, 0, gb)),
                            pl.BlockSpec((gps, gw, gw), lambda s, gb, j: (gb, 0, 0)),
                            pl.BlockSpec((1, gps * gw), lambda s, gb, j: (0, gb))],
        out_specs=pl.BlockSpec((tt, gps * gw), lambda s, gb, j: (s * nt + j, gb)),
        out_shape=jax.ShapeDtypeStruct((m, cp), BF16),
        scratch_shapes=[pltpu.VMEM((halo + tt, gps * gw), F32)],
        compiler_params=_params("arbitrary", "arbitrary", "arbitrary"),
        name="pool_mix",
    )(*([proj] * gps), hist, w_pool, scale)


def _attend(q, k, v, scale):
    s = lax.dot_general(q, k, (((1,), (1,)), ((), ())), preferred_element_type=F32) * scale
    s = s - jnp.max(s, axis=-1, keepdims=True)
    e = jnp.exp(s)
    pr = e / jnp.sum(e, axis=-1, keepdims=True)
    return jnp.dot(pr.astype(BF16), v, preferred_element_type=F32)


def _attn_prompt_kernel(q_ref, k_ref, v_ref, o_ref, *, scale, heads, hd):
    for h in range(heads):
        cols = slice(h * hd, (h + 1) * hd)
        o = _attend(q_ref[:, cols], k_ref[:, cols].astype(BF16), v_ref[:, cols].astype(BF16), scale)
        o_ref[:, cols] = o.astype(o_ref.dtype)


def _attn_prompt(q, mk, mv, *, batch, heads):
    m, d = q.shape
    t = m // batch
    nm = mk.shape[1]
    hd = d // heads
    tq = _pick(t, 512)
    nq = t // tq
    return pl.pallas_call(
        functools.partial(_attn_prompt_kernel, scale=hd ** -0.5, heads=heads, hd=hd),
        grid=(batch, nq),
        in_specs=[pl.BlockSpec((tq, d), lambda b, i: (b * nq + i, 0)),
                  pl.BlockSpec((None, nm, d), lambda b, i: (b, 0, 0)),
                  pl.BlockSpec((None, nm, d), lambda b, i: (b, 0, 0))],
        out_specs=pl.BlockSpec((tq, d), lambda b, i: (b * nq + i, 0)),
        out_shape=jax.ShapeDtypeStruct((m, d), BF16),
        compiler_params=_params("arbitrary", "arbitrary"),
        name="attn_prompt",
    )(q, mk, mv)


def _attn_sample_kernel(q_ref, k_ref, v_ref, o_ref, *, scale, heads, hd):
    t = q_ref.shape[0]
    nm = k_ref.shape[0]
    q = jnp.concatenate([q_ref[:, h * hd:(h + 1) * hd] for h in range(heads)], axis=0)
    k = k_ref[...].reshape(nm * heads, hd)
    v = v_ref[...].reshape(nm * heads, hd)
    s = lax.dot_general(q, k, (((1,), (1,)), ((), ())), preferred_element_type=F32) * scale
    row_head = lax.broadcasted_iota(jnp.int32, s.shape, 0) // t
    col_head = lax.broadcasted_iota(jnp.int32, s.shape, 1) % heads
    s = jnp.where(row_head == col_head, s, -jnp.inf)
    s = s - jnp.max(s, axis=-1, keepdims=True)
    e = jnp.exp(s)
    pr = e / jnp.sum(e, axis=-1, keepdims=True)
    o = jnp.dot(pr, v, preferred_element_type=F32)
    for h in range(heads):
        o_ref[:, h * hd:(h + 1) * hd] = o[h * t:(h + 1) * t, :]


def _attn_sample(q, mk, mv, layer):
    b, t, d = q.shape
    _, _, nm, heads, hd = mk.shape
    kv_spec = pl.BlockSpec((None, None, nm, heads, hd), lambda i: (layer, i, 0, 0, 0))
    return pl.pallas_call(
        functools.partial(_attn_sample_kernel, scale=hd ** -0.5, heads=heads, hd=hd),
        grid=(b,),
        in_specs=[pl.BlockSpec((None, t, d), lambda i: (i, 0, 0)), kv_spec, kv_spec],
        out_specs=pl.BlockSpec((None, t, d), lambda i: (i, 0, 0)),
        out_shape=jax.ShapeDtypeStruct((b, t, d), F32),
        compiler_params=_params("arbitrary"),
        name="attn_sample",
    )(q, mk, mv)


def _ffn_up_kernel(h_ref, hp_ref, wg_ref, wv_ref, cw_ref, cb_ref, st_ref, act_ref, zt_ref, zext_ref, rstd_ref,
                   *, halo, tm, shift, recompute_halo):
    @pl.when(pl.program_id(2) == 0)
    def _():
        rstd_ref[...] = _row_rstd(h_ref[...])

    h = h_ref[...]
    rstd = rstd_ref[...]
    z = jnp.dot(h, wg_ref[...], preferred_element_type=F32) * rstd
    val = jnp.dot(h, wv_ref[...], preferred_element_type=F32) * rstd
    zprev = st_ref[...]
    if recompute_halo:
        hp = hp_ref[...]
        zhalo = jnp.dot(hp, wg_ref[...], preferred_element_type=F32) * _row_rstd(hp)
        zprev = jnp.where(pl.program_id(1) == 0, zprev, zhalo)
    zext_ref[0:halo, :] = zprev
    zext_ref[halo:halo + tm, :] = z
    zm1 = zext_ref[halo - shift:halo - shift + tm, :]
    zm2 = zext_ref[halo - 2 * shift:halo - 2 * shift + tm, :]
    zc = cb_ref[...] + cw_ref[0:1, :] * zm2 + cw_ref[1:2, :] * zm1 + cw_ref[2:3, :] * z
    act_ref[...] = (zc * _sigmoid(zc) * val).astype(act_ref.dtype)
    zt_ref[...] = zext_ref[tm:tm + halo, :]


def _ffn_up(h, wg, wv, cw, cb, state, *, nseq, shift, halo, tm, tn):
    m, d = h.shape
    _, fp = wg.shape
    rows = m // nseq
    tm = _pick(rows, tm)
    nt = rows // tm
    tn = min(tn, fp)
    assert tm >= halo and (nt == 1 or halo == SUBLANE)
    hb = tm // SUBLANE
    return pl.pallas_call(
        functools.partial(_ffn_up_kernel, halo=halo, tm=tm, shift=shift, recompute_halo=nt > 1),
        grid=(nseq, nt, pl.cdiv(fp, tn)),
        in_specs=[pl.BlockSpec((tm, d), lambda s, i, j: (s * nt + i, 0)),
                  pl.BlockSpec((SUBLANE, d), lambda s, i, j: (jnp.maximum((s * nt + i) * hb - 1, 0), 0)),
                  pl.BlockSpec((d, tn), lambda s, i, j: (0, j)),
                  pl.BlockSpec((d, tn), lambda s, i, j: (0, j)),
                  pl.BlockSpec((CONV_W, tn), lambda s, i, j: (0, j)),
                  pl.BlockSpec((1, tn), lambda s, i, j: (0, j)),
                  pl.BlockSpec((None, halo, tn), lambda s, i, j: (s, 0, j))],
        out_specs=[pl.BlockSpec((tm, tn), lambda s, i, j: (s * nt + i, j)),
                   pl.BlockSpec((halo, tn), lambda s, i, j: (s * nt + i, j))],
        out_shape=[jax.ShapeDtypeStruct((m, fp), BF16),
                   jax.ShapeDtypeStruct((nseq * nt * halo, fp), F32)],
        scratch_shapes=[pltpu.VMEM((halo + tm, tn), F32), pltpu.VMEM((tm, 1), F32)],
        compiler_params=_params("arbitrary", "arbitrary", "arbitrary"),
        name="ffn_up",
    )(h, h, wg, wv, cw, cb, state)


def kernel(x_prompt, x_sample, mem_prompt, cache_mem_k, cache_mem_v, state_shift, state_wkv, state_pool,
           state_conv, g_mix, w_in, mu_shift, w0, b_w, a0, b_a, b_g, k_k, k_a, r_k, gn_w, gn_b, w_pool,
           pool_scale, w_out, g_mem, w_mk, w_mv, g_attn, w_mq, w_mo, g_ffn, w_gate, w_val, conv_w, conv_b,
           w_down, g_final):
    bp, tp, d = x_prompt.shape
    bs, ts, _ = x_sample.shape
    depth = w_in.shape[0]
    c = w0.shape[1]
    nh = c // HEAD_SIZE
    nvb = HEAD_SIZE // SUBLANE
    cp = pool_scale.shape[1]
    pw = mu_shift.shape[1]
    lora = pw - 3 * c
    lp = _round_up(lora, LANE)
    ca = 3 * c + lp
    dl, al, gl = b_w.shape[1], b_a.shape[1], b_g.shape[1]
    f = w_gate.shape[2]
    fp = f
    nmem = mem_prompt.shape[1]
    nbuf = state_pool.shape[2]
    mp, ms = bp * tp, bs * ts

    xp = x_prompt.reshape(mp, d)
    xs = x_sample.transpose(1, 0, 2).reshape(ms, d)

    outs = {n: [] for n in ("pm_k", "pm_v", "p_sh", "p_wkv", "p_pool", "p_conv",
                            "s_sh", "s_wkv", "s_pool", "s_conv")}

    for l in range(depth):
        w_in_t = w_in[l].T.astype(BF16)
        wu_t = w_in_t[pw:]
        mu = jnp.pad(mu_shift[l], (0, ca - pw)).reshape(1, ca)
        bw = jnp.zeros((lp, c), F32).at[0:dl].set(b_w[l]).astype(BF16)
        ba = jnp.zeros((lp, c), F32).at[dl:dl + al].set(b_a[l]).astype(BF16)
        bg = jnp.zeros((lp, c), F32).at[dl + al:dl + al + gl].set(b_g[l]).astype(BF16)
        w0l = w0[l].reshape(1, c)
        a0l = a0[l].reshape(1, c)
        wpool = w_pool[l].astype(BF16)
        pscale = pool_scale[l].reshape(1, cp)
        late = {}
        cw = conv_w[l]
        cb = conv_b[l].reshape(1, fp)

        def head_tile(p):
            return p.reshape(nh, HEAD_SIZE).T

        chain_p = [jnp.tile(head_tile(p), (1, bp)).reshape(1, HEAD_SIZE, bp * nh)
                   for p in (k_k[l], k_a[l], r_k[l].reshape(-1), gn_w[l], gn_b[l])]
        chain_s = [jnp.broadcast_to(p.reshape(nh, HEAD_SIZE, 1), (nh, HEAD_SIZE, bs))
                   for p in (k_k[l], k_a[l], r_k[l].reshape(-1), gn_w[l], gn_b[l])]

        mn = _rmsnorm(mem_prompt.reshape(bp * nmem, d), g_mem[l], BF16)
        mk = _matmul(mn, w_mk[l], out_dtype=F32, name="mem_k")
        mv = _matmul(mn, w_mv[l], out_dtype=F32, name="mem_v")
        outs["pm_k"].append(mk.reshape(bp, nmem, MEM_HEADS, d // MEM_HEADS))
        outs["pm_v"].append(mv.reshape(bp, nmem, MEM_HEADS, d // MEM_HEADS))

        def layer(x, *, prompt):
            m = x.shape[0]
            if prompt:
                nseq, shift, halo1, halo, pos0 = bp, 1, SUBLANE, SUBLANE, 0
                st_shift = jnp.zeros((bp, halo1, ca), F32)
                hist = jnp.zeros((bp, POOL_HIST, cp), F32)
                st_conv = jnp.zeros((bp, halo, fp), F32)
            else:
                nseq, shift, halo1, halo, pos0 = 1, bs, bs, (CONV_W - 1) * bs, PAST_LEN
                st_shift = jnp.pad(state_shift[l], ((0, 0), (0, ca - pw))).reshape(1, halo1, ca)
                hist = jnp.pad(state_pool[l].transpose(1, 0, 2), ((POOL_HIST - nbuf, 0), (0, 0), (0, 0)))
                hist = hist.reshape(1, POOL_HIST * bs, cp)
                st_conv = state_conv[l].transpose(1, 0, 2).reshape(1, halo, fp)

            xn = _rmsnorm(x, g_mix[l], BF16)
            proj = _matmul_cols2(xn, w_in_t, ca, wu_t, name="in_proj")
            tile = min(LANE, tp) if prompt else bs
            *rkvda, g = _rwkv_prep(proj, st_shift, mu, w0l, a0l, bw, ba, bg,
                                   nseq=nseq, shift=shift, halo=halo1, tt=tile, c=c, ca=ca)
            if prompt:
                chains = [_to_chain(x, nb=bp, nh=nh).reshape(1, tp, HEAD_SIZE, bp * nh) for x in rkvda]
                s0 = jnp.zeros((1, nvb, HEAD_SIZE, SUBLANE, bp * nh), F32)
                y, s_fin, cast_out = _wkv(*chains, s0, *chain_p, tt=16, group_major=True,
                                          cast=((w_out[l], None), (w_mq[l], g_attn[l]), (w_mo[l], None),
                                                (w_gate[l], g_ffn[l]), (w_val[l], g_ffn[l]),
                                                (w_down[l], None)))
                late.update(zip(("wo", "wmq", "wmo", "wg", "wv", "wd"), cast_out))
                r_out = _from_chain_gate(y.reshape(tp, HEAD_SIZE, bp * nh), g, nb=bp, nh=nh, tt=tile)
                new_wkv = s_fin.reshape(nvb, HEAD_SIZE, SUBLANE, bp, nh).transpose(3, 4, 0, 2, 1)
                new_wkv = new_wkv.reshape(bp, nh, HEAD_SIZE, HEAD_SIZE)
            else:
                chains = [x.reshape(ts, nh, HEAD_SIZE, bs) for x in rkvda]
                s0 = state_wkv[l].reshape(bs, nh, nvb, SUBLANE, HEAD_SIZE).transpose(1, 2, 4, 3, 0)
                y, s_fin, _ = _wkv(*chains, s0, *chain_s, tt=ts, group_major=False)
                r_out = _transpose_gate(y.reshape(ts, c, bs), g)
                new_wkv = s_fin.transpose(4, 0, 1, 3, 2).reshape(bs, nh, HEAD_SIZE, HEAD_SIZE)
            p_out = _pool(proj, ca, hist, wpool, pscale, nseq=nseq, shift=shift,
                          tt=256 if prompt else m, pos0=pos0, gps=len(POOL_WINDOWS) if prompt else 1)
            wmq, wmo, wg, wv, wd = (late[n] for n in ("wmq", "wmo", "wg", "wv", "wd"))
            x1, x1b = _matmul2_res(r_out, p_out, late["wo"], x)

            if prompt:
                q = _matmul(x1b, wmq, out_dtype=BF16, row_norm=True, name="attn_q")
                o = _attn_prompt(q, mk.reshape(bp, nmem, d), mv.reshape(bp, nmem, d),
                                 batch=bp, heads=MEM_HEADS)
            else:
                q = _matmul(x1b, wmq, out_dtype=F32, row_norm=True, name="attn_q")
                q = q.reshape(ts, bs, d).transpose(1, 0, 2)
                o = _attn_sample(q, cache_mem_k, cache_mem_v, l)
                o = o.transpose(1, 0, 2).reshape(m, d)
            x2, x2b = _matmul(o, wmo, out_dtype=F32, res=x1, emit_bf16=True, name="attn_o")

            act, ztail = _ffn_up(x2b, wg, wv, cw, cb, st_conv, nseq=nseq, shift=shift, halo=halo,
                                 tm=1024, tn=512)
            x3 = _matmul(act, wd, out_dtype=F32, res=x2, tm=512, tn=512, name="ffn_down")

            if prompt:
                new_shift = proj.reshape(bp, tp, ca + cp)[:, -1, :pw]
                new_pool = proj.reshape(bp, tp, ca + cp)[:, tp - nbuf:, ca:]
                new_conv = ztail.reshape(bp, -1, halo, fp)[:, -1, halo - (CONV_W - 1):, :f]
            else:
                new_shift = proj[(ts - 1) * bs:, :pw]
                u_b = proj[:, ca:].reshape(ts, bs, cp).transpose(1, 0, 2)
                new_pool = jnp.concatenate([state_pool[l], u_b], axis=1)[:, -nbuf:]
                new_conv = ztail.reshape(CONV_W - 1, bs, fp)[:, :, :f].transpose(1, 0, 2)
            return x3, new_shift, new_wkv, new_pool, new_conv

        xp, sh, wk, po, co = layer(xp, prompt=True)
        outs["p_sh"].append(sh)
        outs["p_wkv"].append(wk)
        outs["p_pool"].append(po)
        outs["p_conv"].append(co)
        xs, sh, wk, po, co = layer(xs, prompt=False)
        outs["s_sh"].append(sh)
        outs["s_wkv"].append(wk)
        outs["s_pool"].append(po)
        outs["s_conv"].append(co)

    y_prompt = _rmsnorm(xp, g_final, F32).reshape(bp, tp, d)
    y_sample = _rmsnorm(xs, g_final, F32).reshape(ts, bs, d).transpose(1, 0, 2)
    stk = lambda n: jnp.stack(outs[n])
    return (y_prompt, y_sample, stk("pm_k"), stk("pm_v"), stk("p_sh"), stk("p_wkv"), stk("p_pool"),
            stk("p_conv"), stk("s_sh"), stk("s_wkv"), stk("s_pool"), stk("s_conv"))
```

```python
import functools

import jax
import jax.numpy as jnp
from jax import lax
from jax.experimental import pallas as pl
from jax.experimental.pallas import tpu as pltpu

F32 = jnp.float32
BF16 = jnp.bfloat16

HEAD_SIZE = 64
POOL_WINDOWS = (2, 4, 8, 16)
POOL_HIST = 16
MEM_HEADS = 4
CONV_W = 3
PAST_LEN = 16384
NORM_EPS = 1e-6
GN_EPS = 64e-5
EXP_M_HALF = 0.6065306597126334
LANE = 128
SUBLANE = 8
VMEM_LIMIT = 56 * 1024 * 1024


def _pick(dim, pref, mult=SUBLANE):
    if dim <= pref:
        return dim
    t = (pref // mult) * mult
    while t >= mult:
        if dim % t == 0:
            return t
        t -= mult
    return dim


def _round_up(x, m):
    return (x + m - 1) // m * m


def _params(*sem):
    return pltpu.CompilerParams(dimension_semantics=sem, vmem_limit_bytes=VMEM_LIMIT)


def _sigmoid(x):
    return 1.0 / (1.0 + jnp.exp(-x))


def _rmsnorm_kernel(x_ref, g_ref, o_ref):
    x = x_ref[...].astype(F32)
    ms = jnp.mean(x * x, axis=-1, keepdims=True)
    o_ref[...] = (x * lax.rsqrt(ms + NORM_EPS) * g_ref[...]).astype(o_ref.dtype)


def _rmsnorm(x, g, out_dtype):
    m, d = x.shape
    tr = _pick(m, 256)
    return pl.pallas_call(
        _rmsnorm_kernel,
        grid=(m // tr,),
        in_specs=[pl.BlockSpec((tr, d), lambda i: (i, 0)),
                  pl.BlockSpec((1, d), lambda i: (0, 0))],
        out_specs=pl.BlockSpec((tr, d), lambda i: (i, 0)),
        out_shape=jax.ShapeDtypeStruct((m, d), out_dtype),
        compiler_params=_params("arbitrary"),
        name="rmsnorm",
    )(x, g.reshape(1, d).astype(F32))


def _row_rstd(x):
    xf = x.astype(F32)
    return lax.rsqrt(jnp.mean(xf * xf, axis=-1, keepdims=True) + NORM_EPS)


def _mm_kernel(*refs, has_res, emit_bf16, row_norm):
    refs = list(refs)
    x_ref, w_ref = refs[:2]
    del refs[:2]
    res_ref = refs.pop(0) if has_res else None
    o_ref = refs.pop(0)
    ob_ref = refs.pop(0) if emit_bf16 else None
    rstd_ref = refs.pop(0) if row_norm else None
    if row_norm:
        @pl.when(pl.program_id(1) == 0)
        def _():
            rstd_ref[...] = _row_rstd(x_ref[...])
    acc = jnp.dot(x_ref[...].astype(BF16), w_ref[...].astype(BF16), preferred_element_type=F32)
    if row_norm:
        acc = acc * rstd_ref[...]
    if has_res:
        acc = acc + res_ref[...]
    o_ref[...] = acc.astype(o_ref.dtype)
    if emit_bf16:
        ob_ref[...] = acc.astype(BF16)


def _matmul(x, w, *, out_dtype, res=None, emit_bf16=False, row_norm=False, tm=1024, tn=512, name="matmul"):
    m, kd = x.shape
    _, n = w.shape
    tm = _pick(m, tm)
    tn = _pick(n, tn, LANE)
    tile = pl.BlockSpec((tm, tn), lambda i, j: (i, j))
    in_specs = [pl.BlockSpec((tm, kd), lambda i, j: (i, 0)),
                pl.BlockSpec((kd, tn), lambda i, j: (0, j))]
    args = [x, w]
    if res is not None:
        in_specs.append(tile)
        args.append(res)
    out_specs, out_shape = [tile], [jax.ShapeDtypeStruct((m, n), out_dtype)]
    if emit_bf16:
        out_specs.append(tile)
        out_shape.append(jax.ShapeDtypeStruct((m, n), BF16))
    out = pl.pallas_call(
        functools.partial(_mm_kernel, has_res=res is not None, emit_bf16=emit_bf16, row_norm=row_norm),
        grid=(m // tm, n // tn),
        in_specs=in_specs,
        out_specs=out_specs,
        out_shape=out_shape,
        scratch_shapes=[pltpu.VMEM((tm, 1), F32)] if row_norm else [],
        compiler_params=_params("arbitrary", "arbitrary"),
        name=name,
    )(*args)
    return out if emit_bf16 else out[0]


def _mm_cols2_kernel(x_ref, wa_ref, wb_ref, o_ref, *, na):
    j = pl.program_id(1)
    contract_last = (((1,), (1,)), ((), ()))

    @pl.when(j < na)
    def _():
        o_ref[...] = lax.dot_general(x_ref[...], wa_ref[...], contract_last, preferred_element_type=F32)

    @pl.when(j >= na)
    def _():
        o_ref[...] = lax.dot_general(x_ref[...], wb_ref[...], contract_last, preferred_element_type=F32)


def _matmul_cols2(x, wa_t, na_cols, wb_t, *, tm=1024, tn=512, name="matmul"):
    m, kd = x.shape
    tm = _pick(m, tm)
    tn = _pick(na_cols, tn, LANE)
    assert na_cols % tn == 0 and wb_t.shape[0] % tn == 0
    na, nb = na_cols // tn, wb_t.shape[0] // tn
    return pl.pallas_call(
        functools.partial(_mm_cols2_kernel, na=na),
        grid=(m // tm, na + nb),
        in_specs=[pl.BlockSpec((tm, kd), lambda i, j: (i, 0)),
                  pl.BlockSpec((tn, kd), lambda i, j: (jnp.minimum(j, na - 1), 0)),
                  pl.BlockSpec((tn, kd), lambda i, j: (jnp.maximum(j - na, 0), 0))],
        out_specs=pl.BlockSpec((tm, tn), lambda i, j: (i, j)),
        out_shape=jax.ShapeDtypeStruct((m, (na + nb) * tn), F32),
        compiler_params=_params("arbitrary", "arbitrary"),
        name=name,
    )(x, wa_t, wb_t)


def _mm2_kernel(x1_ref, x2_ref, w1_ref, w2_ref, res_ref, o_ref, ob_ref):
    acc = jnp.dot(x1_ref[...], w1_ref[...], preferred_element_type=F32)
    acc = acc + jnp.dot(x2_ref[...], w2_ref[...], preferred_element_type=F32)
    acc = acc + res_ref[...]
    o_ref[...] = acc
    ob_ref[...] = acc.astype(BF16)


def _matmul2_res(x1, x2, w, res, *, tm=1024, tn=512):
    m, k1 = x1.shape
    _, k2 = x2.shape
    kw, n = w.shape
    assert k1 == k2 and kw == k1 + k2
    tm = _pick(m, tm)
    tn = _pick(n, tn, LANE)
    return pl.pallas_call(
        _mm2_kernel,
        grid=(m // tm, n // tn),
        in_specs=[pl.BlockSpec((tm, k1), lambda i, j: (i, 0)),
                  pl.BlockSpec((tm, k2), lambda i, j: (i, 0)),
                  pl.BlockSpec((k1, tn), lambda i, j: (0, j)),
                  pl.BlockSpec((k2, tn), lambda i, j: (1, j)),
                  pl.BlockSpec((tm, tn), lambda i, j: (i, j))],
        out_specs=[pl.BlockSpec((tm, tn), lambda i, j: (i, j))] * 2,
        out_shape=[jax.ShapeDtypeStruct((m, n), F32), jax.ShapeDtypeStruct((m, n), BF16)],
        compiler_params=_params("arbitrary", "arbitrary"),
        name="out_proj",
    )(x1, x2, w, w, res)


def _prep_kernel(p_ref, st_ref, mu_ref, w0_ref, a0_ref, bw_ref, ba_ref, bg_ref,
                 r_ref, k_ref, v_ref, d_ref, a_ref, g_ref, ext_ref, *, halo, tt, shift, c):
    @pl.when(pl.program_id(1) == 0)
    def _():
        ext_ref[0:halo, :] = st_ref[...]

    p = p_ref[...]
    ext_ref[halo:halo + tt, :] = p
    prev = ext_ref[halo - shift:halo - shift + tt, :]
    xs = p + (prev - p) * mu_ref[...]
    ext_ref[0:halo, :] = ext_ref[tt:tt + halo, :]

    r_ref[...] = xs[:, 0:c].T
    k_ref[...] = xs[:, c:2 * c].T
    v_ref[...] = xs[:, 2 * c:3 * c].T
    lo = xs[:, 3 * c:]
    wpre = jnp.dot(jnp.tanh(lo).astype(BF16), bw_ref[...], preferred_element_type=F32)
    apre = jnp.dot(lo.astype(BF16), ba_ref[...], preferred_element_type=F32)
    g_ref[...] = jnp.dot(_sigmoid(lo).astype(BF16), bg_ref[...], preferred_element_type=F32)
    ez = jnp.exp(-(w0_ref[...] + wpre))
    d_ref[...] = jnp.exp(-EXP_M_HALF / (1.0 + ez)).T
    a_ref[...] = _sigmoid(a0_ref[...] + apre).T


def _rwkv_prep(proj, state, mu, w0, a0, bw, ba, bg, *, nseq, shift, halo, tt, c, ca):
    m = proj.shape[0]
    rows = m // nseq
    assert rows % tt == 0
    nt = rows // tt
    row_spec = lambda width: pl.BlockSpec((tt, width), lambda s, j: (s * nt + j, 0))
    tr_spec = pl.BlockSpec((None, c, tt), lambda s, j: (s * nt + j, 0, 0))
    full = lambda arr: pl.BlockSpec(arr.shape, lambda s, j: (0,) * arr.ndim)
    out_t = jax.ShapeDtypeStruct((nseq * nt, c, tt), F32)
    return pl.pallas_call(
        functools.partial(_prep_kernel, halo=halo, tt=tt, shift=shift, c=c),
        grid=(nseq, nt),
        in_specs=[row_spec(ca),
                  pl.BlockSpec((None, halo, ca), lambda s, j: (s, 0, 0)),
                  full(mu), full(w0), full(a0), full(bw), full(ba), full(bg)],
        out_specs=[tr_spec] * 5 + [row_spec(c)],
        out_shape=[out_t] * 5 + [jax.ShapeDtypeStruct((m, c), F32)],
        scratch_shapes=[pltpu.VMEM((halo + tt, ca), F32)],
        compiler_params=_params("arbitrary", "arbitrary"),
        name="rwkv_prep",
    )(proj, state, mu, w0, a0, bw, ba, bg)


V_GROUP = 8
K_BLOCK = 32


def _wkv_kernel(*refs, tt, n, cast_steps, cast_gain):
    refs = list(refs)
    (r_ref, k_ref, v_ref, d_ref, a_ref, s0_ref, kk_ref, ka_ref, rk_ref, gw_ref, gb_ref) = refs[:11]
    del refs[:11]
    nc = len(cast_steps)
    w_refs = [refs.pop(0) for _ in range(nc)]
    wg_refs = [refs.pop(0) if has_gain else None for has_gain in cast_gain]
    o_ref, sf_ref = refs.pop(0), refs.pop(0)
    wb_refs = [refs.pop(0) for _ in range(nc)]
    s_ref, y_ref, row_ref, dot_ref, plast_ref = refs

    @pl.when(pl.program_id(1) == 0)
    def _():
        s_ref[...] = s0_ref[...]

    for w_ref, wg_ref, wb_ref, steps in zip(w_refs, wg_refs, wb_refs, cast_steps):
        @pl.when(pl.program_id(1) < steps)
        def _(w_ref=w_ref, wg_ref=wg_ref, wb_ref=wb_ref):
            w = w_ref[...] if wg_ref is None else w_ref[...] * wg_ref[...]
            wb_ref[...] = w.astype(wb_ref.dtype)

    k_k = kk_ref[...]
    k_a = ka_ref[...]
    r_k = rk_ref[...]
    gn_w = gw_ref[...]
    gn_b = gb_ref[...]
    inv_n = 1.0 / n
    lanes = s_ref.shape[-1]

    r = r_ref[...]
    k = k_ref[...]
    w = d_ref[...]
    a = a_ref[...]
    kk = k * k_k
    norm = jnp.sqrt(jnp.sum(kk * kk, axis=1, keepdims=True))
    kk = kk / jnp.maximum(norm, 1e-12)
    k2 = k * (1.0 + (a - 1.0) * k_a)
    bv = kk * a
    p = None
    for t in range(tt):
        row_ref[0, t] = -kk[t] if p is None else -kk[t] * p
        p = w[t] if p is None else p * w[t]
        inv_p = 1.0 / p
        row_ref[1, t] = r[t] * p
        row_ref[2, t] = bv[t] * inv_p
        row_ref[3, t] = k2[t] * inv_p
    plast_ref[...] = p
    dot_ref[0] = jnp.sum(bv * r, axis=1, keepdims=True)
    dot_ref[1] = jnp.sum(k2 * r, axis=1, keepdims=True)
    dot_ref[2] = jnp.sum(r * k2 * r_k, axis=1, keepdims=True)

    def token(t, carry):
        b_r = dot_ref[0, t]
        k_r = dot_ref[1, t]

        def vgroup(gidx, c2):
            vb0 = gidx * V_GROUP
            zero = jnp.zeros((SUBLANE, lanes), F32)

            def reduce_keys(kb, acc):
                sa_acc, q_acc = list(acc[0]), list(acc[1])
                for j in range(K_BLOCK):
                    kx = kb * K_BLOCK + j
                    a_row = row_ref[0, t, pl.ds(kx, 1), :]
                    q_row = row_ref[1, t, pl.ds(kx, 1), :]
                    for i in range(V_GROUP):
                        s_k = s_ref[vb0 + i, kx]
                        sa_acc[i] = sa_acc[i] + s_k * a_row
                        q_acc[i] = q_acc[i] + s_k * q_row
                return tuple(sa_acc), tuple(q_acc)

            sa8, q8 = lax.fori_loop(0, n // K_BLOCK, reduce_keys, ((zero,) * V_GROUP, (zero,) * V_GROUP))
            base = pl.multiple_of(vb0 * SUBLANE, SUBLANE)
            v8 = [v_ref[t, pl.ds(base + i * SUBLANE, SUBLANE), :] for i in range(V_GROUP)]

            def update_keys(kb, c3):
                for j in range(K_BLOCK):
                    kx = kb * K_BLOCK + j
                    b_row = row_ref[2, t, pl.ds(kx, 1), :]
                    k_row = row_ref[3, t, pl.ds(kx, 1), :]
                    for i in range(V_GROUP):
                        s_ref[vb0 + i, kx] = s_ref[vb0 + i, kx] + sa8[i] * b_row + v8[i] * k_row
                return c3

            lax.fori_loop(0, n // K_BLOCK, update_keys, 0)
            for i in range(V_GROUP):
                y_ref[t, pl.ds(base + i * SUBLANE, SUBLANE), :] = q8[i] + sa8[i] * b_r + v8[i] * k_r
            return c2

        lax.fori_loop(0, n // (SUBLANE * V_GROUP), vgroup, 0)
        return carry

    lax.fori_loop(0, tt, token, 0)

    def rescale(vb, carry):
        for kx in range(n):
            s_ref[vb, kx] = s_ref[vb, kx] * plast_ref[kx:kx + 1, :]
        return carry

    lax.fori_loop(0, n // SUBLANE, rescale, 0)

    y = y_ref[...]
    mean = jnp.sum(y, axis=1, keepdims=True) * inv_n
    yc = y - mean
    var = jnp.sum(yc * yc, axis=1, keepdims=True) * inv_n
    o_ref[...] = yc * lax.rsqrt(var + GN_EPS) * gn_w + gn_b + dot_ref[2] * v_ref[...]

    @pl.when(pl.program_id(1) == pl.num_programs(1) - 1)
    def _():
        sf_ref[...] = s_ref[...]


def _cast_rows(rows, nsteps):
    blk = _round_up(-(-rows // nsteps), 2 * SUBLANE)
    while rows % blk:
        blk += 2 * SUBLANE
    return blk, rows // blk


def _wkv(r, k, v, d, a, s0, k_k, k_a, r_k, gn_w, gn_b, *, tt, group_major, cast=()):
    if group_major:
        g, t, n, lanes = r.shape
    else:
        t, g, n, lanes = r.shape
    tt = _pick(t, tt, 1)
    nb = n // SUBLANE
    assert not cast or g == 1
    cast_specs, gain_specs, cast_steps = [], [], []
    for w, gain in cast:
        blk, steps = _cast_rows(w.shape[0], t // tt)
        cast_steps.append(steps)
        index = lambda i, j, last=steps - 1: (jnp.minimum(j, last), 0)
        cast_specs.append(pl.BlockSpec((blk, w.shape[1]), index))
        if gain is not None:
            gain_specs.append(pl.BlockSpec((blk, 1), index))
    weights = [w for w, _ in cast]
    gains = [gain.reshape(-1, 1).astype(F32) for _, gain in cast if gain is not None]
    cast_gain = tuple(gain is not None for _, gain in cast)
    if group_major:
        tok = pl.BlockSpec((None, tt, n, lanes), lambda i, j: (i, j, 0, 0))
    else:
        tok = pl.BlockSpec((tt, None, n, lanes), lambda i, j: (j, i, 0, 0))
    st = pl.BlockSpec((None, nb, n, SUBLANE, lanes), lambda i, j: (i, 0, 0, 0, 0))
    gp = k_k.shape[0]
    par = pl.BlockSpec((None, n, lanes), (lambda i, j: (i, 0, 0)) if gp > 1 else (lambda i, j: (0, 0, 0)))
    y, s_fin, *cast_out = pl.pallas_call(
        functools.partial(_wkv_kernel, tt=tt, n=n, cast_steps=tuple(cast_steps), cast_gain=cast_gain),
        grid=(g, t // tt),
        in_specs=[tok] * 5 + [st] + [par] * 5 + cast_specs + gain_specs,
        out_specs=[tok, st] + cast_specs,
        out_shape=[jax.ShapeDtypeStruct(r.shape, F32),
                   jax.ShapeDtypeStruct((g, nb, n, SUBLANE, lanes), F32)]
                  + [jax.ShapeDtypeStruct(w.shape, BF16) for w in weights],
        scratch_shapes=[pltpu.VMEM((nb, n, SUBLANE, lanes), F32), pltpu.VMEM((tt, n, lanes), F32),
                        pltpu.VMEM((4, tt, n, lanes), F32), pltpu.VMEM((3, tt, 1, lanes), F32),
                        pltpu.VMEM((n, lanes), F32)],
        compiler_params=_params("arbitrary", "arbitrary"),
        name="wkv_recurrence",
    )(r, k, v, d, a, s0, k_k, k_a, r_k, gn_w, gn_b, *weights, *gains)
    return y, s_fin, cast_out


def _to_chain_kernel(x_ref, o_ref, *, nb, nh, n, tt):
    for kb in range(n // SUBLANE):
        ks = slice(kb * SUBLANE, (kb + 1) * SUBLANE)
        per_key = jnp.concatenate([jnp.swapaxes(x_ref[b, :, ks, :], 0, 1) for b in range(nb)], axis=1)
        chains_last = jnp.swapaxes(per_key, 1, 2)
        o_ref[:, ks, :] = jnp.swapaxes(chains_last, 0, 1)


def _to_chain(xt, *, nb, nh):
    tiles, c, tt = xt.shape
    nt = tiles // nb
    n = c // nh
    return pl.pallas_call(
        functools.partial(_to_chain_kernel, nb=nb, nh=nh, n=n, tt=tt),
        grid=(nt,),
        in_specs=[pl.BlockSpec((nb, None, nh, n, tt), lambda j: (0, j, 0, 0, 0))],
        out_specs=pl.BlockSpec((tt, n, nb * nh), lambda j: (j, 0, 0)),
        out_shape=jax.ShapeDtypeStruct((nt * tt, n, nb * nh), F32),
        compiler_params=_params("arbitrary"),
        name="to_chain",
    )(xt.reshape(nb, nt, nh, n, tt))


def _from_chain_gate_kernel(y_ref, g_ref, o_ref, xt_ref, *, nb, nh, n, tt):
    for vb in range(n // SUBLANE):
        vs = slice(vb * SUBLANE, (vb + 1) * SUBLANE)
        chains_mid = jnp.swapaxes(jnp.swapaxes(y_ref[:, vs, :], 0, 1), 1, 2)
        for b in range(nb):
            xt_ref[b, :, vs, :] = jnp.swapaxes(chains_mid[:, b * nh:(b + 1) * nh, :], 0, 1)
    for b in range(nb):
        o_ref[b] = (xt_ref[b].reshape(nh * n, tt).T * g_ref[b]).astype(o_ref.dtype)


def _from_chain_gate(y, g, *, nb, nh, tt):
    m, c = g.shape
    t = m // nb
    n = c // nh
    nt = t // tt
    row = pl.BlockSpec((nb, tt, c), lambda j: (0, j, 0))
    out = pl.pallas_call(
        functools.partial(_from_chain_gate_kernel, nb=nb, nh=nh, n=n, tt=tt),
        grid=(nt,),
        in_specs=[pl.BlockSpec((tt, n, nb * nh), lambda j: (j, 0, 0)), row],
        out_specs=row,
        out_shape=jax.ShapeDtypeStruct((nb, t, c), BF16),
        scratch_shapes=[pltpu.VMEM((nb, nh, n, tt), F32)],
        compiler_params=_params("arbitrary"),
        name="from_chain_gate",
    )(y, g.reshape(nb, t, c))
    return out.reshape(m, c)


def _transpose_gate_kernel(y_ref, g_ref, o_ref):
    o_ref[...] = (y_ref[...].T * g_ref[...]).astype(o_ref.dtype)


def _transpose_gate(yt, g):
    tiles, c, tt = yt.shape
    return pl.pallas_call(
        _transpose_gate_kernel,
        grid=(tiles,),
        in_specs=[pl.BlockSpec((None, c, tt), lambda i: (i, 0, 0)),
                  pl.BlockSpec((tt, c), lambda i: (i, 0))],
        out_specs=pl.BlockSpec((tt, c), lambda i: (i, 0)),
        out_shape=jax.ShapeDtypeStruct((tiles * tt, c), BF16),
        compiler_params=_params("arbitrary"),
        name="transpose_gate",
    )(yt, g)


def _pool_kernel(*refs, gps, halo, tt, shift, pos0):
    u_refs = refs[:gps]
    h_ref, w_ref, sc_ref, o_ref, ext_ref = refs[gps:]
    gw = u_refs[0].shape[1]
    gb = pl.program_id(1)
    j = pl.program_id(2)

    @pl.when(j == 0)
    def _():
        ext_ref[0:halo, :] = h_ref[...]

    row = lax.broadcasted_iota(jnp.int32, (tt, 1), 0) + j * tt
    time = row if shift == 1 else row // shift

    def group(g, win):
        cols = slice(g * gw, (g + 1) * gw)
        x = u_refs[g][...]
        acc = x
        for back in range(1, win):
            lo = halo - back * shift
            acc = acc + ext_ref[lo:lo + tt, cols]
        cnt = jnp.minimum(pos0 + time + 1, win).astype(F32)
        d = acc / cnt - x
        y = jnp.dot(d.astype(BF16), w_ref[g], preferred_element_type=F32)
        o_ref[:, cols] = (y * sc_ref[:, cols]).astype(o_ref.dtype)

    for g in range(gps):
        ext_ref[halo:halo + tt, g * gw:(g + 1) * gw] = u_refs[g][...]
    if gps == len(POOL_WINDOWS):
        for g, win in enumerate(POOL_WINDOWS):
            group(g, win)
    else:
        for g, win in enumerate(POOL_WINDOWS):
            pl.when(gb == g)(functools.partial(group, 0, win))

    if tt >= halo:
        ext_ref[0:halo, :] = ext_ref[tt:tt + halo, :]


def _pool(proj, col0, hist, w_pool, scale, *, nseq, shift, tt, pos0, gps):
    m = proj.shape[0]
    cp = scale.shape[1]
    ng = len(POOL_WINDOWS)
    assert gps in (1, ng)
    gw = cp // ng
    assert col0 % gw == 0
    g0 = col0 // gw
    rows = m // nseq
    tt = _pick(rows, tt)
    nt = rows // tt
    halo = POOL_HIST * shift
    assert nt == 1 or tt >= halo
    u_specs = [pl.BlockSpec((tt, gw), lambda s, gb, j, g=g: (s * nt + j, g0 + gb * gps + g)) for g in range(gps)]
    return pl.pallas_call(
        functools.partial(_pool_kernel, gps=gps, halo=halo, tt=tt, shift=shift, pos0=pos0),
        grid=(nseq, ng // gps, nt),
        in_specs=u_specs + [pl.BlockSpec((None, halo, gps * gw), lambda s, gb, j: (s, 0, gb)),
                            pl.BlockSpec((gps, gw, gw), lambda s, gb, j: (gb, 0, 0)),
                            pl.BlockSpec((1, gps * gw), lambda s, gb, j: (0, gb))],
        out_specs=pl.BlockSpec((tt, gps * gw), lambda s, gb, j: (s * nt + j, gb)),
        out_shape=jax.ShapeDtypeStruct((m, cp), BF16),
        scratch_shapes=[pltpu.VMEM((halo + tt, gps * gw), F32)],
        compiler_params=_params("arbitrary", "arbitrary", "arbitrary"),
        name="pool_mix",
    )(*([proj] * gps), hist, w_pool, scale)


def _attend(q, k, v, scale):
    s = lax.dot_general(q, k, (((1,), (1,)), ((), ())), preferred_element_type=F32) * scale
    s = s - jnp.max(s, axis=-1, keepdims=True)
    e = jnp.exp(s)
    pr = e / jnp.sum(e, axis=-1, keepdims=True)
    return jnp.dot(pr.astype(BF16), v, preferred_element_type=F32)


def _attn_prompt_kernel(q_ref, k_ref, v_ref, o_ref, *, scale, heads, hd):
    for h in range(heads):
        cols = slice(h * hd, (h + 1) * hd)
        o = _attend(q_ref[:, cols], k_ref[:, cols].astype(BF16), v_ref[:, cols].astype(BF16), scale)
        o_ref[:, cols] = o.astype(o_ref.dtype)


def _attn_prompt(q, mk, mv, *, batch, heads):
    m, d = q.shape
    t = m // batch
    nm = mk.shape[1]
    hd = d // heads
    tq = _pick(t, 512)
    nq = t // tq
    return pl.pallas_call(
        functools.partial(_attn_prompt_kernel, scale=hd ** -0.5, heads=heads, hd=hd),
        grid=(batch, nq),
        in_specs=[pl.BlockSpec((tq, d), lambda b, i: (b * nq + i, 0)),
                  pl.BlockSpec((None, nm, d), lambda b, i: (b, 0, 0)),
                  pl.BlockSpec((None, nm, d), lambda b, i: (b, 0, 0))],
        out_specs=pl.BlockSpec((tq, d), lambda b, i: (b * nq + i, 0)),
        out_shape=jax.ShapeDtypeStruct((m, d), BF16),
        compiler_params=_params("arbitrary", "arbitrary"),
        name="attn_prompt",
    )(q, mk, mv)


def _attn_sample_kernel(q_ref, k_ref, v_ref, o_ref, *, scale, heads, hd):
    t = q_ref.shape[0]
    nm = k_ref.shape[0]
    q = jnp.concatenate([q_ref[:, h * hd:(h + 1) * hd] for h in range(heads)], axis=0)
    k = k_ref[...].reshape(nm * heads, hd)
    v = v_ref[...].reshape(nm * heads, hd)
    s = lax.dot_general(q, k, (((1,), (1,)), ((), ())), preferred_element_type=F32) * scale
    row_head = lax.broadcasted_iota(jnp.int32, s.shape, 0) // t
    col_head = lax.broadcasted_iota(jnp.int32, s.shape, 1) % heads
    s = jnp.where(row_head == col_head, s, -jnp.inf)
    s = s - jnp.max(s, axis=-1, keepdims=True)
    e = jnp.exp(s)
    pr = e / jnp.sum(e, axis=-1, keepdims=True)
    o = jnp.dot(pr, v, preferred_element_type=F32)
    for h in range(heads):
        o_ref[:, h * hd:(h + 1) * hd] = o[h * t:(h + 1) * t, :]


def _attn_sample(q, mk, mv, layer):
    b, t, d = q.shape
    _, _, nm, heads, hd = mk.shape
    kv_spec = pl.BlockSpec((None, None, nm, heads, hd), lambda i: (layer, i, 0, 0, 0))
    return pl.pallas_call(
        functools.partial(_attn_sample_kernel, scale=hd ** -0.5, heads=heads, hd=hd),
        grid=(b,),
        in_specs=[pl.BlockSpec((None, t, d), lambda i: (i, 0, 0)), kv_spec, kv_spec],
        out_specs=pl.BlockSpec((None, t, d), lambda i: (i, 0, 0)),
        out_shape=jax.ShapeDtypeStruct((b, t, d), F32),
        compiler_params=_params("arbitrary"),
        name="attn_sample",
    )(q, mk, mv)


def _ffn_up_kernel(h_ref, hp_ref, wg_ref, wv_ref, cw_ref, cb_ref, st_ref, act_ref, zt_ref, zext_ref, rstd_ref,
                   *, halo, tm, shift, recompute_halo):
    @pl.when(pl.program_id(2) == 0)
    def _():
        rstd_ref[...] = _row_rstd(h_ref[...])

    h = h_ref[...]
    rstd = rstd_ref[...]
    z = jnp.dot(h, wg_ref[...], preferred_element_type=F32) * rstd
    val = jnp.dot(h, wv_ref[...], preferred_element_type=F32) * rstd
    zprev = st_ref[...]
    if recompute_halo:
        hp = hp_ref[...]
        zhalo = jnp.dot(hp, wg_ref[...], preferred_element_type=F32) * _row_rstd(hp)
        zprev = jnp.where(pl.program_id(1) == 0, zprev, zhalo)
    zext_ref[0:halo, :] = zprev
    zext_ref[halo:halo + tm, :] = z
    zm1 = zext_ref[halo - shift:halo - shift + tm, :]
    zm2 = zext_ref[halo - 2 * shift:halo - 2 * shift + tm, :]
    zc = cb_ref[...] + cw_ref[0:1, :] * zm2 + cw_ref[1:2, :] * zm1 + cw_ref[2:3, :] * z
    act_ref[...] = (zc * _sigmoid(zc) * val).astype(act_ref.dtype)
    zt_ref[...] = zext_ref[tm:tm + halo, :]


def _ffn_up(h, wg, wv, cw, cb, state, *, nseq, shift, halo, tm, tn):
    m, d = h.shape
    _, fp = wg.shape
    rows = m // nseq
    tm = _pick(rows, tm)
    nt = rows // tm
    tn = min(tn, fp)
    assert tm >= halo and (nt == 1 or halo == SUBLANE)
    hb = tm // SUBLANE
    return pl.pallas_call(
        functools.partial(_ffn_up_kernel, halo=halo, tm=tm, shift=shift, recompute_halo=nt > 1),
        grid=(nseq, nt, pl.cdiv(fp, tn)),
        in_specs=[pl.BlockSpec((tm, d), lambda s, i, j: (s * nt + i, 0)),
                  pl.BlockSpec((SUBLANE, d), lambda s, i, j: (jnp.maximum((s * nt + i) * hb - 1, 0), 0)),
                  pl.BlockSpec((d, tn), lambda s, i, j: (0, j)),
                  pl.BlockSpec((d, tn), lambda s, i, j: (0, j)),
                  pl.BlockSpec((CONV_W, tn), lambda s, i, j: (0, j)),
                  pl.BlockSpec((1, tn), lambda s, i, j: (0, j)),
                  pl.BlockSpec((None, halo, tn), lambda s, i, j: (s, 0, j))],
        out_specs=[pl.BlockSpec((tm, tn), lambda s, i, j: (s * nt + i, j)),
                   pl.BlockSpec((halo, tn), lambda s, i, j: (s * nt + i, j))],
        out_shape=[jax.ShapeDtypeStruct((m, fp), BF16),
                   jax.ShapeDtypeStruct((nseq * nt * halo, fp), F32)],
        scratch_shapes=[pltpu.VMEM((halo + tm, tn), F32), pltpu.VMEM((tm, 1), F32)],
        compiler_params=_params("arbitrary", "arbitrary", "arbitrary"),
        name="ffn_up",
    )(h, h, wg, wv, cw, cb, state)


def kernel(x_prompt, x_sample, mem_prompt, cache_mem_k, cache_mem_v, state_shift, state_wkv, state_pool,
           state_conv, g_mix, w_in, mu_shift, w0, b_w, a0, b_a, b_g, k_k, k_a, r_k, gn_w, gn_b, w_pool,
           pool_scale, w_out, g_mem, w_mk, w_mv, g_attn, w_mq, w_mo, g_ffn, w_gate, w_val, conv_w, conv_b,
           w_down, g_final):
    bp, tp, d = x_prompt.shape
    bs, ts, _ = x_sample.shape
    depth = w_in.shape[0]
    c = w0.shape[1]
    nh = c // HEAD_SIZE
    nvb = HEAD_SIZE // SUBLANE
    cp = pool_scale.shape[1]
    pw = mu_shift.shape[1]
    lora = pw - 3 * c
    lp = _round_up(lora, LANE)
    ca = 3 * c + lp
    dl, al, gl = b_w.shape[1], b_a.shape[1], b_g.shape[1]
    f = w_gate.shape[2]
    fp = f
    nmem = mem_prompt.shape[1]
    nbuf = state_pool.shape[2]
    mp, ms = bp * tp, bs * ts

    xp = x_prompt.reshape(mp, d)
    xs = x_sample.transpose(1, 0, 2).reshape(ms, d)

    outs = {n: [] for n in ("pm_k", "pm_v", "p_sh", "p_wkv", "p_pool", "p_conv",
                            "s_sh", "s_wkv", "s_pool", "s_conv")}

    for l in range(depth):
        w_in_t = w_in[l].T.astype(BF16)
        wu_t = w_in_t[pw:]
        mu = jnp.pad(mu_shift[l], (0, ca - pw)).reshape(1, ca)
        bw = jnp.zeros((lp, c), F32).at[0:dl].set(b_w[l]).astype(BF16)
        ba = jnp.zeros((lp, c), F32).at[dl:dl + al].set(b_a[l]).astype(BF16)
        bg = jnp.zeros((lp, c), F32).at[dl + al:dl + al + gl].set(b_g[l]).astype(BF16)
        w0l = w0[l].reshape(1, c)
        a0l = a0[l].reshape(1, c)
        wpool = w_pool[l].astype(BF16)
        pscale = pool_scale[l].reshape(1, cp)
        late = {}
        cw = conv_w[l]
        cb = conv_b[l].reshape(1, fp)

        def head_tile(p):
            return p.reshape(nh, HEAD_SIZE).T

        chain_p = [jnp.tile(head_tile(p), (1, bp)).reshape(1, HEAD_SIZE, bp * nh)
                   for p in (k_k[l], k_a[l], r_k[l].reshape(-1), gn_w[l], gn_b[l])]
        chain_s = [jnp.broadcast_to(p.reshape(nh, HEAD_SIZE, 1), (nh, HEAD_SIZE, bs))
                   for p in (k_k[l], k_a[l], r_k[l].reshape(-1), gn_w[l], gn_b[l])]

        mn = _rmsnorm(mem_prompt.reshape(bp * nmem, d), g_mem[l], BF16)
        mk = _matmul(mn, w_mk[l], out_dtype=F32, name="mem_k")
        mv = _matmul(mn, w_mv[l], out_dtype=F32, name="mem_v")
        outs["pm_k"].append(mk.reshape(bp, nmem, MEM_HEADS, d // MEM_HEADS))
        outs["pm_v"].append(mv.reshape(bp, nmem, MEM_HEADS, d // MEM_HEADS))

        def layer(x, *, prompt):
            m = x.shape[0]
            if prompt:
                nseq, shift, halo1, halo, pos0 = bp, 1, SUBLANE, SUBLANE, 0
                st_shift = jnp.zeros((bp, halo1, ca), F32)
                hist = jnp.zeros((bp, POOL_HIST, cp), F32)
                st_conv = jnp.zeros((bp, halo, fp), F32)
            else:
                nseq, shift, halo1, halo, pos0 = 1, bs, bs, (CONV_W - 1) * bs, PAST_LEN
                st_shift = jnp.pad(state_shift[l], ((0, 0), (0, ca - pw))).reshape(1, halo1, ca)
                hist = jnp.pad(state_pool[l].transpose(1, 0, 2), ((POOL_HIST - nbuf, 0), (0, 0), (0, 0)))
                hist = hist.reshape(1, POOL_HIST * bs, cp)
                st_conv = state_conv[l].transpose(1, 0, 2).reshape(1, halo, fp)

            xn = _rmsnorm(x, g_mix[l], BF16)
            proj = _matmul_cols2(xn, w_in_t, ca, wu_t, name="in_proj")
            tile = min(LANE, tp) if prompt else bs
            *rkvda, g = _rwkv_prep(proj, st_shift, mu, w0l, a0l, bw, ba, bg,
                                   nseq=nseq, shift=shift, halo=halo1, tt=tile, c=c, ca=ca)
            if prompt:
                chains = [_to_chain(x, nb=bp, nh=nh).reshape(1, tp, HEAD_SIZE, bp * nh) for x in rkvda]
                s0 = jnp.zeros((1, nvb, HEAD_SIZE, SUBLANE, bp * nh), F32)
                y, s_fin, cast_out = _wkv(*chains, s0, *chain_p, tt=16, group_major=True,
                                          cast=((w_out[l], None), (w_mq[l], g_attn[l]), (w_mo[l], None),
                                                (w_gate[l], g_ffn[l]), (w_val[l], g_ffn[l]),
                                                (w_down[l], None)))
                late.update(zip(("wo", "wmq", "wmo", "wg", "wv", "wd"), cast_out))
                r_out = _from_chain_gate(y.reshape(tp, HEAD_SIZE, bp * nh), g, nb=bp, nh=nh, tt=tile)
                new_wkv = s_fin.reshape(nvb, HEAD_SIZE, SUBLANE, bp, nh).transpose(3, 4, 0, 2, 1)
                new_wkv = new_wkv.reshape(bp, nh, HEAD_SIZE, HEAD_SIZE)
            else:
                chains = [x.reshape(ts, nh, HEAD_SIZE, bs) for x in rkvda]
                s0 = state_wkv[l].reshape(bs, nh, nvb, SUBLANE, HEAD_SIZE).transpose(1, 2, 4, 3, 0)
                y, s_fin, _ = _wkv(*chains, s0, *chain_s, tt=ts, group_major=False)
                r_out = _transpose_gate(y.reshape(ts, c, bs), g)
                new_wkv = s_fin.transpose(4, 0, 1, 3, 2).reshape(bs, nh, HEAD_SIZE, HEAD_SIZE)
            p_out = _pool(proj, ca, hist, wpool, pscale, nseq=nseq, shift=shift,
                          tt=256 if prompt else m, pos0=pos0, gps=len(POOL_WINDOWS) if prompt else 1)
            wmq, wmo, wg, wv, wd = (late[n] for n in ("wmq", "wmo", "wg", "wv", "wd"))
            x1, x1b = _matmul2_res(r_out, p_out, late["wo"], x)

            if prompt:
                q = _matmul(x1b, wmq, out_dtype=BF16, row_norm=True, name="attn_q")
                o = _attn_prompt(q, mk.reshape(bp, nmem, d), mv.reshape(bp, nmem, d),
                                 batch=bp, heads=MEM_HEADS)
            else:
                q = _matmul(x1b, wmq, out_dtype=F32, row_norm=True, name="attn_q")
                q = q.reshape(ts, bs, d).transpose(1, 0, 2)
                o = _attn_sample(q, cache_mem_k, cache_mem_v, l)
                o = o.transpose(1, 0, 2).reshape(m, d)
            x2, x2b = _matmul(o, wmo, out_dtype=F32, res=x1, emit_bf16=True, name="attn_o")

            act, ztail = _ffn_up(x2b, wg, wv, cw, cb, st_conv, nseq=nseq, shift=shift, halo=halo,
                                 tm=1024, tn=512)
            x3 = _matmul(act, wd, out_dtype=F32, res=x2, tm=512, tn=512, name="ffn_down")

            if prompt:
                new_shift = proj.reshape(bp, tp, ca + cp)[:, -1, :pw]
                new_pool = proj.reshape(bp, tp, ca + cp)[:, tp - nbuf:, ca:]
                new_conv = ztail.reshape(bp, -1, halo, fp)[:, -1, halo - (CONV_W - 1):, :f]
            else:
                new_shift = proj[(ts - 1) * bs:, :pw]
                u_b = proj[:, ca:].reshape(ts, bs, cp).transpose(1, 0, 2)
                new_pool = jnp.concatenate([state_pool[l], u_b], axis=1)[:, -nbuf:]
                new_conv = ztail.reshape(CONV_W - 1, bs, fp)[:, :, :f].transpose(1, 0, 2)
            return x3, new_shift, new_wkv, new_pool, new_conv

        xp, sh, wk, po, co = layer(xp, prompt=True)
        outs["p_sh"].append(sh)
        outs["p_wkv"].append(wk)
        outs["p_pool"].append(po)
        outs["p_conv"].append(co)
        xs, sh, wk, po, co = layer(xs, prompt=False)
        outs["s_sh"].append(sh)
        outs["s_wkv"].append(wk)
        outs["s_pool"].append(po)
        outs["s_conv"].append(co)

    y_prompt = _rmsnorm(xp, g_final, F32).reshape(bp, tp, d)
    y_sample = _rmsnorm(xs, g_final, F32).reshape(ts, bs, d).transpose(1, 0, 2)
    stk = lambda n: jnp.stack(outs[n])
    return (y_prompt, y_sample, stk("pm_k"), stk("pm_v"), stk("p_sh"), stk("p_wkv"), stk("p_pool"),
            stk("p_conv"), stk("s_sh"), stk("s_wkv"), stk("s_pool"), stk("s_conv"))
```

```python
import functools

import jax
import jax.numpy as jnp
from jax import lax
from jax.experimental import pallas as pl
from jax.experimental.pallas import tpu as pltpu

F32 = jnp.float32
BF16 = jnp.bfloat16

HEAD_SIZE = 64
POOL_WINDOWS = (2, 4, 8, 16)
POOL_HIST = 16
MEM_HEADS = 4
CONV_W = 3
PAST_LEN = 16384
NORM_EPS = 1e-6
GN_EPS = 64e-5
EXP_M_HALF = 0.6065306597126334
LANE = 128
SUBLANE = 8
VMEM_LIMIT = 56 * 1024 * 1024


def _pick(dim, pref, mult=SUBLANE):
    if dim <= pref:
        return dim
    t = (pref // mult) * mult
    while t >= mult:
        if dim % t == 0:
            return t
        t -= mult
    return dim


def _round_up(x, m):
    return (x + m - 1) // m * m


def _params(*sem):
    return pltpu.CompilerParams(dimension_semantics=sem, vmem_limit_bytes=VMEM_LIMIT)


def _sigmoid(x):
    return 1.0 / (1.0 + jnp.exp(-x))


def _rmsnorm_kernel(x_ref, g_ref, o_ref):
    x = x_ref[...].astype(F32)
    ms = jnp.mean(x * x, axis=-1, keepdims=True)
    o_ref[...] = (x * lax.rsqrt(ms + NORM_EPS) * g_ref[...]).astype(o_ref.dtype)


def _rmsnorm(x, g, out_dtype):
    m, d = x.shape
    tr = _pick(m, 256)
    return pl.pallas_call(
        _rmsnorm_kernel,
        grid=(m // tr,),
        in_specs=[pl.BlockSpec((tr, d), lambda i: (i, 0)),
                  pl.BlockSpec((1, d), lambda i: (0, 0))],
        out_specs=pl.BlockSpec((tr, d), lambda i: (i, 0)),
        out_shape=jax.ShapeDtypeStruct((m, d), out_dtype),
        compiler_params=_params("arbitrary"),
        name="rmsnorm",
    )(x, g.reshape(1, d).astype(F32))


def _row_rstd(x):
    xf = x.astype(F32)
    return lax.rsqrt(jnp.mean(xf * xf, axis=-1, keepdims=True) + NORM_EPS)


def _mm_kernel(*refs, has_res, emit_bf16, row_norm):
    refs = list(refs)
    x_ref, w_ref = refs[:2]
    del refs[:2]
    res_ref = refs.pop(0) if has_res else None
    o_ref = refs.pop(0)
    ob_ref = refs.pop(0) if emit_bf16 else None
    rstd_ref = refs.pop(0) if row_norm else None
    if row_norm:
        @pl.when(pl.program_id(1) == 0)
        def _():
            rstd_ref[...] = _row_rstd(x_ref[...])
    acc = jnp.dot(x_ref[...].astype(BF16), w_ref[...].astype(BF16), preferred_element_type=F32)
    if row_norm:
        acc = acc * rstd_ref[...]
    if has_res:
        acc = acc + res_ref[...]
    o_ref[...] = acc.astype(o_ref.dtype)
    if emit_bf16:
        ob_ref[...] = acc.astype(BF16)


def _matmul(x, w, *, out_dtype, res=None, emit_bf16=False, row_norm=False, tm=1024, tn=512, name="matmul"):
    m, kd = x.shape
    _, n = w.shape
    tm = _pick(m, tm)
    tn = _pick(n, tn, LANE)
    tile = pl.BlockSpec((tm, tn), lambda i, j: (i, j))
    in_specs = [pl.BlockSpec((tm, kd), lambda i, j: (i, 0)),
                pl.BlockSpec((kd, tn), lambda i, j: (0, j))]
    args = [x, w]
    if res is not None:
        in_specs.append(tile)
        args.append(res)
    out_specs, out_shape = [tile], [jax.ShapeDtypeStruct((m, n), out_dtype)]
    if emit_bf16:
        out_specs.append(tile)
        out_shape.append(jax.ShapeDtypeStruct((m, n), BF16))
    out = pl.pallas_call(
        functools.partial(_mm_kernel, has_res=res is not None, emit_bf16=emit_bf16, row_norm=row_norm),
        grid=(m // tm, n // tn),
        in_specs=in_specs,
        out_specs=out_specs,
        out_shape=out_shape,
        scratch_shapes=[pltpu.VMEM((tm, 1), F32)] if row_norm else [],
        compiler_params=_params("arbitrary", "arbitrary"),
        name=name,
    )(*args)
    return out if emit_bf16 else out[0]


def _norm_mm2_kernel(x_ref, g_ref, wa_ref, wb_ref, oa_ref, ob_ref, xn_ref, *, na):
    j = pl.program_id(1)

    @pl.when(j == 0)
    def _():
        x = x_ref[...]
        xn_ref[...] = (x * _row_rstd(x) * g_ref[...]).astype(BF16)

    @pl.when(j < na)
    def _():
        oa_ref[...] = jnp.dot(xn_ref[...], wa_ref[...].astype(BF16), preferred_element_type=F32)

    @pl.when(j >= na)
    def _():
        ob_ref[...] = jnp.dot(xn_ref[...], wb_ref[...].astype(BF16), preferred_element_type=F32)


def _norm_matmul2(x, g, wa, wb, *, tm=512, tn=256, name="norm_matmul2"):
    m, kd = x.shape
    n = wa.shape[1]
    tm = _pick(m, tm)
    tn = _pick(n, tn, LANE)
    na = n // tn
    w_a = pl.BlockSpec((kd, tn), lambda i, j: (0, jnp.minimum(j, na - 1)))
    w_b = pl.BlockSpec((kd, tn), lambda i, j: (0, jnp.maximum(j - na, 0)))
    o_a = pl.BlockSpec((tm, tn), lambda i, j: (i, jnp.minimum(j, na - 1)))
    o_b = pl.BlockSpec((tm, tn), lambda i, j: (i, jnp.maximum(j - na, 0)))
    return pl.pallas_call(
        functools.partial(_norm_mm2_kernel, na=na),
        grid=(m // tm, 2 * na),
        in_specs=[pl.BlockSpec((tm, kd), lambda i, j: (i, 0)), pl.BlockSpec((1, kd), lambda i, j: (0, 0)), w_a, w_b],
        out_specs=[o_a, o_b],
        out_shape=[jax.ShapeDtypeStruct((m, n), F32)] * 2,
        scratch_shapes=[pltpu.VMEM((tm, kd), BF16)],
        compiler_params=_params("arbitrary", "arbitrary"),
        name=name,
    )(x, g.reshape(1, kd).astype(F32), wa, wb)


def _mm_cols2_kernel(x_ref, wa_ref, wb_ref, o_ref, *, na):
    j = pl.program_id(1)
    contract_last = (((1,), (1,)), ((), ()))

    @pl.when(j < na)
    def _():
        o_ref[...] = lax.dot_general(x_ref[...], wa_ref[...], contract_last, preferred_element_type=F32)

    @pl.when(j >= na)
    def _():
        o_ref[...] = lax.dot_general(x_ref[...], wb_ref[...], contract_last, preferred_element_type=F32)


def _matmul_cols2(x, wa_t, na_cols, wb_t, *, tm=1024, tn=512, name="matmul"):
    m, kd = x.shape
    tm = _pick(m, tm)
    tn = _pick(na_cols, tn, LANE)
    assert na_cols % tn == 0 and wb_t.shape[0] % tn == 0
    na, nb = na_cols // tn, wb_t.shape[0] // tn
    return pl.pallas_call(
        functools.partial(_mm_cols2_kernel, na=na),
        grid=(m // tm, na + nb),
        in_specs=[pl.BlockSpec((tm, kd), lambda i, j: (i, 0)),
                  pl.BlockSpec((tn, kd), lambda i, j: (jnp.minimum(j, na - 1), 0)),
                  pl.BlockSpec((tn, kd), lambda i, j: (jnp.maximum(j - na, 0), 0))],
        out_specs=pl.BlockSpec((tm, tn), lambda i, j: (i, j)),
        out_shape=jax.ShapeDtypeStruct((m, (na + nb) * tn), F32),
        compiler_params=_params("arbitrary", "arbitrary"),
        name=name,
    )(x, wa_t, wb_t)


def _mm2_kernel(x1_ref, x2_ref, w1_ref, w2_ref, res_ref, o_ref, ob_ref):
    acc = jnp.dot(x1_ref[...], w1_ref[...], preferred_element_type=F32)
    acc = acc + jnp.dot(x2_ref[...], w2_ref[...], preferred_element_type=F32)
    acc = acc + res_ref[...]
    o_ref[...] = acc
    ob_ref[...] = acc.astype(BF16)


def _matmul2_res(x1, x2, w, res, *, tm=1024, tn=512):
    m, k1 = x1.shape
    _, k2 = x2.shape
    kw, n = w.shape
    assert k1 == k2 and kw == k1 + k2
    tm = _pick(m, tm)
    tn = _pick(n, tn, LANE)
    return pl.pallas_call(
        _mm2_kernel,
        grid=(m // tm, n // tn),
        in_specs=[pl.BlockSpec((tm, k1), lambda i, j: (i, 0)),
                  pl.BlockSpec((tm, k2), lambda i, j: (i, 0)),
                  pl.BlockSpec((k1, tn), lambda i, j: (0, j)),
                  pl.BlockSpec((k2, tn), lambda i, j: (1, j)),
                  pl.BlockSpec((tm, tn), lambda i, j: (i, j))],
        out_specs=[pl.BlockSpec((tm, tn), lambda i, j: (i, j))] * 2,
        out_shape=[jax.ShapeDtypeStruct((m, n), F32), jax.ShapeDtypeStruct((m, n), BF16)],
        compiler_params=_params("arbitrary", "arbitrary"),
        name="out_proj",
    )(x1, x2, w, w, res)


def _prep_kernel(p_ref, st_ref, mu_ref, w0_ref, a0_ref, bw_ref, ba_ref, bg_ref,
                 r_ref, k_ref, v_ref, d_ref, a_ref, g_ref, ext_ref, *, halo, tt, shift, c):
    @pl.when(pl.program_id(1) == 0)
    def _():
        ext_ref[0:halo, :] = st_ref[...]

    p = p_ref[...]
    ext_ref[halo:halo + tt, :] = p
    prev = ext_ref[halo - shift:halo - shift + tt, :]
    xs = p + (prev - p) * mu_ref[...]
    ext_ref[0:halo, :] = ext_ref[tt:tt + halo, :]

    r_ref[...] = xs[:, 0:c].T
    k_ref[...] = xs[:, c:2 * c].T
    v_ref[...] = xs[:, 2 * c:3 * c].T
    lo = xs[:, 3 * c:]
    wpre = jnp.dot(jnp.tanh(lo).astype(BF16), bw_ref[...], preferred_element_type=F32)
    apre = jnp.dot(lo.astype(BF16), ba_ref[...], preferred_element_type=F32)
    g_ref[...] = jnp.dot(_sigmoid(lo).astype(BF16), bg_ref[...], preferred_element_type=F32)
    ez = jnp.exp(-(w0_ref[...] + wpre))
    d_ref[...] = jnp.exp(-EXP_M_HALF / (1.0 + ez)).T
    a_ref[...] = _sigmoid(a0_ref[...] + apre).T


def _rwkv_prep(proj, state, mu, w0, a0, bw, ba, bg, *, nseq, shift, halo, tt, c, ca):
    m = proj.shape[0]
    rows = m // nseq
    assert rows % tt == 0
    nt = rows // tt
    row_spec = lambda width: pl.BlockSpec((tt, width), lambda s, j: (s * nt + j, 0))
    tr_spec = pl.BlockSpec((None, c, tt), lambda s, j: (s * nt + j, 0, 0))
    full = lambda arr: pl.BlockSpec(arr.shape, lambda s, j: (0,) * arr.ndim)
    out_t = jax.ShapeDtypeStruct((nseq * nt, c, tt), F32)
    return pl.pallas_call(
        functools.partial(_prep_kernel, halo=halo, tt=tt, shift=shift, c=c),
        grid=(nseq, nt),
        in_specs=[row_spec(ca),
                  pl.BlockSpec((None, halo, ca), lambda s, j: (s, 0, 0)),
                  full(mu), full(w0), full(a0), full(bw), full(ba), full(bg)],
        out_specs=[tr_spec] * 5 + [row_spec(c)],
        out_shape=[out_t] * 5 + [jax.ShapeDtypeStruct((m, c), F32)],
        scratch_shapes=[pltpu.VMEM((halo + tt, ca), F32)],
        compiler_params=_params("arbitrary", "arbitrary"),
        name="rwkv_prep",
    )(proj, state, mu, w0, a0, bw, ba, bg)


V_GROUP = 8
K_BLOCK = 32


def _wkv_kernel(*refs, tt, n, cast_steps, cast_gain):
    refs = list(refs)
    (r_ref, k_ref, v_ref, d_ref, a_ref, s0_ref, kk_ref, ka_ref, rk_ref, gw_ref, gb_ref) = refs[:11]
    del refs[:11]
    nc = len(cast_steps)
    w_refs = [refs.pop(0) for _ in range(nc)]
    wg_refs = [refs.pop(0) if has_gain else None for has_gain in cast_gain]
    o_ref, sf_ref = refs.pop(0), refs.pop(0)
    wb_refs = [refs.pop(0) for _ in range(nc)]
    s_ref, y_ref, row_ref, dot_ref, plast_ref = refs

    @pl.when(pl.program_id(1) == 0)
    def _():
        s_ref[...] = s0_ref[...]

    for w_ref, wg_ref, wb_ref, steps in zip(w_refs, wg_refs, wb_refs, cast_steps):
        @pl.when(pl.program_id(1) < steps)
        def _(w_ref=w_ref, wg_ref=wg_ref, wb_ref=wb_ref):
            w = w_ref[...] if wg_ref is None else w_ref[...] * wg_ref[...]
            wb_ref[...] = w.astype(wb_ref.dtype)

    k_k = kk_ref[...]
    k_a = ka_ref[...]
    r_k = rk_ref[...]
    gn_w = gw_ref[...]
    gn_b = gb_ref[...]
    inv_n = 1.0 / n
    lanes = s_ref.shape[-1]

    r = r_ref[...]
    k = k_ref[...]
    w = d_ref[...]
    a = a_ref[...]
    kk = k * k_k
    norm = jnp.sqrt(jnp.sum(kk * kk, axis=1, keepdims=True))
    kk = kk / jnp.maximum(norm, 1e-12)
    k2 = k * (1.0 + (a - 1.0) * k_a)
    bv = kk * a
    p = None
    for t in range(tt):
        row_ref[0, t] = -kk[t] if p is None else -kk[t] * p
        p = w[t] if p is None else p * w[t]
        inv_p = 1.0 / p
        row_ref[1, t] = r[t] * p
        row_ref[2, t] = bv[t] * inv_p
        row_ref[3, t] = k2[t] * inv_p
    plast_ref[...] = p
    dot_ref[0] = jnp.sum(bv * r, axis=1, keepdims=True)
    dot_ref[1] = jnp.sum(k2 * r, axis=1, keepdims=True)
    dot_ref[2] = jnp.sum(r * k2 * r_k, axis=1, keepdims=True)

    def token(t, carry):
        b_r = dot_ref[0, t]
        k_r = dot_ref[1, t]

        def vgroup(gidx, c2):
            vb0 = gidx * V_GROUP
            zero = jnp.zeros((SUBLANE, lanes), F32)

            def reduce_keys(kb, acc):
                sa_acc, q_acc = list(acc[0]), list(acc[1])
                for j in range(K_BLOCK):
                    kx = kb * K_BLOCK + j
                    a_row = row_ref[0, t, pl.ds(kx, 1), :]
                    q_row = row_ref[1, t, pl.ds(kx, 1), :]
                    for i in range(V_GROUP):
                        s_k = s_ref[vb0 + i, kx]
                        sa_acc[i] = sa_acc[i] + s_k * a_row
                        q_acc[i] = q_acc[i] + s_k * q_row
                return tuple(sa_acc), tuple(q_acc)

            sa8, q8 = lax.fori_loop(0, n // K_BLOCK, reduce_keys, ((zero,) * V_GROUP, (zero,) * V_GROUP))
            base = pl.multiple_of(vb0 * SUBLANE, SUBLANE)
            v8 = [v_ref[t, pl.ds(base + i * SUBLANE, SUBLANE), :] for i in range(V_GROUP)]

            def update_keys(kb, c3):
                for j in range(K_BLOCK):
                    kx = kb * K_BLOCK + j
                    b_row = row_ref[2, t, pl.ds(kx, 1), :]
                    k_row = row_ref[3, t, pl.ds(kx, 1), :]
                    for i in range(V_GROUP):
                        s_ref[vb0 + i, kx] = s_ref[vb0 + i, kx] + sa8[i] * b_row + v8[i] * k_row
                return c3

            lax.fori_loop(0, n // K_BLOCK, update_keys, 0)
            for i in range(V_GROUP):
                y_ref[t, pl.ds(base + i * SUBLANE, SUBLANE), :] = q8[i] + sa8[i] * b_r + v8[i] * k_r
            return c2

        lax.fori_loop(0, n // (SUBLANE * V_GROUP), vgroup, 0)
        return carry

    lax.fori_loop(0, tt, token, 0)

    def rescale(vb, carry):
        for kx in range(n):
            s_ref[vb, kx] = s_ref[vb, kx] * plast_ref[kx:kx + 1, :]
        return carry

    lax.fori_loop(0, n // SUBLANE, rescale, 0)

    y = y_ref[...]
    mean = jnp.sum(y, axis=1, keepdims=True) * inv_n
    yc = y - mean
    var = jnp.sum(yc * yc, axis=1, keepdims=True) * inv_n
    o_ref[...] = yc * lax.rsqrt(var + GN_EPS) * gn_w + gn_b + dot_ref[2] * v_ref[...]

    @pl.when(pl.program_id(1) == pl.num_programs(1) - 1)
    def _():
        sf_ref[...] = s_ref[...]


def _cast_rows(rows, nsteps):
    blk = _round_up(-(-rows // nsteps), 2 * SUBLANE)
    while rows % blk:
        blk += 2 * SUBLANE
    return blk, rows // blk


def _wkv(r, k, v, d, a, s0, k_k, k_a, r_k, gn_w, gn_b, *, tt, group_major, cast=()):
    if group_major:
        g, t, n, lanes = r.shape
    else:
        t, g, n, lanes = r.shape
    tt = _pick(t, tt, 1)
    nb = n // SUBLANE
    assert not cast or g == 1
    cast_specs, gain_specs, cast_steps = [], [], []
    for w, gain in cast:
        blk, steps = _cast_rows(w.shape[0], t // tt)
        cast_steps.append(steps)
        index = lambda i, j, last=steps - 1: (jnp.minimum(j, last), 0)
        cast_specs.append(pl.BlockSpec((blk, w.shape[1]), index))
        if gain is not None:
            gain_specs.append(pl.BlockSpec((blk, 1), index))
    weights = [w for w, _ in cast]
    gains = [gain.reshape(-1, 1).astype(F32) for _, gain in cast if gain is not None]
    cast_gain = tuple(gain is not None for _, gain in cast)
    if group_major:
        tok = pl.BlockSpec((None, tt, n, lanes), lambda i, j: (i, j, 0, 0))
    else:
        tok = pl.BlockSpec((tt, None, n, lanes), lambda i, j: (j, i, 0, 0))
    st = pl.BlockSpec((None, nb, n, SUBLANE, lanes), lambda i, j: (i, 0, 0, 0, 0))
    gp = k_k.shape[0]
    par = pl.BlockSpec((None, n, lanes), (lambda i, j: (i, 0, 0)) if gp > 1 else (lambda i, j: (0, 0, 0)))
    y, s_fin, *cast_out = pl.pallas_call(
        functools.partial(_wkv_kernel, tt=tt, n=n, cast_steps=tuple(cast_steps), cast_gain=cast_gain),
        grid=(g, t // tt),
        in_specs=[tok] * 5 + [st] + [par] * 5 + cast_specs + gain_specs,
        out_specs=[tok, st] + cast_specs,
        out_shape=[jax.ShapeDtypeStruct(r.shape, F32),
                   jax.ShapeDtypeStruct((g, nb, n, SUBLANE, lanes), F32)]
                  + [jax.ShapeDtypeStruct(w.shape, BF16) for w in weights],
        scratch_shapes=[pltpu.VMEM((nb, n, SUBLANE, lanes), F32), pltpu.VMEM((tt, n, lanes), F32),
                        pltpu.VMEM((4, tt, n, lanes), F32), pltpu.VMEM((3, tt, 1, lanes), F32),
                        pltpu.VMEM((n, lanes), F32)],
        compiler_params=_params("arbitrary", "arbitrary"),
        name="wkv_recurrence",
    )(r, k, v, d, a, s0, k_k, k_a, r_k, gn_w, gn_b, *weights, *gains)
    return y, s_fin, cast_out


def _to_chain_kernel(x_ref, o_ref, *, nb, nh, n, tt):
    for kb in range(n // SUBLANE):
        ks = slice(kb * SUBLANE, (kb + 1) * SUBLANE)
        per_key = jnp.concatenate([jnp.swapaxes(x_ref[b, :, ks, :], 0, 1) for b in range(nb)], axis=1)
        chains_last = jnp.swapaxes(per_key, 1, 2)
        o_ref[:, ks, :] = jnp.swapaxes(chains_last, 0, 1)


def _to_chain(xt, *, nb, nh):
    tiles, c, tt = xt.shape
    nt = tiles // nb
    n = c // nh
    return pl.pallas_call(
        functools.partial(_to_chain_kernel, nb=nb, nh=nh, n=n, tt=tt),
        grid=(nt,),
        in_specs=[pl.BlockSpec((nb, None, nh, n, tt), lambda j: (0, j, 0, 0, 0))],
        out_specs=pl.BlockSpec((tt, n, nb * nh), lambda j: (j, 0, 0)),
        out_shape=jax.ShapeDtypeStruct((nt * tt, n, nb * nh), F32),
        compiler_params=_params("arbitrary"),
        name="to_chain",
    )(xt.reshape(nb, nt, nh, n, tt))


def _from_chain_gate_kernel(y_ref, g_ref, o_ref, xt_ref, *, nb, nh, n, tt):
    for vb in range(n // SUBLANE):
        vs = slice(vb * SUBLANE, (vb + 1) * SUBLANE)
        chains_mid = jnp.swapaxes(jnp.swapaxes(y_ref[:, vs, :], 0, 1), 1, 2)
        for b in range(nb):
            xt_ref[b, :, vs, :] = jnp.swapaxes(chains_mid[:, b * nh:(b + 1) * nh, :], 0, 1)
    for b in range(nb):
        o_ref[b] = (xt_ref[b].reshape(nh * n, tt).T * g_ref[b]).astype(o_ref.dtype)


def _from_chain_gate(y, g, *, nb, nh, tt):
    m, c = g.shape
    t = m // nb
    n = c // nh
    nt = t // tt
    row = pl.BlockSpec((nb, tt, c), lambda j: (0, j, 0))
    out = pl.pallas_call(
        functools.partial(_from_chain_gate_kernel, nb=nb, nh=nh, n=n, tt=tt),
        grid=(nt,),
        in_specs=[pl.BlockSpec((tt, n, nb * nh), lambda j: (j, 0, 0)), row],
        out_specs=row,
        out_shape=jax.ShapeDtypeStruct((nb, t, c), BF16),
        scratch_shapes=[pltpu.VMEM((nb, nh, n, tt), F32)],
        compiler_params=_params("arbitrary"),
        name="from_chain_gate",
    )(y, g.reshape(nb, t, c))
    return out.reshape(m, c)


def _transpose_gate_kernel(y_ref, g_ref, o_ref):
    o_ref[...] = (y_ref[...].T * g_ref[...]).astype(o_ref.dtype)


def _transpose_gate(yt, g):
    tiles, c, tt = yt.shape
    return pl.pallas_call(
        _transpose_gate_kernel,
        grid=(tiles,),
        in_specs=[pl.BlockSpec((None, c, tt), lambda i: (i, 0, 0)),
                  pl.BlockSpec((tt, c), lambda i: (i, 0))],
        out_specs=pl.BlockSpec((tt, c), lambda i: (i, 0)),
        out_shape=jax.ShapeDtypeStruct((tiles * tt, c), BF16),
        compiler_params=_params("arbitrary"),
        name="transpose_gate",
    )(yt, g)


def _pool_kernel(*refs, gps, halo, tt, shift, pos0):
    u_refs = refs[:gps]
    h_ref, w_ref, sc_ref, o_ref, ext_ref = refs[gps:]
    gw = u_refs[0].shape[1]
    gb = pl.program_id(1)
    j = pl.program_id(2)

    @pl.when(j == 0)
    def _():
        ext_ref[0:halo, :] = h_ref[...]

    row = lax.broadcasted_iota(jnp.int32, (tt, 1), 0) + j * tt
    time = row if shift == 1 else row // shift

    def group(g, win):
        cols = slice(g * gw, (g + 1) * gw)
        x = u_refs[g][...]
        acc = x
        for back in range(1, win):
            lo = halo - back * shift
            acc = acc + ext_ref[lo:lo + tt, cols]
        cnt = jnp.minimum(pos0 + time + 1, win).astype(F32)
        d = acc / cnt - x
        y = jnp.dot(d.astype(BF16), w_ref[g], preferred_element_type=F32)
        o_ref[:, cols] = (y * sc_ref[:, cols]).astype(o_ref.dtype)

    for g in range(gps):
        ext_ref[halo:halo + tt, g * gw:(g + 1) * gw] = u_refs[g][...]
    if gps == len(POOL_WINDOWS):
        for g, win in enumerate(POOL_WINDOWS):
            group(g, win)
    else:
        for g, win in enumerate(POOL_WINDOWS):
            pl.when(gb == g)(functools.partial(group, 0, win))

    if tt >= halo:
        ext_ref[0:halo, :] = ext_ref[tt:tt + halo, :]


def _pool(proj, col0, hist, w_pool, scale, *, nseq, shift, tt, pos0, gps):
    m = proj.shape[0]
    cp = scale.shape[1]
    ng = len(POOL_WINDOWS)
    assert gps in (1, ng)
    gw = cp // ng
    assert col0 % gw == 0
    g0 = col0 // gw
    rows = m // nseq
    tt = _pick(rows, tt)
    nt = rows // tt
    halo = POOL_HIST * shift
    assert nt == 1 or tt >= halo
    u_specs = [pl.BlockSpec((tt, gw), lambda s, gb, j, g=g: (s * nt + j, g0 + gb * gps + g)) for g in range(gps)]
    return pl.pallas_call(
        functools.partial(_pool_kernel, gps=gps, halo=halo, tt=tt, shift=shift, pos0=pos0),
        grid=(nseq, ng // gps, nt),
        in_specs=u_specs + [pl.BlockSpec((None, halo, gps * gw), lambda s, gb, j: (s, 0, gb)),
                            pl.BlockSpec((gps, gw, gw), lambda s, gb, j: (gb, 0, 0)),
                            pl.BlockSpec((1, gps * gw), lambda s, gb, j: (0, gb))],
        out_specs=pl.BlockSpec((tt, gps * gw), lambda s, gb, j: (s * nt + j, gb)),
        out_shape=jax.ShapeDtypeStruct((m, cp), BF16),
        scratch_shapes=[pltpu.VMEM((halo + tt, gps * gw), F32)],
        compiler_params=_params("arbitrary", "arbitrary", "arbitrary"),
        name="pool_mix",
    )(*([proj] * gps), hist, w_pool, scale)


def _attend(q, k, v, scale):
    s = lax.dot_general(q, k, (((1,), (1,)), ((), ())), preferred_element_type=F32) * scale
    s = s - jnp.max(s, axis=-1, keepdims=True)
    e = jnp.exp(s)
    pr = e / jnp.sum(e, axis=-1, keepdims=True)
    return jnp.dot(pr.astype(BF16), v, preferred_element_type=F32)


def _attn_prompt_kernel(q_ref, k_ref, v_ref, o_ref, *, scale, heads, hd):
    for h in range(heads):
        cols = slice(h * hd, (h + 1) * hd)
        o = _attend(q_ref[:, cols], k_ref[:, cols].astype(BF16), v_ref[:, cols].astype(BF16), scale)
        o_ref[:, cols] = o.astype(o_ref.dtype)


def _attn_prompt(q, mk, mv, *, batch, heads):
    m, d = q.shape
    t = m // batch
    nm = mk.shape[1]
    hd = d // heads
    tq = _pick(t, 512)
    nq = t // tq
    return pl.pallas_call(
        functools.partial(_attn_prompt_kernel, scale=hd ** -0.5, heads=heads, hd=hd),
        grid=(batch, nq),
        in_specs=[pl.BlockSpec((tq, d), lambda b, i: (b * nq + i, 0)),
                  pl.BlockSpec((None, nm, d), lambda b, i: (b, 0, 0)),
                  pl.BlockSpec((None, nm, d), lambda b, i: (b, 0, 0))],
        out_specs=pl.BlockSpec((tq, d), lambda b, i: (b * nq + i, 0)),
        out_shape=jax.ShapeDtypeStruct((m, d), BF16),
        compiler_params=_params("arbitrary", "arbitrary"),
        name="attn_prompt",
    )(q, mk, mv)


def _attn_sample_kernel(q_ref, k_ref, v_ref, o_ref, *, scale, heads, hd):
    t = q_ref.shape[0]
    nm = k_ref.shape[0]
    q = jnp.concatenate([q_ref[:, h * hd:(h + 1) * hd] for h in range(heads)], axis=0)
    k = k_ref[...].reshape(nm * heads, hd)
    v = v_ref[...].reshape(nm * heads, hd)
    s = lax.dot_general(q, k, (((1,), (1,)), ((), ())), preferred_element_type=F32) * scale
    row_head = lax.broadcasted_iota(jnp.int32, s.shape, 0) // t
    col_head = lax.broadcasted_iota(jnp.int32, s.shape, 1) % heads
    s = jnp.where(row_head == col_head, s, -jnp.inf)
    s = s - jnp.max(s, axis=-1, keepdims=True)
    e = jnp.exp(s)
    pr = e / jnp.sum(e, axis=-1, keepdims=True)
    o = jnp.dot(pr, v, preferred_element_type=F32)
    for h in range(heads):
        o_ref[:, h * hd:(h + 1) * hd] = o[h * t:(h + 1) * t, :]


def _attn_sample(q, mk, mv, layer):
    b, t, d = q.shape
    _, _, nm, heads, hd = mk.shape
    kv_spec = pl.BlockSpec((None, None, nm, heads, hd), lambda i: (layer, i, 0, 0, 0))
    return pl.pallas_call(
        functools.partial(_attn_sample_kernel, scale=hd ** -0.5, heads=heads, hd=hd),
        grid=(b,),
        in_specs=[pl.BlockSpec((None, t, d), lambda i: (i, 0, 0)), kv_spec, kv_spec],
        out_specs=pl.BlockSpec((None, t, d), lambda i: (i, 0, 0)),
        out_shape=jax.ShapeDtypeStruct((b, t, d), F32),
        compiler_params=_params("arbitrary"),
        name="attn_sample",
    )(q, mk, mv)


def _ffn_up_kernel(h_ref, hp_ref, wg_ref, wv_ref, cw_ref, cb_ref, st_ref, act_ref, zt_ref, zext_ref, rstd_ref,
                   *, halo, tm, shift, recompute_halo):
    @pl.when(pl.program_id(2) == 0)
    def _():
        rstd_ref[...] = _row_rstd(h_ref[...])

    h = h_ref[...]
    rstd = rstd_ref[...]
    z = jnp.dot(h, wg_ref[...], preferred_element_type=F32) * rstd
    val = jnp.dot(h, wv_ref[...], preferred_element_type=F32) * rstd
    zprev = st_ref[...]
    if recompute_halo:
        hp = hp_ref[...]
        zhalo = jnp.dot(hp, wg_ref[...], preferred_element_type=F32) * _row_rstd(hp)
        zprev = jnp.where(pl.program_id(1) == 0, zprev, zhalo)
    zext_ref[0:halo, :] = zprev
    zext_ref[halo:halo + tm, :] = z
    zm1 = zext_ref[halo - shift:halo - shift + tm, :]
    zm2 = zext_ref[halo - 2 * shift:halo - 2 * shift + tm, :]
    zc = cb_ref[...] + cw_ref[0:1, :] * zm2 + cw_ref[1:2, :] * zm1 + cw_ref[2:3, :] * z
    act_ref[...] = (zc * _sigmoid(zc) * val).astype(act_ref.dtype)
    zt_ref[...] = zext_ref[tm:tm + halo, :]


def _ffn_up(h, wg, wv, cw, cb, state, *, nseq, shift, halo, tm, tn):
    m, d = h.shape
    _, fp = wg.shape
    rows = m // nseq
    tm = _pick(rows, tm)
    nt = rows // tm
    tn = min(tn, fp)
    assert tm >= halo and (nt == 1 or halo == SUBLANE)
    hb = tm // SUBLANE
    return pl.pallas_call(
        functools.partial(_ffn_up_kernel, halo=halo, tm=tm, shift=shift, recompute_halo=nt > 1),
        grid=(nseq, nt, pl.cdiv(fp, tn)),
        in_specs=[pl.BlockSpec((tm, d), lambda s, i, j: (s * nt + i, 0)),
                  pl.BlockSpec((SUBLANE, d), lambda s, i, j: (jnp.maximum((s * nt + i) * hb - 1, 0), 0)),
                  pl.BlockSpec((d, tn), lambda s, i, j: (0, j)),
                  pl.BlockSpec((d, tn), lambda s, i, j: (0, j)),
                  pl.BlockSpec((CONV_W, tn), lambda s, i, j: (0, j)),
                  pl.BlockSpec((1, tn), lambda s, i, j: (0, j)),
                  pl.BlockSpec((None, halo, tn), lambda s, i, j: (s, 0, j))],
        out_specs=[pl.BlockSpec((tm, tn), lambda s, i, j: (s * nt + i, j)),
                   pl.BlockSpec((halo, tn), lambda s, i, j: (s * nt + i, j))],
        out_shape=[jax.ShapeDtypeStruct((m, fp), BF16),
                   jax.ShapeDtypeStruct((nseq * nt * halo, fp), F32)],
        scratch_shapes=[pltpu.VMEM((halo + tm, tn), F32), pltpu.VMEM((tm, 1), F32)],
        compiler_params=_params("arbitrary", "arbitrary", "arbitrary"),
        name="ffn_up",
    )(h, h, wg, wv, cw, cb, state)


def kernel(x_prompt, x_sample, mem_prompt, cache_mem_k, cache_mem_v, state_shift, state_wkv, state_pool,
           state_conv, g_mix, w_in, mu_shift, w0, b_w, a0, b_a, b_g, k_k, k_a, r_k, gn_w, gn_b, w_pool,
           pool_scale, w_out, g_mem, w_mk, w_mv, g_attn, w_mq, w_mo, g_ffn, w_gate, w_val, conv_w, conv_b,
           w_down, g_final):
    bp, tp, d = x_prompt.shape
    bs, ts, _ = x_sample.shape
    depth = w_in.shape[0]
    c = w0.shape[1]
    nh = c // HEAD_SIZE
    nvb = HEAD_SIZE // SUBLANE
    cp = pool_scale.shape[1]
    pw = mu_shift.shape[1]
    lora = pw - 3 * c
    lp = _round_up(lora, LANE)
    ca = 3 * c + lp
    dl, al, gl = b_w.shape[1], b_a.shape[1], b_g.shape[1]
    f = w_gate.shape[2]
    fp = f
    nmem = mem_prompt.shape[1]
    nbuf = state_pool.shape[2]
    mp, ms = bp * tp, bs * ts

    xp = x_prompt.reshape(mp, d)
    xs = x_sample.transpose(1, 0, 2).reshape(ms, d)

    outs = {n: [] for n in ("pm_k", "pm_v", "p_sh", "p_wkv", "p_pool", "p_conv",
                            "s_sh", "s_wkv", "s_pool", "s_conv")}

    for l in range(depth):
        w_in_t = w_in[l].T.astype(BF16)
        wu_t = w_in_t[pw:]
        mu = jnp.pad(mu_shift[l], (0, ca - pw)).reshape(1, ca)
        bw = jnp.zeros((lp, c), F32).at[0:dl].set(b_w[l]).astype(BF16)
        ba = jnp.zeros((lp, c), F32).at[dl:dl + al].set(b_a[l]).astype(BF16)
        bg = jnp.zeros((lp, c), F32).at[dl + al:dl + al + gl].set(b_g[l]).astype(BF16)
        w0l = w0[l].reshape(1, c)
        a0l = a0[l].reshape(1, c)
        wpool = w_pool[l].astype(BF16)
        pscale = pool_scale[l].reshape(1, cp)
        late = {}
        cw = conv_w[l]
        cb = conv_b[l].reshape(1, fp)

        def head_tile(p):
            return p.reshape(nh, HEAD_SIZE).T

        chain_p = [jnp.tile(head_tile(p), (1, bp)).reshape(1, HEAD_SIZE, bp * nh)
                   for p in (k_k[l], k_a[l], r_k[l].reshape(-1), gn_w[l], gn_b[l])]
        chain_s = [jnp.broadcast_to(p.reshape(nh, HEAD_SIZE, 1), (nh, HEAD_SIZE, bs))
                   for p in (k_k[l], k_a[l], r_k[l].reshape(-1), gn_w[l], gn_b[l])]

        mk, mv = _norm_matmul2(mem_prompt.reshape(bp * nmem, d), g_mem[l], w_mk[l], w_mv[l], name="mem_kv")
        outs["pm_k"].append(mk.reshape(bp, nmem, MEM_HEADS, d // MEM_HEADS))
        outs["pm_v"].append(mv.reshape(bp, nmem, MEM_HEADS, d // MEM_HEADS))

        def layer(x, *, prompt):
            m = x.shape[0]
            if prompt:
                nseq, shift, halo1, halo, pos0 = bp, 1, SUBLANE, SUBLANE, 0
                st_shift = jnp.zeros((bp, halo1, ca), F32)
                hist = jnp.zeros((bp, POOL_HIST, cp), F32)
                st_conv = jnp.zeros((bp, halo, fp), F32)
            else:
                nseq, shift, halo1, halo, pos0 = 1, bs, bs, (CONV_W - 1) * bs, PAST_LEN
                st_shift = jnp.pad(state_shift[l], ((0, 0), (0, ca - pw))).reshape(1, halo1, ca)
                hist = jnp.pad(state_pool[l].transpose(1, 0, 2), ((POOL_HIST - nbuf, 0), (0, 0), (0, 0)))
                hist = hist.reshape(1, POOL_HIST * bs, cp)
                st_conv = state_conv[l].transpose(1, 0, 2).reshape(1, halo, fp)

            xn = _rmsnorm(x, g_mix[l], BF16)
            proj = _matmul_cols2(xn, w_in_t, ca, wu_t, name="in_proj")
            tile = min(LANE, tp) if prompt else bs
            *rkvda, g = _rwkv_prep(proj, st_shift, mu, w0l, a0l, bw, ba, bg,
                                   nseq=nseq, shift=shift, halo=halo1, tt=tile, c=c, ca=ca)
            if prompt:
                chains = [_to_chain(x, nb=bp, nh=nh).reshape(1, tp, HEAD_SIZE, bp * nh) for x in rkvda]
                s0 = jnp.zeros((1, nvb, HEAD_SIZE, SUBLANE, bp * nh), F32)
                y, s_fin, cast_out = _wkv(*chains, s0, *chain_p, tt=16, group_major=True,
                                          cast=((w_out[l], None), (w_mq[l], g_attn[l]), (w_mo[l], None),
                                                (w_gate[l], g_ffn[l]), (w_val[l], g_ffn[l]),
                                                (w_down[l], None)))
                late.update(zip(("wo", "wmq", "wmo", "wg", "wv", "wd"), cast_out))
                r_out = _from_chain_gate(y.reshape(tp, HEAD_SIZE, bp * nh), g, nb=bp, nh=nh, tt=tile)
                new_wkv = s_fin.reshape(nvb, HEAD_SIZE, SUBLANE, bp, nh).transpose(3, 4, 0, 2, 1)
                new_wkv = new_wkv.reshape(bp, nh, HEAD_SIZE, HEAD_SIZE)
            else:
                chains = [x.reshape(ts, nh, HEAD_SIZE, bs) for x in rkvda]
                s0 = state_wkv[l].reshape(bs, nh, nvb, SUBLANE, HEAD_SIZE).transpose(1, 2, 4, 3, 0)
                y, s_fin, _ = _wkv(*chains, s0, *chain_s, tt=ts, group_major=False)
                r_out = _transpose_gate(y.reshape(ts, c, bs), g)
                new_wkv = s_fin.transpose(4, 0, 1, 3, 2).reshape(bs, nh, HEAD_SIZE, HEAD_SIZE)
            p_out = _pool(proj, ca, hist, wpool, pscale, nseq=nseq, shift=shift,
                          tt=256 if prompt else m, pos0=pos0, gps=len(POOL_WINDOWS) if prompt else 1)
            wmq, wmo, wg, wv, wd = (late[n] for n in ("wmq", "wmo", "wg", "wv", "wd"))
            x1, x1b = _matmul2_res(r_out, p_out, late["wo"], x)

            if prompt:
                q = _matmul(x1b, wmq, out_dtype=BF16, row_norm=True, name="attn_q")
                o = _attn_prompt(q, mk.reshape(bp, nmem, d), mv.reshape(bp, nmem, d),
                                 batch=bp, heads=MEM_HEADS)
            else:
                q = _matmul(x1b, wmq, out_dtype=F32, row_norm=True, name="attn_q")
                q = q.reshape(ts, bs, d).transpose(1, 0, 2)
                o = _attn_sample(q, cache_mem_k, cache_mem_v, l)
                o = o.transpose(1, 0, 2).reshape(m, d)
            x2, x2b = _matmul(o, wmo, out_dtype=F32, res=x1, emit_bf16=True, name="attn_o")

            act, ztail = _ffn_up(x2b, wg, wv, cw, cb, st_conv, nseq=nseq, shift=shift, halo=halo,
                                 tm=1024, tn=512)
            x3 = _matmul(act, wd, out_dtype=F32, res=x2, tm=512, tn=512, name="ffn_down")

            if prompt:
                new_shift = proj.reshape(bp, tp, ca + cp)[:, -1, :pw]
                new_pool = proj.reshape(bp, tp, ca + cp)[:, tp - nbuf:, ca:]
                new_conv = ztail.reshape(bp, -1, halo, fp)[:, -1, halo - (CONV_W - 1):, :f]
            else:
                new_shift = proj[(ts - 1) * bs:, :pw]
                u_b = proj[:, ca:].reshape(ts, bs, cp).transpose(1, 0, 2)
                new_pool = jnp.concatenate([state_pool[l], u_b], axis=1)[:, -nbuf:]
                new_conv = ztail.reshape(CONV_W - 1, bs, fp)[:, :, :f].transpose(1, 0, 2)
            return x3, new_shift, new_wkv, new_pool, new_conv

        xp, sh, wk, po, co = layer(xp, prompt=True)
        outs["p_sh"].append(sh)
        outs["p_wkv"].append(wk)
        outs["p_pool"].append(po)
        outs["p_conv"].append(co)
        xs, sh, wk, po, co = layer(xs, prompt=False)
        outs["s_sh"].append(sh)
        outs["s_wkv"].append(wk)
        outs["s_pool"].append(po)
        outs["s_conv"].append(co)

    y_prompt = _rmsnorm(xp, g_final, F32).reshape(bp, tp, d)
    y_sample = _rmsnorm(xs, g_final, F32).reshape(ts, bs, d).transpose(1, 0, 2)
    stk = lambda n: jnp.stack(outs[n])
    return (y_prompt, y_sample, stk("pm_k"), stk("pm_v"), stk("p_sh"), stk("p_wkv"), stk("p_pool"),
            stk("p_conv"), stk("s_sh"), stk("s_wkv"), stk("s_pool"), stk("s_conv"))
```
